```python
import math
import jax, jax.numpy as jnp
from jax import lax
import numpy as np

D_MODEL = 1024
BATCH = 2
SEQ = 8192
DEPTH = 1

MOBA_HEAD_DIM = 64
MOBA_HEADS = (D_MODEL // 2) // MOBA_HEAD_DIM
MOBA_WIDTH = MOBA_HEADS * MOBA_HEAD_DIM
MOBA_BLOCK = 256
MOBA_TOPK = 3
Q_BLOCK = 128
GLA_HEADS = 4
GLA_VAL_DIM = (D_MODEL // 2) // GLA_HEADS
GLA_KEY_DIM = GLA_VAL_DIM // 2
GLA_VWIDTH = GLA_HEADS * GLA_VAL_DIM
GLA_KWIDTH = GLA_HEADS * GLA_KEY_DIM
GLA_GATE_RANK = 16
GLA_GATE_TAU = 16.0
GLA_CHUNK = 64
MIX_WIDTH = MOBA_WIDTH + GLA_VWIDTH
IN_COLS = 3 * MOBA_WIDTH + 2 * GLA_KWIDTH + 2 * GLA_VWIDTH + GLA_GATE_RANK
D_FF = 256 * ((8 * D_MODEL // 3 + 255) // 256)
CONV_WIDTH = 3
ROPE_THETA = 10000.0
EPS = 1e-6

kernel_name = "hymba_moba_gla_convglu"


def rms_norm(x, g):
    xf = x.astype(jnp.float32)
    y = xf * lax.rsqrt(jnp.mean(xf * xf, axis=-1, keepdims=True) + EPS)
    return (y * g.astype(jnp.float32)).astype(x.dtype)


def rope(t, pos):
    hd = t.shape[-1]
    inv_freq = 1.0 / (ROPE_THETA ** (jnp.arange(0, hd, 2, dtype=jnp.float32) / hd))
    ang = pos.astype(jnp.float32)[:, None] * inv_freq[None, :]
    cos = jnp.cos(ang).astype(t.dtype)
    sin = jnp.sin(ang).astype(t.dtype)
    t1, t2 = t[..., : hd // 2], t[..., hd // 2:]
    return jnp.concatenate([t1 * cos - t2 * sin, t2 * cos + t1 * sin], axis=-1)


def moba_attention(q, k, v):
    B, H, S, hd = q.shape
    nb = -(-S // MOBA_BLOCK)
    n_sel = min(MOBA_TOPK, nb)
    pad = nb * MOBA_BLOCK - S
    padw = ((0, 0), (0, 0), (0, pad), (0, 0))
    kp = jnp.pad(k, padw).reshape(B, H, nb, MOBA_BLOCK, hd)
    vp = jnp.pad(v, padw).reshape(B, H, nb, MOBA_BLOCK, hd)
    k_mean = jnp.mean(kp, axis=3)
    q = q * (hd ** -0.5)
    n_qb = S // Q_BLOCK
    gather = jax.vmap(jax.vmap(lambda blocks, idx: blocks[idx]))
    block_ids = jnp.arange(nb)

    def one_query_block(c):
        q0 = c * Q_BLOCK
        qc = lax.dynamic_slice_in_dim(q, q0, Q_BLOCK, axis=2)
        own = q0 // MOBA_BLOCK
        qpos = q0 + jnp.arange(Q_BLOCK)
        gate = jnp.einsum('bhqd,bhnd->bhqn', qc, k_mean).astype(jnp.float32)
        gate = jnp.where((block_ids < own)[None, None, None, :], gate, -jnp.inf)
        top_s, top_i = lax.top_k(gate, n_sel)
        valid = top_s > -jnp.inf
        kg = gather(kp, top_i)
        vg = gather(vp, top_i)
        s_sel = jnp.einsum('bhqd,bhqnkd->bhqnk', qc, kg).astype(jnp.float32)
        s_sel = jnp.where(valid[..., None], s_sel, -jnp.inf)
        s_sel = s_sel.reshape(B, H, Q_BLOCK, n_sel * MOBA_BLOCK)
        k_own = lax.dynamic_index_in_dim(kp, own, axis=2, keepdims=False)
        v_own = lax.dynamic_index_in_dim(vp, own, axis=2, keepdims=False)
        kpos = own * MOBA_BLOCK + jnp.arange(MOBA_BLOCK)
        s_own = jnp.einsum('bhqd,bhkd->bhqk', qc, k_own).astype(jnp.float32)
        s_own = jnp.where(qpos[:, None] >= kpos[None, :], s_own, -jnp.inf)
        p = jax.nn.softmax(jnp.concatenate([s_sel, s_own], axis=-1), axis=-1).astype(v.dtype)
        p_sel = p[..., : n_sel * MOBA_BLOCK].reshape(B, H, Q_BLOCK, n_sel, MOBA_BLOCK)
        p_own = p[..., n_sel * MOBA_BLOCK:]
        return (jnp.einsum('bhqnk,bhqnkd->bhqd', p_sel, vg)
                + jnp.einsum('bhqk,bhkd->bhqd', p_own, v_own))

    out = lax.map(one_query_block, jnp.arange(n_qb))
    return out.transpose(1, 2, 0, 3, 4).reshape(B, H, S, hd)


def gla_attention(q, k, v, log_a):
    B, H, S, dk = q.shape
    dv = v.shape[-1]
    C = GLA_CHUNK
    nc = S // C
    f32 = jnp.float32

    def to_chunks(t):
        return t.astype(f32).reshape(B, H, nc, C, t.shape[-1]).transpose(2, 0, 1, 3, 4)

    qc = to_chunks(q * (dk ** -0.5))
    kc, vc = to_chunks(k), to_chunks(v)
    G = jnp.cumsum(to_chunks(log_a), axis=3)
    causal = jnp.tril(jnp.ones((C, C), dtype=bool))

    def step(state, inp):
        q_, k_, v_, G_ = inp
        diff = G_[:, :, :, None, :] - G_[:, :, None, :, :]
        decay = jnp.exp(jnp.where(causal[None, None, :, :, None], diff, -jnp.inf))
        A = jnp.einsum('bhtd,bhsd,bhtsd->bhts', q_, k_, decay)
        o = (jnp.einsum('bhts,bhsv->bhtv', A, v_)
             + jnp.einsum('bhtd,bhdv->bhtv', q_ * jnp.exp(G_), state))
        G_last = G_[:, :, -1:, :]
        state = (jnp.exp(G_last)[:, :, 0, :, None] * state
                 + jnp.einsum('bhsd,bhsv->bhdv', k_ * jnp.exp(G_last - G_), v_))
        return state, o

    state0 = jnp.zeros((B, H, dk, dv), f32)
    _, o = lax.scan(step, state0, (qc, kc, vc, G))
    return o.transpose(1, 2, 0, 3, 4).reshape(B, H, S, dv)


def causal_dwconv(h, w, b):
    S = h.shape[1]
    hp = jnp.pad(h, ((0, 0), (CONV_WIDTH - 1, 0), (0, 0)))
    return sum(w[j] * hp[:, j:j + S] for j in range(CONV_WIDTH)) + b


def split_heads(t, n_heads):
    B, S, W = t.shape
    return t.reshape(B, S, n_heads, W // n_heads).transpose(0, 2, 1, 3)


def merge_heads(t):
    B, H, S, d = t.shape
    return t.transpose(0, 2, 1, 3).reshape(B, S, H * d)


def setup_inputs(seed: int = 0) -> dict:
    key = jax.random.key(seed)
    ks = jax.random.split(key, 16)
    f32 = jnp.float32
    D, L = D_MODEL, DEPTH
    nrm = lambda k, shape, fan: jax.random.normal(k, shape, f32) * (fan ** -0.5)
    return {
        "x": jax.random.normal(ks[0], (BATCH, SEQ, D), f32),
        "attn_norm_g": 1.0 + 0.02 * jax.random.normal(ks[1], (L, D), f32),
        "w_in": nrm(ks[2], (L, D, IN_COLS), D),
        "w_gate_up": nrm(ks[3], (L, GLA_GATE_RANK, GLA_KWIDTH), GLA_GATE_RANK),
        "b_gate": 0.1 * jax.random.normal(ks[4], (L, GLA_KWIDTH), f32),
        "gla_norm_g": 1.0 + 0.02 * jax.random.normal(ks[5], (L, GLA_HEADS, GLA_VAL_DIM), f32),
        "w_out": nrm(ks[6], (L, MIX_WIDTH, D), MIX_WIDTH),
        "ffn_norm_g": 1.0 + 0.02 * jax.random.normal(ks[7], (L, D), f32),
        "w_ffn_up": nrm(ks[8], (L, D, 2 * D_FF), D),
        "conv_w": nrm(ks[9], (L, CONV_WIDTH, 2 * D_FF), CONV_WIDTH),
        "conv_b": 0.02 * jax.random.normal(ks[10], (L, 2 * D_FF), f32),
        "w_ffn_down": nrm(ks[11], (L, D_FF, D), D_FF),
        "final_norm_g": 1.0 + 0.02 * jax.random.normal(ks[12], (D,), f32),
    }


def reference(x, attn_norm_g, w_in, w_gate_up, b_gate, gla_norm_g, w_out,
              ffn_norm_g, w_ffn_up, conv_w, conv_b, w_ffn_down, final_norm_g):
    B, S, _ = x.shape
    pos = jnp.arange(S)
    o_mq = 0
    o_mk = o_mq + MOBA_WIDTH
    o_mv = o_mk + MOBA_WIDTH
    o_gq = o_mv + MOBA_WIDTH
    o_gk = o_gq + GLA_KWIDTH
    o_gv = o_gk + GLA_KWIDTH
    o_gr = o_gv + GLA_VWIDTH
    o_gg = o_gr + GLA_VWIDTH
    for l in range(DEPTH):
        xn = rms_norm(x, attn_norm_g[l])
        proj = xn @ w_in[l]
        mq = rope(split_heads(proj[..., o_mq:o_mk], MOBA_HEADS), pos)
        mk = rope(split_heads(proj[..., o_mk:o_mv], MOBA_HEADS), pos)
        mv = split_heads(proj[..., o_mv:o_gq], MOBA_HEADS)
        y_moba = merge_heads(moba_attention(mq, mk, mv))
        gq = split_heads(proj[..., o_gq:o_gk], GLA_HEADS)
        gk = split_heads(proj[..., o_gk:o_gv], GLA_HEADS)
        gv = split_heads(proj[..., o_gv:o_gr], GLA_HEADS)
        gr = proj[..., o_gr:o_gg]
        gate_lr = proj[..., o_gg:]
        log_a = jax.nn.log_sigmoid((gate_lr @ w_gate_up[l] + b_gate[l]).astype(jnp.float32)) / GLA_GATE_TAU
        og = gla_attention(gq, gk, gv, split_heads(log_a, GLA_HEADS))
        og = og * lax.rsqrt(jnp.mean(og * og, axis=-1, keepdims=True) + EPS)
        og = og * gla_norm_g[l].astype(jnp.float32)[None, :, None, :]
        y_gla = merge_heads(og).astype(x.dtype) * jax.nn.silu(gr)
        x = x + jnp.concatenate([y_moba, y_gla], axis=-1) @ w_out[l]
        hn = rms_norm(x, ffn_norm_g[l])
        h = causal_dwconv(hn @ w_ffn_up[l], conv_w[l], conv_b[l])
        h_gate, h_val = h[..., :D_FF], h[..., D_FF:]
        x = x + (jax.nn.silu(h_gate) * h_val) @ w_ffn_down[l]
    return rms_norm(x, final_norm_g)
```

```python
import functools
import math

import jax
import jax.numpy as jnp
from jax import lax
from jax.experimental import pallas as pl
from jax.experimental.pallas import tpu as pltpu

D_MODEL = 1024
MOBA_HEAD_DIM = 64
MOBA_HEADS = 8
MOBA_WIDTH = 512
MOBA_BLOCK = 256
MOBA_TOPK = 3
GLA_HEADS = 4
GLA_VAL_DIM = 128
GLA_KEY_DIM = 64
GLA_VWIDTH = 512
GLA_KWIDTH = 256
GLA_GATE_RANK = 16
GLA_GATE_TAU = 16.0
GLA_CHUNK = 64
GLA_SUB = 16
D_FF = 2816
CONV_WIDTH = 3
ROPE_THETA = 10000.0
EPS = 1e-6

LANES = 128
NEG_BIG = -1e30
EXP_CAP = 80.0

PROJ_TILE = 512
GLA_TILE = 512
FFN_TILE = 256
VMEM_LIMIT = 56 * 1024 * 1024

f32 = jnp.float32
bf16 = jnp.bfloat16


def _dot(a, b):
    return jnp.dot(a, b, preferred_element_type=f32)


def _dot_nt(a, b):
    return lax.dot_general(a, b, (((1,), (1,)), ((), ())), preferred_element_type=f32)


def _dot_tn(a, b):
    return lax.dot_general(a, b, (((0,), (0,)), ((), ())), preferred_element_type=f32)


def _rms(xf, g):
    return xf * lax.rsqrt(jnp.mean(xf * xf, axis=-1, keepdims=True) + EPS) * g


def _proj_kernel(x_ref, g_ref, wqt_ref, wk_ref, wvt_ref, wg_ref, wgu_ref, bg_ref,
                 cos_t_ref, sin_t_ref, cos_r_ref, sin_r_ref,
                 qt_ref, k_ref, vt_ref, kmean_ref, gq_ref, gk_ref, gv_ref, gr_ref, la_ref):
    T = x_ref.shape[1]
    xn = _rms(x_ref[0], g_ref[...]).astype(bf16)

    qt = _dot_nt(wqt_ref[...], xn)
    cos_t = cos_t_ref[...]
    sin_t = sin_t_ref[...]
    half = MOBA_HEAD_DIM // 2
    scale = MOBA_HEAD_DIM ** -0.5
    for h in range(MOBA_HEADS):
        r0 = h * MOBA_HEAD_DIM
        t1 = qt[r0:r0 + half]
        t2 = qt[r0 + half:r0 + MOBA_HEAD_DIM]
        qt_ref[0, r0:r0 + half, :] = ((t1 * cos_t - t2 * sin_t) * scale).astype(bf16)
        qt_ref[0, r0 + half:r0 + MOBA_HEAD_DIM, :] = ((t2 * cos_t + t1 * sin_t) * scale).astype(bf16)

    k = _dot(xn, wk_ref[...])
    cos_r = cos_r_ref[...]
    sin_r = sin_r_ref[...]
    lane = lax.broadcasted_iota(jnp.int32, (T, LANES), 1)
    first_half = (lane % MOBA_HEAD_DIM) < half
    nblk = T // MOBA_BLOCK
    for p in range(MOBA_WIDTH // LANES):
        kp = k[:, p * LANES:(p + 1) * LANES]
        rot = jnp.where(first_half, pltpu.roll(kp, LANES - half, 1), pltpu.roll(kp, half, 1))
        kr = kp * cos_r + rot * sin_r
        k_ref[0, :, p * LANES:(p + 1) * LANES] = kr.astype(bf16)
        for j in range(nblk):
            kmean_ref[0, 0, j:j + 1, p * LANES:(p + 1) * LANES] = jnp.mean(
                kr[j * MOBA_BLOCK:(j + 1) * MOBA_BLOCK], axis=0, keepdims=True)

    vt = _dot_nt(wvt_ref[...], xn).astype(bf16)
    for j in range(nblk):
        vt_ref[0, j] = vt[:, j * MOBA_BLOCK:(j + 1) * MOBA_BLOCK]

    pg = _dot(xn, wg_ref[...])
    o_gk = GLA_KWIDTH
    o_gv = o_gk + GLA_KWIDTH
    o_gr = o_gv + GLA_VWIDTH
    o_gg = o_gr + GLA_VWIDTH
    gq_ref[0] = pg[:, :o_gk] * (GLA_KEY_DIM ** -0.5)
    gk_ref[0] = pg[:, o_gk:o_gv]
    gv_ref[0] = pg[:, o_gv:o_gr]
    gr_ref[0] = pg[:, o_gr:o_gg]
    gate_lr = pg[:, o_gg:]
    z = jnp.dot(gate_lr, wgu_ref[...], preferred_element_type=f32,
                precision=lax.Precision.HIGHEST) + bg_ref[...]
    log_sig = jnp.minimum(z, 0.0) - jnp.log1p(jnp.exp(-jnp.abs(z)))
    la_ref[0] = log_sig * (1.0 / GLA_GATE_TAU)


def _proj_call(x, g, wqt, wk, wvt, wg, wgu, bg, cos_t, sin_t, cos_r, sin_r):
    B, S, D = x.shape
    T = PROJ_TILE
    nb_t = T // MOBA_BLOCK
    const = lambda shape: pl.BlockSpec(shape, lambda b, i: (0,) * len(shape))
    tok = lambda w: pl.BlockSpec((1, T, w), lambda b, i: (b, i, 0))
    out_shape = (
        jax.ShapeDtypeStruct((B, MOBA_WIDTH, S), bf16),
        jax.ShapeDtypeStruct((B, S, MOBA_WIDTH), bf16),
        jax.ShapeDtypeStruct((B, S // MOBA_BLOCK, MOBA_WIDTH, MOBA_BLOCK), bf16),
        jax.ShapeDtypeStruct((B, S // T, nb_t, MOBA_WIDTH), f32),
        jax.ShapeDtypeStruct((B, S, GLA_KWIDTH), f32),
        jax.ShapeDtypeStruct((B, S, GLA_KWIDTH), f32),
        jax.ShapeDtypeStruct((B, S, GLA_VWIDTH), f32),
        jax.ShapeDtypeStruct((B, S, GLA_VWIDTH), f32),
        jax.ShapeDtypeStruct((B, S, GLA_KWIDTH), f32),
    )
    out_specs = (
        pl.BlockSpec((1, MOBA_WIDTH, T), lambda b, i: (b, 0, i)),
        tok(MOBA_WIDTH),
        pl.BlockSpec((1, nb_t, MOBA_WIDTH, MOBA_BLOCK), lambda b, i: (b, i, 0, 0)),
        pl.BlockSpec((1, 1, nb_t, MOBA_WIDTH), lambda b, i: (b, i, 0, 0)),
        tok(GLA_KWIDTH), tok(GLA_KWIDTH), tok(GLA_VWIDTH), tok(GLA_VWIDTH), tok(GLA_KWIDTH),
    )
    in_specs = [
        tok(D),
        const((1, D)),
        const(wqt.shape), const(wk.shape), const(wvt.shape), const(wg.shape),
        const(wgu.shape), const((1, GLA_KWIDTH)),
        pl.BlockSpec((MOBA_HEAD_DIM // 2, T), lambda b, i: (0, i)),
        pl.BlockSpec((MOBA_HEAD_DIM // 2, T), lambda b, i: (0, i)),
        pl.BlockSpec((T, LANES), lambda b, i: (i, 0)),
        pl.BlockSpec((T, LANES), lambda b, i: (i, 0)),
    ]
    return pl.pallas_call(
        _proj_kernel,
        grid=(B, S // T),
        in_specs=in_specs,
        out_specs=out_specs,
        out_shape=out_shape,
        compiler_params=pltpu.CompilerParams(
            dimension_semantics=("parallel", "parallel"), vmem_limit_bytes=VMEM_LIMIT),
        name="in_proj",
    )(x, g, wqt, wk, wvt, wg, wgu, bg, cos_t, sin_t, cos_r, sin_r)


def _moba_kernel(qt_ref, k_ref, vt_ref, kmean_ref, o_ref, bias_ref):
    qi = pl.program_id(2)
    QB = qt_ref.shape[2]
    NB = kmean_ref.shape[1]
    HD = MOBA_HEAD_DIM

    km = kmean_ref[0]
    km_hi = km.astype(bf16)
    km_lo = (km - km_hi.astype(f32)).astype(bf16)
    blk = lax.broadcasted_iota(jnp.int32, (NB, QB), 0)
    kpos = lax.broadcasted_iota(jnp.int32, (MOBA_BLOCK, QB), 0)
    qpos = lax.broadcasted_iota(jnp.int32, (MOBA_BLOCK, QB), 1)
    causal = kpos <= qpos
    zeros_half = jnp.zeros((HD, QB), bf16)

    outs = []
    for h in range(2):
        q_h = qt_ref[0, h * HD:(h + 1) * HD, :]
        qz = jnp.concatenate([q_h, zeros_half] if h == 0 else [zeros_half, q_h], axis=0)

        gate = _dot(km_hi, qz) + _dot(km_lo, qz)
        gate = jnp.where(blk < qi, gate, -jnp.inf)
        sel = jnp.zeros((NB, QB), jnp.bool_)
        for _ in range(MOBA_TOPK):
            top = jnp.max(gate, axis=0, keepdims=True)
            idx = jnp.min(jnp.where(gate == top, blk, NB), axis=0, keepdims=True)
            pick = blk == idx
            sel = jnp.logical_or(sel, jnp.logical_and(pick, top > -jnp.inf))
            gate = jnp.where(pick, -jnp.inf, gate)
        bias_ref[h] = jnp.where(sel, 0.0, NEG_BIG)

        k_own = k_ref[0, pl.ds(pl.multiple_of(qi * MOBA_BLOCK, MOBA_BLOCK), MOBA_BLOCK), :]
        s = jnp.where(causal, _dot(k_own, qz), NEG_BIG)
        m0 = jnp.max(s, axis=0, keepdims=True)
        p = jnp.exp(s - m0)
        l0 = jnp.sum(p, axis=0, keepdims=True)
        acc0 = _dot(vt_ref[0, qi, h * HD:(h + 1) * HD, :], p.astype(bf16))

        def body(j, carry, qz=qz, h=h):
            m, l, acc = carry
            k_j = k_ref[0, pl.ds(pl.multiple_of(j * MOBA_BLOCK, MOBA_BLOCK), MOBA_BLOCK), :]
            s = _dot(k_j, qz) + bias_ref[h, pl.ds(j, 1), :]
            m_new = jnp.maximum(m, jnp.max(s, axis=0, keepdims=True))
            alpha = jnp.exp(m - m_new)
            p = jnp.exp(s - m_new)
            l = alpha * l + jnp.sum(p, axis=0, keepdims=True)
            acc = alpha * acc + _dot(vt_ref[0, j, h * HD:(h + 1) * HD, :], p.astype(bf16))
            return m_new, l, acc

        _, l, acc = lax.fori_loop(0, qi, body, (m0, l0, acc0))
        outs.append(acc / l)

    o_t = jnp.concatenate(outs, axis=0)
    o_ref[0] = o_t.T.astype(o_ref.dtype)


def _moba_call(qt, k, vt, kmean):
    B, W, S = qt.shape
    NB = S // MOBA_BLOCK
    QB = MOBA_BLOCK
    n_hp = W // LANES
    return pl.pallas_call(
        _moba_kernel,
        grid=(B, n_hp, S // QB),
        in_specs=[
            pl.BlockSpec((1, LANES, QB), lambda b, hp, i: (b, hp, i)),
            pl.BlockSpec((1, S, LANES), lambda b, hp, i: (b, 0, hp)),
            pl.BlockSpec((1, NB, LANES, MOBA_BLOCK), lambda b, hp, i: (b, 0, hp, 0)),
            pl.BlockSpec((1, NB, LANES), lambda b, hp, i: (b, 0, hp)),
        ],
        out_specs=pl.BlockSpec((1, QB, LANES), lambda b, hp, i: (b, i, hp)),
        out_shape=jax.ShapeDtypeStruct((B, S, W), bf16),
        scratch_shapes=[pltpu.VMEM((2, NB, QB), f32)],
        compiler_params=pltpu.CompilerParams(
            dimension_semantics=("parallel", "parallel", "arbitrary"),
            vmem_limit_bytes=VMEM_LIMIT),
        name="moba",
    )(qt, k, vt, kmean)


def _split3(a):
    hi = a.astype(bf16)
    r1 = a - hi.astype(f32)
    mid = r1.astype(bf16)
    lo = (r1 - mid.astype(f32)).astype(bf16)
    return hi, mid, lo


def _gla_kernel(gq_ref, gk_ref, gv_ref, gr_ref, la_ref, gn_ref, y_ref, st_ref):
    C = GLA_CHUNK
    H = GLA_HEADS
    KW = GLA_KWIDTH
    DV = GLA_VAL_DIM
    NSUB = C // GLA_SUB
    T = gq_ref.shape[1]

    @pl.when(pl.program_id(1) == 0)
    def _():
        st_ref[...] = jnp.zeros_like(st_ref)

    row = lax.broadcasted_iota(jnp.int32, (C, C), 0)
    col = lax.broadcasted_iota(jnp.int32, (C, C), 1)
    tril = (col <= row).astype(bf16)
    lane_head = lax.broadcasted_iota(jnp.int32, (C, KW), 1) // GLA_KEY_DIM
    rt = lax.broadcasted_iota(jnp.int32, (C, NSUB * C), 0)
    rc = lax.broadcasted_iota(jnp.int32, (C, NSUB * C), 1)
    keep = jnp.logical_and(rc // C == rt // GLA_SUB, rc % C <= rt)
    st_lane_head = lax.broadcasted_iota(jnp.int32, (DV, KW), 1) // GLA_KEY_DIM

    for c in range(T // C):
        rows = slice(c * C, (c + 1) * C)
        q = gq_ref[0, rows, :]
        k = gk_ref[0, rows, :]
        v = gv_ref[0, rows, :].astype(bf16)
        la = la_ref[0, rows, :]

        hi, mid, lo = _split3(la)
        G = _dot(tril, hi) + _dot(tril, mid) + _dot(tril, lo)
        g_last = G[C - 1:C, :]

        g_ref_rows = [G[i * GLA_SUB:i * GLA_SUB + 1, :] for i in range(NSUB)]
        g_own = jnp.concatenate(
            [jnp.broadcast_to(g, (GLA_SUB, KW)) for g in g_ref_rows], axis=0)
        q_in = q * jnp.exp(G - g_own)
        k_in = jnp.concatenate(
            [(k * jnp.exp(jnp.minimum(g - G, EXP_CAP))).astype(bf16) for g in g_ref_rows],
            axis=0)
        q_st = q * jnp.exp(G)
        k_st = (k * jnp.exp(g_last - G)).astype(bf16)

        qz_in = jnp.concatenate(
            [jnp.where(lane_head == h, q_in, 0.0).astype(bf16) for h in range(H)], axis=0)
        qz_st = jnp.concatenate(
            [jnp.where(lane_head == h, q_st, 0.0).astype(bf16) for h in range(H)], axis=0)

        r = _dot_nt(qz_in, k_in)
        keep_all = jnp.concatenate([keep] * H, axis=0)
        r = jnp.where(keep_all, r, 0.0).astype(bf16)
        v_rep = jnp.concatenate([v] * NSUB, axis=0)
        o_intra = _dot(r, v_rep)

        st = st_ref[...]
        o_inter = _dot_nt(qz_st, st.astype(bf16))

        upd = _dot_tn(v, k_st)
        new_st = st * jnp.exp(g_last)
        for h in range(H):
            new_st = new_st + jnp.where(st_lane_head == h, upd[h * DV:(h + 1) * DV, :], 0.0)
        st_ref[...] = new_st

        for h in range(H):
            o = o_intra[h * C:(h + 1) * C, h * DV:(h + 1) * DV] + o_inter[h * C:(h + 1) * C, :]
            o = o * lax.rsqrt(jnp.mean(o * o, axis=-1, keepdims=True) + EPS)
            o = o * gn_ref[:, h * DV:(h + 1) * DV]
            gr = gr_ref[0, rows, h * DV:(h + 1) * DV]
            y = o * (gr * jax.nn.sigmoid(gr))
            y_ref[0, rows, h * DV:(h + 1) * DV] = y.astype(y_ref.dtype)


def _gla_call(gq, gk, gv, gr, la, gn):
    B, S, _ = gq.shape
    T = GLA_TILE
    tok = lambda w: pl.BlockSpec((1, T, w), lambda b, i: (b, i, 0))
    return pl.pallas_call(
        _gla_kernel,
        grid=(B, S // T),
        in_specs=[tok(GLA_KWIDTH), tok(GLA_KWIDTH), tok(GLA_VWIDTH), tok(GLA_VWIDTH),
                  tok(GLA_KWIDTH), pl.BlockSpec((1, GLA_VWIDTH), lambda b, i: (0, 0))],
        out_specs=tok(GLA_VWIDTH),
        out_shape=jax.ShapeDtypeStruct((B, S, GLA_VWIDTH), bf16),
        scratch_shapes=[pltpu.VMEM((GLA_VAL_DIM, GLA_KWIDTH), f32)],
        compiler_params=pltpu.CompilerParams(
            dimension_semantics=("parallel", "arbitrary"), vmem_limit_bytes=VMEM_LIMIT),
        name="gla",
    )(gq, gk, gv, gr, la, gn)


def _ffn_kernel(x_ref, ym_ref, yg_ref, wo_m_ref, wo_g_ref, fg_ref, wup_ref, cw_ref, cb_ref,
                wdn_ref, og_ref, out_ref, u_ref):
    T = x_ref.shape[1]
    PAD = 8

    @pl.when(pl.program_id(1) == 0)
    def _():
        u_ref[0:PAD, :] = jnp.zeros((PAD, u_ref.shape[1]), f32)

    h = x_ref[0] + _dot(ym_ref[0], wo_m_ref[...]) + _dot(yg_ref[0], wo_g_ref[...])
    hn = _rms(h, fg_ref[...]).astype(bf16)
    u_ref[PAD:PAD + T, :] = _dot(hn, wup_ref[...])
    cw = cw_ref[...]
    conv = (cw[0:1] * u_ref[PAD - 2:PAD - 2 + T, :]
            + cw[1:2] * u_ref[PAD - 1:PAD - 1 + T, :]
            + cw[2:3] * u_ref[PAD:PAD + T, :]
            + cb_ref[...])
    u_ref[0:PAD, :] = u_ref[T:T + PAD, :]
    hg = conv[:, :D_FF]
    act = (hg * jax.nn.sigmoid(hg) * conv[:, D_FF:]).astype(bf16)
    y = h + _dot(act, wdn_ref[...])
    out_ref[0] = _rms(y, og_ref[...])


def _ffn_call(x, ym, yg, wo_m, wo_g, fg, wup, cw, cb, wdn, og):
    B, S, D = x.shape
    T = FFN_TILE
    tok = lambda w: pl.BlockSpec((1, T, w), lambda b, i: (b, i, 0))
    const = lambda a: pl.BlockSpec(a.shape, lambda b, i: (0,) * a.ndim, pipeline_mode=pl.Buffered(1))
    return pl.pallas_call(
        _ffn_kernel,
        grid=(B, S // T),
        in_specs=[tok(D), tok(MOBA_WIDTH), tok(GLA_VWIDTH),
                  const(wo_m), const(wo_g), const(fg), const(wup), const(cw), const(cb),
                  const(wdn), const(og)],
        out_specs=tok(D),
        out_shape=jax.ShapeDtypeStruct((B, S, D), x.dtype),
        scratch_shapes=[pltpu.VMEM((T + 8, 2 * D_FF), f32)],
        compiler_params=pltpu.CompilerParams(
            dimension_semantics=("parallel", "arbitrary"), vmem_limit_bytes=VMEM_LIMIT),
        name="out_ffn",
    )(x, ym, yg, wo_m, wo_g, fg, wup, cw, cb, wdn, og)


def _rope_tables(S):
    hd = MOBA_HEAD_DIM
    inv_freq = 1.0 / (ROPE_THETA ** (jnp.arange(0, hd, 2, dtype=f32) / hd))
    ang = jnp.arange(S).astype(f32)[:, None] * inv_freq[None, :]
    cos, sin = jnp.cos(ang), jnp.sin(ang)
    cos_r = jnp.tile(cos, (1, LANES // (hd // 2)))
    sin_r = jnp.tile(jnp.concatenate([-sin, sin], axis=1), (1, LANES // hd))
    return cos.T, sin.T, cos_r, sin_r


def kernel(x, attn_norm_g, w_in, w_gate_up, b_gate, gla_norm_g, w_out, ffn_norm_g, w_ffn_up,
           conv_w, conv_b, w_ffn_down, final_norm_g):
    B, S, D = x.shape
    l = 0
    o_mk = MOBA_WIDTH
    o_mv = 2 * MOBA_WIDTH
    o_gq = 3 * MOBA_WIDTH
    o_gg = o_gq + 2 * GLA_KWIDTH + 2 * GLA_VWIDTH
    w = w_in[l]
    wqt = w[:, :o_mk].T.astype(bf16)
    wk = w[:, o_mk:o_mv].astype(bf16)
    wvt = w[:, o_mv:o_gq].T.astype(bf16)
    wg = jnp.pad(w[:, o_gq:], ((0, 0), (0, LANES - GLA_GATE_RANK))).astype(bf16)
    wgu = jnp.pad(w_gate_up[l], ((0, LANES - GLA_GATE_RANK), (0, 0)))
    cos_t, sin_t, cos_r, sin_r = _rope_tables(S)

    qt, k, vt, kmean, gq, gk, gv, gr, la = _proj_call(
        x, attn_norm_g[l][None, :], wqt, wk, wvt, wg, wgu, b_gate[l][None, :],
        cos_t, sin_t, cos_r, sin_r)
    kmean = kmean.reshape(B, S // MOBA_BLOCK, MOBA_WIDTH)

    y_moba = _moba_call(qt, k, vt, kmean)
    y_gla = _gla_call(gq, gk, gv, gr, la, gla_norm_g[l].reshape(1, GLA_VWIDTH))

    wo = w_out[l].astype(bf16)
    return _ffn_call(
        x, y_moba, y_gla, wo[:MOBA_WIDTH], wo[MOBA_WIDTH:], ffn_norm_g[l][None, :],
        w_ffn_up[l].astype(bf16), conv_w[l], conv_b[l][None, :], w_ffn_down[l].astype(bf16),
        final_norm_g[None, :])
```

```python
import functools
import math

import jax
import jax.numpy as jnp
from jax import lax
from jax.experimental import pallas as pl
from jax.experimental.pallas import tpu as pltpu

D_MODEL = 1024
MOBA_HEAD_DIM = 64
MOBA_HEADS = 8
MOBA_WIDTH = 512
MOBA_BLOCK = 256
MOBA_TOPK = 3
MOBA_UNROLL = 4
MOBA_ROWS = 32
GLA_HEADS = 4
GLA_VAL_DIM = 128
GLA_KEY_DIM = 64
GLA_VWIDTH = 512
GLA_KWIDTH = 256
GLA_GATE_RANK = 16
GLA_GATE_TAU = 16.0
GLA_CHUNK = 64
GLA_SUB = 16
D_FF = 2816
CONV_WIDTH = 3
ROPE_THETA = 10000.0
EPS = 1e-6

LANES = 128
NEG_BIG = -1e30
EXP_CAP = 80.0

PROJ_TILE = 512
GLA_TILE = 512
FFN_TILE = 256
VMEM_LIMIT = 56 * 1024 * 1024

f32 = jnp.float32
bf16 = jnp.bfloat16


def _dot(a, b):
    return jnp.dot(a, b, preferred_element_type=f32)


def _dot_nt(a, b):
    return lax.dot_general(a, b, (((1,), (1,)), ((), ())), preferred_element_type=f32)


def _dot_tn(a, b):
    return lax.dot_general(a, b, (((0,), (0,)), ((), ())), preferred_element_type=f32)


def _rms(xf, g):
    return xf * lax.rsqrt(jnp.mean(xf * xf, axis=-1, keepdims=True) + EPS) * g


def _proj_kernel(x_ref, g_ref, wqt_ref, wk_ref, wvt_ref, wg_ref, wgu_ref, bg_ref,
                 cos_t_ref, sin_t_ref, cos_r_ref, sin_r_ref,
                 qt_ref, k_ref, vt_ref, kmean_ref, gq_ref, gk_ref, gv_ref, gr_ref, la_ref):
    T = x_ref.shape[1]
    xn = _rms(x_ref[0], g_ref[...]).astype(bf16)

    qt = _dot_nt(wqt_ref[...], xn)
    cos_t = cos_t_ref[...]
    sin_t = sin_t_ref[...]
    half = MOBA_HEAD_DIM // 2
    scale = MOBA_HEAD_DIM ** -0.5
    for h in range(MOBA_HEADS):
        r0 = h * MOBA_HEAD_DIM
        t1 = qt[r0:r0 + half]
        t2 = qt[r0 + half:r0 + MOBA_HEAD_DIM]
        qt_ref[0, r0:r0 + half, :] = ((t1 * cos_t - t2 * sin_t) * scale).astype(bf16)
        qt_ref[0, r0 + half:r0 + MOBA_HEAD_DIM, :] = ((t2 * cos_t + t1 * sin_t) * scale).astype(bf16)

    k = _dot(xn, wk_ref[...])
    cos_r = cos_r_ref[...]
    sin_r = sin_r_ref[...]
    lane = lax.broadcasted_iota(jnp.int32, (T, LANES), 1)
    first_half = (lane % MOBA_HEAD_DIM) < half
    nblk = T // MOBA_BLOCK
    for p in range(MOBA_WIDTH // LANES):
        kp = k[:, p * LANES:(p + 1) * LANES]
        rot = jnp.where(first_half, pltpu.roll(kp, LANES - half, 1), pltpu.roll(kp, half, 1))
        kr = kp * cos_r + rot * sin_r
        k_ref[0, :, p * LANES:(p + 1) * LANES] = kr.astype(bf16)
        for j in range(nblk):
            kmean_ref[0, 0, j:j + 1, p * LANES:(p + 1) * LANES] = jnp.mean(
                kr[j * MOBA_BLOCK:(j + 1) * MOBA_BLOCK], axis=0, keepdims=True)

    vt = _dot_nt(wvt_ref[...], xn).astype(bf16)
    for j in range(nblk):
        vt_ref[0, j] = vt[:, j * MOBA_BLOCK:(j + 1) * MOBA_BLOCK]

    pg = _dot(xn, wg_ref[...])
    o_gk = GLA_KWIDTH
    o_gv = o_gk + GLA_KWIDTH
    o_gr = o_gv + GLA_VWIDTH
    o_gg = o_gr + GLA_VWIDTH
    gq_ref[0] = pg[:, :o_gk] * (GLA_KEY_DIM ** -0.5)
    gk_ref[0] = pg[:, o_gk:o_gv]
    gv_ref[0] = pg[:, o_gv:o_gr]
    gr_ref[0] = pg[:, o_gr:o_gg]
    gate_lr = pg[:, o_gg:]
    z = jnp.dot(gate_lr, wgu_ref[...], preferred_element_type=f32,
                precision=lax.Precision.HIGHEST) + bg_ref[...]
    log_sig = jnp.minimum(z, 0.0) - jnp.log1p(jnp.exp(-jnp.abs(z)))
    la_ref[0] = log_sig * (1.0 / GLA_GATE_TAU)


def _proj_call(x, g, wqt, wk, wvt, wg, wgu, bg, cos_t, sin_t, cos_r, sin_r):
    B, S, D = x.shape
    T = PROJ_TILE
    nb_t = T // MOBA_BLOCK
    const = lambda shape: pl.BlockSpec(shape, lambda b, i: (0,) * len(shape))
    tok = lambda w: pl.BlockSpec((1, T, w), lambda b, i: (b, i, 0))
    out_shape = (
        jax.ShapeDtypeStruct((B, MOBA_WIDTH, S), bf16),
        jax.ShapeDtypeStruct((B, S, MOBA_WIDTH), bf16),
        jax.ShapeDtypeStruct((B, S // MOBA_BLOCK, MOBA_WIDTH, MOBA_BLOCK), bf16),
        jax.ShapeDtypeStruct((B, S // T, nb_t, MOBA_WIDTH), f32),
        jax.ShapeDtypeStruct((B, S, GLA_KWIDTH), f32),
        jax.ShapeDtypeStruct((B, S, GLA_KWIDTH), f32),
        jax.ShapeDtypeStruct((B, S, GLA_VWIDTH), f32),
        jax.ShapeDtypeStruct((B, S, GLA_VWIDTH), f32),
        jax.ShapeDtypeStruct((B, S, GLA_KWIDTH), f32),
    )
    out_specs = (
        pl.BlockSpec((1, MOBA_WIDTH, T), lambda b, i: (b, 0, i)),
        tok(MOBA_WIDTH),
        pl.BlockSpec((1, nb_t, MOBA_WIDTH, MOBA_BLOCK), lambda b, i: (b, i, 0, 0)),
        pl.BlockSpec((1, 1, nb_t, MOBA_WIDTH), lambda b, i: (b, i, 0, 0)),
        tok(GLA_KWIDTH), tok(GLA_KWIDTH), tok(GLA_VWIDTH), tok(GLA_VWIDTH), tok(GLA_KWIDTH),
    )
    in_specs = [
        tok(D),
        const((1, D)),
        const(wqt.shape), const(wk.shape), const(wvt.shape), const(wg.shape),
        const(wgu.shape), const((1, GLA_KWIDTH)),
        pl.BlockSpec((MOBA_HEAD_DIM // 2, T), lambda b, i: (0, i)),
        pl.BlockSpec((MOBA_HEAD_DIM // 2, T), lambda b, i: (0, i)),
        pl.BlockSpec((T, LANES), lambda b, i: (i, 0)),
        pl.BlockSpec((T, LANES), lambda b, i: (i, 0)),
    ]
    return pl.pallas_call(
        _proj_kernel,
        grid=(B, S // T),
        in_specs=in_specs,
        out_specs=out_specs,
        out_shape=out_shape,
        compiler_params=pltpu.CompilerParams(
            dimension_semantics=("parallel", "parallel"), vmem_limit_bytes=VMEM_LIMIT),
        name="in_proj",
    )(x, g, wqt, wk, wvt, wg, wgu, bg, cos_t, sin_t, cos_r, sin_r)


def _moba_kernel(qt_ref, k_ref, vt_ref, kmean_ref, onehot_ref, o_ref, s_ref, p_ref):
    qi = pl.program_id(2)
    QB = qt_ref.shape[2]
    NB = kmean_ref.shape[1]
    HD = MOBA_HEAD_DIM

    U = MOBA_UNROLL
    BLK = MOBA_BLOCK
    ROWS = MOBA_ROWS

    km = kmean_ref[0]
    km_hi = km.astype(bf16)
    km_lo = (km - km_hi.astype(f32)).astype(bf16)
    blk = lax.broadcasted_iota(jnp.int32, (NB, QB), 0)
    zeros_half = jnp.zeros((HD, QB), bf16)
    zeros_tail = jnp.zeros((LANES - NB, QB), bf16)

    qzs = []
    for h in range(2):
        q_h = qt_ref[0, h * HD:(h + 1) * HD, :]
        qz = jnp.concatenate([q_h, zeros_half] if h == 0 else [zeros_half, q_h], axis=0)

        gate = _dot(km_hi, qz) + _dot(km_lo, qz)
        gate = jnp.where(blk < qi, gate, -jnp.inf)
        sel = jnp.zeros((NB, QB), jnp.bool_)
        for _ in range(MOBA_TOPK):
            top = jnp.max(gate, axis=0, keepdims=True)
            idx = jnp.min(jnp.where(gate == top, blk, NB), axis=0, keepdims=True)
            pick = blk == idx
            sel = jnp.logical_or(sel, jnp.logical_and(pick, top > -jnp.inf))
            gate = jnp.where(pick, -jnp.inf, gate)
        bias = jnp.where(jnp.logical_or(sel, blk == qi), 0.0, NEG_BIG).astype(bf16)
        qzs.append(jnp.concatenate([qz, bias, zeros_tail], axis=0))

    n_groups = qi // U + 1

    def scores(t, slot):
        r0 = pl.multiple_of(t * (U * BLK), U * BLK)
        k_aug = jnp.concatenate(
            [k_ref[0, pl.ds(r0, U * BLK), :], onehot_ref[pl.ds(r0, U * BLK), :]], axis=1)
        tops = []
        for h in range(2):
            s = _dot(k_aug, qzs[h])
            s_ref[slot, h] = s
            tops.append(jnp.max(s, axis=0, keepdims=True))
        return tuple(tops)

    def absorb(t, load_rows, top, state, h):
        m, l, acc = state
        m_new = jnp.maximum(m, top)
        alpha = jnp.exp(m - m_new)
        part = jnp.zeros((ROWS, QB), f32)
        for r in range(0, U * BLK, ROWS):
            e = jnp.exp(load_rows(r) - m_new)
            part = part + e
            p_ref[h, r:r + ROWS, :] = e.astype(bf16)
        l = alpha * l + jnp.sum(part, axis=0, keepdims=True)
        pv = _dot(vt_ref[0, t * U, h * HD:(h + 1) * HD, :], p_ref[h, :BLK, :])
        for u in range(1, U):
            pv = pv + _dot(vt_ref[0, t * U + u, h * HD:(h + 1) * HD, :],
                           p_ref[h, u * BLK:(u + 1) * BLK, :])
        return m_new, l, alpha * acc + pv

    def step(t, carry, cur):
        tops, states = carry
        tops_next = scores(t + 1, 1 - cur)
        return tops_next, tuple(
            absorb(t, lambda r, h=h: s_ref[cur, h, r:r + ROWS, :], tops[h], states[h], h)
            for h in range(2))

    def body(t, carry):
        return lax.cond(t % 2 == 0,
                        functools.partial(step, t, cur=0), functools.partial(step, t, cur=1), carry)

    state0 = (jnp.full((1, QB), -jnp.inf, f32), jnp.zeros((1, QB), f32), jnp.zeros((HD, QB), f32))
    _, states = lax.fori_loop(0, n_groups - 1, body, (scores(0, 0), (state0, state0)))

    t_last = n_groups - 1
    slot_last = t_last % 2
    key_minus_query = (t_last * (U * BLK) - qi * BLK
                       + lax.broadcasted_iota(jnp.int32, (ROWS, QB), 0)
                       - lax.broadcasted_iota(jnp.int32, (ROWS, QB), 1))
    outs = []
    for h in range(2):
        def load_rows(r, h=h):
            return jnp.where(key_minus_query <= -r, s_ref[slot_last, h, r:r + ROWS, :], NEG_BIG)
        top = load_rows(0)
        for r in range(ROWS, U * BLK, ROWS):
            top = jnp.maximum(top, load_rows(r))
        top = jnp.max(top, axis=0, keepdims=True)
        _, l, acc = absorb(t_last, load_rows, top, states[h], h)
        outs.append(acc / l)
    o_t = jnp.concatenate(outs, axis=0)
    o_ref[0] = o_t.T.astype(o_ref.dtype)


def _moba_call(qt, k, vt, kmean):
    B, W, S = qt.shape
    NB = S // MOBA_BLOCK
    QB = MOBA_BLOCK
    n_hp = W // LANES
    assert NB % MOBA_UNROLL == 0 and NB <= LANES
    onehot = (jnp.arange(S)[:, None] // MOBA_BLOCK == jnp.arange(LANES)[None, :]).astype(bf16)
    return pl.pallas_call(
        _moba_kernel,
        grid=(B, n_hp, S // QB),
        in_specs=[
            pl.BlockSpec((1, LANES, QB), lambda b, hp, i: (b, hp, i)),
            pl.BlockSpec((1, S, LANES), lambda b, hp, i: (b, 0, hp)),
            pl.BlockSpec((1, NB, LANES, MOBA_BLOCK), lambda b, hp, i: (b, 0, hp, 0)),
            pl.BlockSpec((1, NB, LANES), lambda b, hp, i: (b, 0, hp)),
            pl.BlockSpec((S, LANES), lambda b, hp, i: (0, 0)),
        ],
        out_specs=pl.BlockSpec((1, QB, LANES), lambda b, hp, i: (b, i, hp)),
        out_shape=jax.ShapeDtypeStruct((B, S, W), bf16),
        scratch_shapes=[pltpu.VMEM((2, 2, MOBA_UNROLL * MOBA_BLOCK, QB), f32),
                        pltpu.VMEM((2, MOBA_UNROLL * MOBA_BLOCK, QB), bf16)],
        compiler_params=pltpu.CompilerParams(
            dimension_semantics=("parallel", "parallel", "arbitrary"),
            vmem_limit_bytes=VMEM_LIMIT),
        name="moba",
    )(qt, k, vt, kmean, onehot)


def _split3(a):
    hi = a.astype(bf16)
    r1 = a - hi.astype(f32)
    mid = r1.astype(bf16)
    lo = (r1 - mid.astype(f32)).astype(bf16)
    return hi, mid, lo


def _gla_kernel(gq_ref, gk_ref, gv_ref, gr_ref, la_ref, gn_ref, y_ref, st_ref):
    C = GLA_CHUNK
    H = GLA_HEADS
    KW = GLA_KWIDTH
    DV = GLA_VAL_DIM
    NSUB = C // GLA_SUB
    T = gq_ref.shape[1]

    @pl.when(pl.program_id(1) == 0)
    def _():
        st_ref[...] = jnp.zeros_like(st_ref)

    row = lax.broadcasted_iota(jnp.int32, (C, C), 0)
    col = lax.broadcasted_iota(jnp.int32, (C, C), 1)
    tril = (col <= row).astype(bf16)
    lane_head = lax.broadcasted_iota(jnp.int32, (C, KW), 1) // GLA_KEY_DIM
    rt = lax.broadcasted_iota(jnp.int32, (C, NSUB * C), 0)
    rc = lax.broadcasted_iota(jnp.int32, (C, NSUB * C), 1)
    keep = jnp.logical_and(rc // C == rt // GLA_SUB, rc % C <= rt)
    st_lane_head = lax.broadcasted_iota(jnp.int32, (DV, KW), 1) // GLA_KEY_DIM

    for c in range(T // C):
        rows = slice(c * C, (c + 1) * C)
        q = gq_ref[0, rows, :]
        k = gk_ref[0, rows, :]
        v = gv_ref[0, rows, :].astype(bf16)
        la = la_ref[0, rows, :]

        hi, mid, lo = _split3(la)
        G = _dot(tril, hi) + _dot(tril, mid) + _dot(tril, lo)
        g_last = G[C - 1:C, :]

        g_ref_rows = [G[i * GLA_SUB:i * GLA_SUB + 1, :] for i in range(NSUB)]
        g_own = jnp.concatenate(
            [jnp.broadcast_to(g, (GLA_SUB, KW)) for g in g_ref_rows], axis=0)
        q_in = q * jnp.exp(G - g_own)
        k_in = jnp.concatenate(
            [(k * jnp.exp(jnp.minimum(g - G, EXP_CAP))).astype(bf16) for g in g_ref_rows],
            axis=0)
        q_st = q * jnp.exp(G)
        k_st = (k * jnp.exp(g_last - G)).astype(bf16)

        qz_in = jnp.concatenate(
            [jnp.where(lane_head == h, q_in, 0.0).astype(bf16) for h in range(H)], axis=0)
        qz_st = jnp.concatenate(
            [jnp.where(lane_head == h, q_st, 0.0).astype(bf16) for h in range(H)], axis=0)

        r = _dot_nt(qz_in, k_in)
        keep_all = jnp.concatenate([keep] * H, axis=0)
        r = jnp.where(keep_all, r, 0.0).astype(bf16)
        v_rep = jnp.concatenate([v] * NSUB, axis=0)
        o_intra = _dot(r, v_rep)

        st = st_ref[...]
        o_inter = _dot_nt(qz_st, st.astype(bf16))

        upd = _dot_tn(v, k_st)
        new_st = st * jnp.exp(g_last)
        for h in range(H):
            new_st = new_st + jnp.where(st_lane_head == h, upd[h * DV:(h + 1) * DV, :], 0.0)
        st_ref[...] = new_st

        for h in range(H):
            o = o_intra[h * C:(h + 1) * C, h * DV:(h + 1) * DV] + o_inter[h * C:(h + 1) * C, :]
            o = o * lax.rsqrt(jnp.mean(o * o, axis=-1, keepdims=True) + EPS)
            o = o * gn_ref[:, h * DV:(h + 1) * DV]
            gr = gr_ref[0, rows, h * DV:(h + 1) * DV]
            y = o * (gr * jax.nn.sigmoid(gr))
            y_ref[0, rows, h * DV:(h + 1) * DV] = y.astype(y_ref.dtype)


def _gla_call(gq, gk, gv, gr, la, gn):
    B, S, _ = gq.shape
    T = GLA_TILE
    tok = lambda w: pl.BlockSpec((1, T, w), lambda b, i: (b, i, 0))
    return pl.pallas_call(
        _gla_kernel,
        grid=(B, S // T),
        in_specs=[tok(GLA_KWIDTH), tok(GLA_KWIDTH), tok(GLA_VWIDTH), tok(GLA_VWIDTH),
                  tok(GLA_KWIDTH), pl.BlockSpec((1, GLA_VWIDTH), lambda b, i: (0, 0))],
        out_specs=tok(GLA_VWIDTH),
        out_shape=jax.ShapeDtypeStruct((B, S, GLA_VWIDTH), bf16),
        scratch_shapes=[pltpu.VMEM((GLA_VAL_DIM, GLA_KWIDTH), f32)],
        compiler_params=pltpu.CompilerParams(
            dimension_semantics=("parallel", "arbitrary"), vmem_limit_bytes=VMEM_LIMIT),
        name="gla",
    )(gq, gk, gv, gr, la, gn)


def _ffn_kernel(x_ref, ym_ref, yg_ref, wo_m_ref, wo_g_ref, fg_ref, wup_ref, cw_ref, cb_ref,
                wdn_ref, og_ref, out_ref, u_ref):
    T = x_ref.shape[1]
    PAD = 8

    @pl.when(pl.program_id(1) == 0)
    def _():
        u_ref[0:PAD, :] = jnp.zeros((PAD, u_ref.shape[1]), f32)

    h = x_ref[0] + _dot(ym_ref[0], wo_m_ref[...]) + _dot(yg_ref[0], wo_g_ref[...])
    hn = _rms(h, fg_ref[...]).astype(bf16)
    u_ref[PAD:PAD + T, :] = _dot(hn, wup_ref[...])
    cw = cw_ref[...]
    conv = (cw[0:1] * u_ref[PAD - 2:PAD - 2 + T, :]
            + cw[1:2] * u_ref[PAD - 1:PAD - 1 + T, :]
            + cw[2:3] * u_ref[PAD:PAD + T, :]
            + cb_ref[...])
    u_ref[0:PAD, :] = u_ref[T:T + PAD, :]
    hg = conv[:, :D_FF]
    act = (hg * jax.nn.sigmoid(hg) * conv[:, D_FF:]).astype(bf16)
    y = h + _dot(act, wdn_ref[...])
    out_ref[0] = _rms(y, og_ref[...])


def _ffn_call(x, ym, yg, wo_m, wo_g, fg, wup, cw, cb, wdn, og):
    B, S, D = x.shape
    T = FFN_TILE
    tok = lambda w: pl.BlockSpec((1, T, w), lambda b, i: (b, i, 0))
    const = lambda a: pl.BlockSpec(a.shape, lambda b, i: (0,) * a.ndim, pipeline_mode=pl.Buffered(1))
    return pl.pallas_call(
        _ffn_kernel,
        grid=(B, S // T),
        in_specs=[tok(D), tok(MOBA_WIDTH), tok(GLA_VWIDTH),
                  const(wo_m), const(wo_g), const(fg), const(wup), const(cw), const(cb),
                  const(wdn), const(og)],
        out_specs=tok(D),
        out_shape=jax.ShapeDtypeStruct((B, S, D), x.dtype),
        scratch_shapes=[pltpu.VMEM((T + 8, 2 * D_FF), f32)],
        compiler_params=pltpu.CompilerParams(
            dimension_semantics=("parallel", "arbitrary"), vmem_limit_bytes=VMEM_LIMIT),
        name="out_ffn",
    )(x, ym, yg, wo_m, wo_g, fg, wup, cw, cb, wdn, og)


def _rope_tables(S):
    hd = MOBA_HEAD_DIM
    inv_freq = 1.0 / (ROPE_THETA ** (jnp.arange(0, hd, 2, dtype=f32) / hd))
    ang = jnp.arange(S).astype(f32)[:, None] * inv_freq[None, :]
    cos, sin = jnp.cos(ang), jnp.sin(ang)
    cos_r = jnp.tile(cos, (1, LANES // (hd // 2)))
    sin_r = jnp.tile(jnp.concatenate([-sin, sin], axis=1), (1, LANES // hd))
    return cos.T, sin.T, cos_r, sin_r


def kernel(x, attn_norm_g, w_in, w_gate_up, b_gate, gla_norm_g, w_out, ffn_norm_g, w_ffn_up,
           conv_w, conv_b, w_ffn_down, final_norm_g):
    B, S, D = x.shape
    l = 0
    o_mk = MOBA_WIDTH
    o_mv = 2 * MOBA_WIDTH
    o_gq = 3 * MOBA_WIDTH
    o_gg = o_gq + 2 * GLA_KWIDTH + 2 * GLA_VWIDTH
    w = w_in[l]
    wqt = w[:, :o_mk].T.astype(bf16)
    wk = w[:, o_mk:o_mv].astype(bf16)
    wvt = w[:, o_mv:o_gq].T.astype(bf16)
    wg = jnp.pad(w[:, o_gq:], ((0, 0), (0, LANES - GLA_GATE_RANK))).astype(bf16)
    wgu = jnp.pad(w_gate_up[l], ((0, LANES - GLA_GATE_RANK), (0, 0)))
    cos_t, sin_t, cos_r, sin_r = _rope_tables(S)

    qt, k, vt, kmean, gq, gk, gv, gr, la = _proj_call(
        x, attn_norm_g[l][None, :], wqt, wk, wvt, wg, wgu, b_gate[l][None, :],
        cos_t, sin_t, cos_r, sin_r)
    kmean = kmean.reshape(B, S // MOBA_BLOCK, MOBA_WIDTH)

    y_moba = _moba_call(qt, k, vt, kmean)
    y_gla = _gla_call(gq, gk, gv, gr, la, gla_norm_g[l].reshape(1, GLA_VWIDTH))

    wo = w_out[l].astype(bf16)
    return _ffn_call(
        x, y_moba, y_gla, wo[:MOBA_WIDTH], wo[MOBA_WIDTH:], ffn_norm_g[l][None, :],
        w_ffn_up[l].astype(bf16), conv_w[l], conv_b[l][None, :], w_ffn_down[l].astype(bf16),
        final_norm_g[None, :])
```

```python
import functools
import math

import jax
import jax.numpy as jnp
from jax import lax
from jax.experimental import pallas as pl
from jax.experimental.pallas import tpu as pltpu

D_MODEL = 1024
MOBA_HEAD_DIM = 64
MOBA_HEADS = 8
MOBA_WIDTH = 512
MOBA_BLOCK = 256
MOBA_TOPK = 3
MOBA_UNROLL = 4
MOBA_ROWS = 32
GLA_HEADS = 4
GLA_VAL_DIM = 128
GLA_KEY_DIM = 64
GLA_VWIDTH = 512
GLA_KWIDTH = 256
GLA_GATE_RANK = 16
GLA_GATE_TAU = 16.0
GLA_CHUNK = 64
GLA_SUB = 16
D_FF = 2816
CONV_WIDTH = 3
ROPE_THETA = 10000.0
EPS = 1e-6

LANES = 128
LOG2_E = 1.4426950408889634
NEG_BIG = -1e30
EXP_CAP = 80.0

PROJ_TILE = 512
GLA_TILE = 512
FFN_TILE = 256
VMEM_LIMIT = 56 * 1024 * 1024

f32 = jnp.float32
bf16 = jnp.bfloat16


def _dot(a, b):
    return jnp.dot(a, b, preferred_element_type=f32)


def _dot_nt(a, b):
    return lax.dot_general(a, b, (((1,), (1,)), ((), ())), preferred_element_type=f32)


def _dot_tn(a, b):
    return lax.dot_general(a, b, (((0,), (0,)), ((), ())), preferred_element_type=f32)


def _rms(xf, g):
    return xf * lax.rsqrt(jnp.mean(xf * xf, axis=-1, keepdims=True) + EPS) * g


def _proj_kernel(x_ref, g_ref, wqt_ref, wk_ref, wvt_ref, wg_ref, wgu_ref, bg_ref,
                 cos_t_ref, sin_t_ref, cos_r_ref, sin_r_ref,
                 qt_ref, k_ref, vt_ref, kmean_ref, gq_ref, gk_ref, gv_ref, gr_ref, la_ref):
    T = x_ref.shape[1]
    xn = _rms(x_ref[0], g_ref[...]).astype(bf16)

    qt = _dot_nt(wqt_ref[...], xn)
    cos_t = cos_t_ref[...]
    sin_t = sin_t_ref[...]
    half = MOBA_HEAD_DIM // 2
    scale = MOBA_HEAD_DIM ** -0.5 * LOG2_E
    for h in range(MOBA_HEADS):
        r0 = h * MOBA_HEAD_DIM
        t1 = qt[r0:r0 + half]
        t2 = qt[r0 + half:r0 + MOBA_HEAD_DIM]
        qt_ref[0, r0:r0 + half, :] = ((t1 * cos_t - t2 * sin_t) * scale).astype(bf16)
        qt_ref[0, r0 + half:r0 + MOBA_HEAD_DIM, :] = ((t2 * cos_t + t1 * sin_t) * scale).astype(bf16)

    k = _dot(xn, wk_ref[...])
    cos_r = cos_r_ref[...]
    sin_r = sin_r_ref[...]
    lane = lax.broadcasted_iota(jnp.int32, (T, LANES), 1)
    first_half = (lane % MOBA_HEAD_DIM) < half
    nblk = T // MOBA_BLOCK
    for p in range(MOBA_WIDTH // LANES):
        kp = k[:, p * LANES:(p + 1) * LANES]
        rot = jnp.where(first_half, pltpu.roll(kp, LANES - half, 1), pltpu.roll(kp, half, 1))
        kr = kp * cos_r + rot * sin_r
        k_ref[0, :, p * LANES:(p + 1) * LANES] = kr.astype(bf16)
        for j in range(nblk):
            kmean_ref[0, 0, j:j + 1, p * LANES:(p + 1) * LANES] = jnp.mean(
                kr[j * MOBA_BLOCK:(j + 1) * MOBA_BLOCK], axis=0, keepdims=True)

    vt = _dot_nt(wvt_ref[...], xn).astype(bf16)
    for j in range(nblk):
        vt_ref[0, j] = vt[:, j * MOBA_BLOCK:(j + 1) * MOBA_BLOCK]

    pg = _dot(xn, wg_ref[...])
    o_gk = GLA_KWIDTH
    o_gv = o_gk + GLA_KWIDTH
    o_gr = o_gv + GLA_VWIDTH
    o_gg = o_gr + GLA_VWIDTH
    gq_ref[0] = pg[:, :o_gk] * (GLA_KEY_DIM ** -0.5)
    gk_ref[0] = pg[:, o_gk:o_gv]
    gv_ref[0] = pg[:, o_gv:o_gr]
    gr_ref[0] = pg[:, o_gr:o_gg]
    gate_lr = pg[:, o_gg:]
    z = jnp.dot(gate_lr, wgu_ref[...], preferred_element_type=f32,
                precision=lax.Precision.HIGHEST) + bg_ref[...]
    log_sig = jnp.minimum(z, 0.0) - jnp.log1p(jnp.exp(-jnp.abs(z)))
    la_ref[0] = log_sig * (1.0 / GLA_GATE_TAU)


def _proj_call(x, g, wqt, wk, wvt, wg, wgu, bg, cos_t, sin_t, cos_r, sin_r):
    B, S, D = x.shape
    T = PROJ_TILE
    nb_t = T // MOBA_BLOCK
    const = lambda shape: pl.BlockSpec(shape, lambda b, i: (0,) * len(shape))
    tok = lambda w: pl.BlockSpec((1, T, w), lambda b, i: (b, i, 0))
    out_shape = (
        jax.ShapeDtypeStruct((B, MOBA_WIDTH, S), bf16),
        jax.ShapeDtypeStruct((B, S, MOBA_WIDTH), bf16),
        jax.ShapeDtypeStruct((B, S // MOBA_BLOCK, MOBA_WIDTH, MOBA_BLOCK), bf16),
        jax.ShapeDtypeStruct((B, S // T, nb_t, MOBA_WIDTH), f32),
        jax.ShapeDtypeStruct((B, S, GLA_KWIDTH), f32),
        jax.ShapeDtypeStruct((B, S, GLA_KWIDTH), f32),
        jax.ShapeDtypeStruct((B, S, GLA_VWIDTH), f32),
        jax.ShapeDtypeStruct((B, S, GLA_VWIDTH), f32),
        jax.ShapeDtypeStruct((B, S, GLA_KWIDTH), f32),
    )
    out_specs = (
        pl.BlockSpec((1, MOBA_WIDTH, T), lambda b, i: (b, 0, i)),
        tok(MOBA_WIDTH),
        pl.BlockSpec((1, nb_t, MOBA_WIDTH, MOBA_BLOCK), lambda b, i: (b, i, 0, 0)),
        pl.BlockSpec((1, 1, nb_t, MOBA_WIDTH), lambda b, i: (b, i, 0, 0)),
        tok(GLA_KWIDTH), tok(GLA_KWIDTH), tok(GLA_VWIDTH), tok(GLA_VWIDTH), tok(GLA_KWIDTH),
    )
    in_specs = [
        tok(D),
        const((1, D)),
        const(wqt.shape), const(wk.shape), const(wvt.shape), const(wg.shape),
        const(wgu.shape), const((1, GLA_KWIDTH)),
        pl.BlockSpec((MOBA_HEAD_DIM // 2, T), lambda b, i: (0, i)),
        pl.BlockSpec((MOBA_HEAD_DIM // 2, T), lambda b, i: (0, i)),
        pl.BlockSpec((T, LANES), lambda b, i: (i, 0)),
        pl.BlockSpec((T, LANES), lambda b, i: (i, 0)),
    ]
    return pl.pallas_call(
        _proj_kernel,
        grid=(B, S // T),
        in_specs=in_specs,
        out_specs=out_specs,
        out_shape=out_shape,
        compiler_params=pltpu.CompilerParams(
            dimension_semantics=("parallel", "parallel"), vmem_limit_bytes=VMEM_LIMIT),
        name="in_proj",
    )(x, g, wqt, wk, wvt, wg, wgu, bg, cos_t, sin_t, cos_r, sin_r)


def _moba_kernel(qt_ref, k_ref, vt_ref, kmean_ref, onehot_ref, o_ref, s_ref, p_ref):
    qi = pl.program_id(2)
    QB = qt_ref.shape[2]
    NB = kmean_ref.shape[1]
    HD = MOBA_HEAD_DIM

    U = MOBA_UNROLL
    BLK = MOBA_BLOCK
    ROWS = MOBA_ROWS

    km = kmean_ref[0]
    km_hi = km.astype(bf16)
    km_lo = (km - km_hi.astype(f32)).astype(bf16)
    blk = lax.broadcasted_iota(jnp.int32, (NB, QB), 0)
    zeros_half = jnp.zeros((HD, QB), bf16)
    zeros_tail = jnp.zeros((LANES - NB, QB), bf16)

    qzs = []
    for h in range(2):
        q_h = qt_ref[0, h * HD:(h + 1) * HD, :]
        qz = jnp.concatenate([q_h, zeros_half] if h == 0 else [zeros_half, q_h], axis=0)

        gate = _dot(km_hi, qz) + _dot(km_lo, qz)
        gate = jnp.where(blk < qi, gate, -jnp.inf)
        sel = jnp.zeros((NB, QB), jnp.bool_)
        for _ in range(MOBA_TOPK):
            top = jnp.max(gate, axis=0, keepdims=True)
            idx = jnp.min(jnp.where(gate == top, blk, NB), axis=0, keepdims=True)
            pick = blk == idx
            sel = jnp.logical_or(sel, jnp.logical_and(pick, top > -jnp.inf))
            gate = jnp.where(pick, -jnp.inf, gate)
        bias = jnp.where(jnp.logical_or(sel, blk == qi), 0.0, NEG_BIG).astype(bf16)
        qzs.append(jnp.concatenate([qz, bias, zeros_tail], axis=0))

    n_groups = qi // U + 1

    def block_scores(t, u, slot, tops):
        r0 = pl.multiple_of((t * U + u) * BLK, BLK)
        k_aug = jnp.concatenate(
            [k_ref[0, pl.ds(r0, BLK), :], onehot_ref[pl.ds(r0, BLK), :]], axis=1)
        out = []
        for h in range(2):
            s = _dot(k_aug, qzs[h])
            s_ref[slot, h, u * BLK:(u + 1) * BLK, :] = s
            top = jnp.max(s, axis=0, keepdims=True)
            out.append(top if tops is None else jnp.maximum(tops[h], top))
        return tuple(out)

    def block_softmax(load_rows, u, m_new, part, h):
        for r in range(u * BLK, (u + 1) * BLK, ROWS):
            e = jnp.exp2(load_rows(r) - m_new)
            part = part + e
            p_ref[h, r:r + ROWS, :] = e.astype(bf16)
        return part

    def block_values(t, u, pv, h):
        d = _dot(vt_ref[0, t * U + u, h * HD:(h + 1) * HD, :], p_ref[h, u * BLK:(u + 1) * BLK, :])
        return d if pv is None else pv + d

    def group(t, tops, mls, accs, load_rows, next_slot):
        m_new = [jnp.maximum(mls[h][0], tops[h]) for h in range(2)]
        alpha = [jnp.exp2(mls[h][0] - m_new[h]) for h in range(2)]
        part = [jnp.zeros((ROWS, QB), f32)] * 2
        pv = [None, None]
        tops_next = None
        for u in range(U):
            if next_slot is not None:
                tops_next = block_scores(t + 1, u, next_slot, tops_next)
            for h in range(2):
                part[h] = block_softmax(load_rows[h], u, m_new[h], part[h], h)
            for h in range(2):
                pv[h] = block_values(t, u, pv[h], h)
        mls = tuple((m_new[h], alpha[h] * mls[h][1] + jnp.sum(part[h], axis=0, keepdims=True))
                    for h in range(2))
        accs = tuple(alpha[h] * accs[h] + pv[h] for h in range(2))
        return tops_next, mls, accs

    def step(t, carry, cur):
        tops, mls, accs = carry
        load_rows = [lambda r, h=h: s_ref[cur, h, r:r + ROWS, :] for h in range(2)]
        return group(t, tops, mls, accs, load_rows, 1 - cur)

    def body(t, carry):
        return lax.cond(t % 2 == 0,
                        functools.partial(step, t, cur=0), functools.partial(step, t, cur=1), carry)

    tops0 = None
    for u in range(U):
        tops0 = block_scores(0, u, 0, tops0)
    ml0 = (jnp.full((1, QB), -jnp.inf, f32), jnp.zeros((1, QB), f32))
    acc0 = jnp.zeros((HD, QB), f32)
    _, mls, accs = lax.fori_loop(0, n_groups - 1, body, (tops0, (ml0, ml0), (acc0, acc0)))

    t_last = n_groups - 1
    slot_last = t_last % 2
    key_minus_query = (t_last * (U * BLK) - qi * BLK
                       + lax.broadcasted_iota(jnp.int32, (ROWS, QB), 0)
                       - lax.broadcasted_iota(jnp.int32, (ROWS, QB), 1))
    load_rows, tops = [], []
    for h in range(2):
        def load(r, h=h):
            return jnp.where(key_minus_query <= -r, s_ref[slot_last, h, r:r + ROWS, :], NEG_BIG)
        top = load(0)
        for r in range(ROWS, U * BLK, ROWS):
            top = jnp.maximum(top, load(r))
        load_rows.append(load)
        tops.append(jnp.max(top, axis=0, keepdims=True))
    _, mls, accs = group(t_last, tops, mls, accs, load_rows, None)
    o_t = jnp.concatenate([accs[h] / mls[h][1] for h in range(2)], axis=0)
    o_ref[0] = o_t.T.astype(o_ref.dtype)


def _moba_call(qt, k, vt, kmean):
    B, W, S = qt.shape
    NB = S // MOBA_BLOCK
    QB = MOBA_BLOCK
    n_hp = W // LANES
    assert NB % MOBA_UNROLL == 0 and NB <= LANES
    onehot = (jnp.arange(S)[:, None] // MOBA_BLOCK == jnp.arange(LANES)[None, :]).astype(bf16)
    return pl.pallas_call(
        _moba_kernel,
        grid=(B, n_hp, S // QB),
        in_specs=[
            pl.BlockSpec((1, LANES, QB), lambda b, hp, i: (b, hp, i)),
            pl.BlockSpec((1, S, LANES), lambda b, hp, i: (b, 0, hp)),
            pl.BlockSpec((1, NB, LANES, MOBA_BLOCK), lambda b, hp, i: (b, 0, hp, 0)),
            pl.BlockSpec((1, NB, LANES), lambda b, hp, i: (b, 0, hp)),
            pl.BlockSpec((S, LANES), lambda b, hp, i: (0, 0)),
        ],
        out_specs=pl.BlockSpec((1, QB, LANES), lambda b, hp, i: (b, i, hp)),
        out_shape=jax.ShapeDtypeStruct((B, S, W), bf16),
        scratch_shapes=[pltpu.VMEM((2, 2, MOBA_UNROLL * MOBA_BLOCK, QB), f32),
                        pltpu.VMEM((2, MOBA_UNROLL * MOBA_BLOCK, QB), bf16)],
        compiler_params=pltpu.CompilerParams(
            dimension_semantics=("parallel", "parallel", "arbitrary"),
            vmem_limit_bytes=VMEM_LIMIT),
        name="moba",
    )(qt, k, vt, kmean, onehot)


def _split3(a):
    hi = a.astype(bf16)
    r1 = a - hi.astype(f32)
    mid = r1.astype(bf16)
    lo = (r1 - mid.astype(f32)).astype(bf16)
    return hi, mid, lo


def _gla_kernel(gq_ref, gk_ref, gv_ref, gr_ref, la_ref, gn_ref, y_ref, st_ref):
    C = GLA_CHUNK
    H = GLA_HEADS
    KW = GLA_KWIDTH
    DV = GLA_VAL_DIM
    NSUB = C // GLA_SUB
    T = gq_ref.shape[1]

    @pl.when(pl.program_id(1) == 0)
    def _():
        st_ref[...] = jnp.zeros_like(st_ref)

    row = lax.broadcasted_iota(jnp.int32, (C, C), 0)
    col = lax.broadcasted_iota(jnp.int32, (C, C), 1)
    tril = (col <= row).astype(bf16)
    lane_head = lax.broadcasted_iota(jnp.int32, (C, KW), 1) // GLA_KEY_DIM
    rt = lax.broadcasted_iota(jnp.int32, (C, NSUB * C), 0)
    rc = lax.broadcasted_iota(jnp.int32, (C, NSUB * C), 1)
    keep = jnp.logical_and(rc // C == rt // GLA_SUB, rc % C <= rt)
    st_lane_head = lax.broadcasted_iota(jnp.int32, (DV, KW), 1) // GLA_KEY_DIM

    for c in range(T // C):
        rows = slice(c * C, (c + 1) * C)
        q = gq_ref[0, rows, :]
        k = gk_ref[0, rows, :]
        v = gv_ref[0, rows, :].astype(bf16)
        la = la_ref[0, rows, :]

        hi, mid, lo = _split3(la)
        G = _dot(tril, hi) + _dot(tril, mid) + _dot(tril, lo)
        g_last = G[C - 1:C, :]

        g_ref_rows = [G[i * GLA_SUB:i * GLA_SUB + 1, :] for i in range(NSUB)]
        g_own = jnp.concatenate(
            [jnp.broadcast_to(g, (GLA_SUB, KW)) for g in g_ref_rows], axis=0)
        q_in = q * jnp.exp(G - g_own)
        k_in = jnp.concatenate(
            [(k * jnp.exp(jnp.minimum(g - G, EXP_CAP))).astype(bf16) for g in g_ref_rows],
            axis=0)
        q_st = q * jnp.exp(G)
        k_st = (k * jnp.exp(g_last - G)).astype(bf16)

        qz_in = jnp.concatenate(
            [jnp.where(lane_head == h, q_in, 0.0).astype(bf16) for h in range(H)], axis=0)
        qz_st = jnp.concatenate(
            [jnp.where(lane_head == h, q_st, 0.0).astype(bf16) for h in range(H)], axis=0)

        r = _dot_nt(qz_in, k_in)
        keep_all = jnp.concatenate([keep] * H, axis=0)
        r = jnp.where(keep_all, r, 0.0).astype(bf16)
        v_rep = jnp.concatenate([v] * NSUB, axis=0)
        o_intra = _dot(r, v_rep)

        st = st_ref[...]
        o_inter = _dot_nt(qz_st, st.astype(bf16))

        upd = _dot_tn(v, k_st)
        new_st = st * jnp.exp(g_last)
        for h in range(H):
            new_st = new_st + jnp.where(st_lane_head == h, upd[h * DV:(h + 1) * DV, :], 0.0)
        st_ref[...] = new_st

        for h in range(H):
            o = o_intra[h * C:(h + 1) * C, h * DV:(h + 1) * DV] + o_inter[h * C:(h + 1) * C, :]
            o = o * lax.rsqrt(jnp.mean(o * o, axis=-1, keepdims=True) + EPS)
            o = o * gn_ref[:, h * DV:(h + 1) * DV]
            gr = gr_ref[0, rows, h * DV:(h + 1) * DV]
            y = o * (gr * jax.nn.sigmoid(gr))
            y_ref[0, rows, h * DV:(h + 1) * DV] = y.astype(y_ref.dtype)


def _gla_call(gq, gk, gv, gr, la, gn):
    B, S, _ = gq.shape
    T = GLA_TILE
    tok = lambda w: pl.BlockSpec((1, T, w), lambda b, i: (b, i, 0))
    return pl.pallas_call(
        _gla_kernel,
        grid=(B, S // T),
        in_specs=[tok(GLA_KWIDTH), tok(GLA_KWIDTH), tok(GLA_VWIDTH), tok(GLA_VWIDTH),
                  tok(GLA_KWIDTH), pl.BlockSpec((1, GLA_VWIDTH), lambda b, i: (0, 0))],
        out_specs=tok(GLA_VWIDTH),
        out_shape=jax.ShapeDtypeStruct((B, S, GLA_VWIDTH), bf16),
        scratch_shapes=[pltpu.VMEM((GLA_VAL_DIM, GLA_KWIDTH), f32)],
        compiler_params=pltpu.CompilerParams(
            dimension_semantics=("parallel", "arbitrary"), vmem_limit_bytes=VMEM_LIMIT),
        name="gla",
    )(gq, gk, gv, gr, la, gn)


def _ffn_kernel(x_ref, ym_ref, yg_ref, wo_m_ref, wo_g_ref, fg_ref, wup_ref, cw_ref, cb_ref,
                wdn_ref, og_ref, out_ref, u_ref):
    T = x_ref.shape[1]
    PAD = 8

    @pl.when(pl.program_id(1) == 0)
    def _():
        u_ref[0:PAD, :] = jnp.zeros((PAD, u_ref.shape[1]), f32)

    h = x_ref[0] + _dot(ym_ref[0], wo_m_ref[...]) + _dot(yg_ref[0], wo_g_ref[...])
    hn = _rms(h, fg_ref[...]).astype(bf16)
    u_ref[PAD:PAD + T, :] = _dot(hn, wup_ref[...])
    cw = cw_ref[...]
    conv = (cw[0:1] * u_ref[PAD - 2:PAD - 2 + T, :]
            + cw[1:2] * u_ref[PAD - 1:PAD - 1 + T, :]
            + cw[2:3] * u_ref[PAD:PAD + T, :]
            + cb_ref[...])
    u_ref[0:PAD, :] = u_ref[T:T + PAD, :]
    hg = conv[:, :D_FF]
    act = (hg * jax.nn.sigmoid(hg) * conv[:, D_FF:]).astype(bf16)
    y = h + _dot(act, wdn_ref[...])
    out_ref[0] = _rms(y, og_ref[...])


def _ffn_call(x, ym, yg, wo_m, wo_g, fg, wup, cw, cb, wdn, og):
    B, S, D = x.shape
    T = FFN_TILE
    tok = lambda w: pl.BlockSpec((1, T, w), lambda b, i: (b, i, 0))
    const = lambda a: pl.BlockSpec(a.shape, lambda b, i: (0,) * a.ndim, pipeline_mode=pl.Buffered(1))
    return pl.pallas_call(
        _ffn_kernel,
        grid=(B, S // T),
        in_specs=[tok(D), tok(MOBA_WIDTH), tok(GLA_VWIDTH),
                  const(wo_m), const(wo_g), const(fg), const(wup), const(cw), const(cb),
                  const(wdn), const(og)],
        out_specs=tok(D),
        out_shape=jax.ShapeDtypeStruct((B, S, D), x.dtype),
        scratch_shapes=[pltpu.VMEM((T + 8, 2 * D_FF), f32)],
        compiler_params=pltpu.CompilerParams(
            dimension_semantics=("parallel", "arbitrary"), vmem_limit_bytes=VMEM_LIMIT),
        name="out_ffn",
    )(x, ym, yg, wo_m, wo_g, fg, wup, cw, cb, wdn, og)


def _rope_tables(S):
    hd = MOBA_HEAD_DIM
    inv_freq = 1.0 / (ROPE_THETA ** (jnp.arange(0, hd, 2, dtype=f32) / hd))
    ang = jnp.arange(S).astype(f32)[:, None] * inv_freq[None, :]
    cos, sin = jnp.cos(ang), jnp.sin(ang)
    cos_r = jnp.tile(cos, (1, LANES // (hd // 2)))
    sin_r = jnp.tile(jnp.concatenate([-sin, sin], axis=1), (1, LANES // hd))
    return cos.T, sin.T, cos_r, sin_r


def kernel(x, attn_norm_g, w_in, w_gate_up, b_gate, gla_norm_g, w_out, ffn_norm_g, w_ffn_up,
           conv_w, conv_b, w_ffn_down, final_norm_g):
    B, S, D = x.shape
    l = 0
    o_mk = MOBA_WIDTH
    o_mv = 2 * MOBA_WIDTH
    o_gq = 3 * MOBA_WIDTH
    o_gg = o_gq + 2 * GLA_KWIDTH + 2 * GLA_VWIDTH
    w = w_in[l]
    wqt = w[:, :o_mk].T.astype(bf16)
    wk = w[:, o_mk:o_mv].astype(bf16)
    wvt = w[:, o_mv:o_gq].T.astype(bf16)
    wg = jnp.pad(w[:, o_gq:], ((0, 0), (0, LANES - GLA_GATE_RANK))).astype(bf16)
    wgu = jnp.pad(w_gate_up[l], ((0, LANES - GLA_GATE_RANK), (0, 0)))
    cos_t, sin_t, cos_r, sin_r = _rope_tables(S)

    qt, k, vt, kmean, gq, gk, gv, gr, la = _proj_call(
        x, attn_norm_g[l][None, :], wqt, wk, wvt, wg, wgu, b_gate[l][None, :],
        cos_t, sin_t, cos_r, sin_r)
    kmean = kmean.reshape(B, S // MOBA_BLOCK, MOBA_WIDTH)

    y_moba = _moba_call(qt, k, vt, kmean)
    y_gla = _gla_call(gq, gk, gv, gr, la, gla_norm_g[l].reshape(1, GLA_VWIDTH))

    wo = w_out[l].astype(bf16)
    return _ffn_call(
        x, y_moba, y_gla, wo[:MOBA_WIDTH], wo[MOBA_WIDTH:], ffn_norm_g[l][None, :],
        w_ffn_up[l].astype(bf16), conv_w[l], conv_b[l][None, :], w_ffn_down[l].astype(bf16),
        final_norm_g[None, :])
```

```python
import functools
import math

import jax
import jax.numpy as jnp
from jax import lax
from jax.experimental import pallas as pl
from jax.experimental.pallas import tpu as pltpu

D_MODEL = 1024
MOBA_HEAD_DIM = 64
MOBA_HEADS = 8
MOBA_WIDTH = 512
MOBA_BLOCK = 256
MOBA_TOPK = 3
MOBA_UNROLL = 4
MOBA_ROWS = 32
GLA_HEADS = 4
GLA_VAL_DIM = 128
GLA_KEY_DIM = 64
GLA_VWIDTH = 512
GLA_KWIDTH = 256
GLA_GATE_RANK = 16
GLA_GATE_TAU = 16.0
GLA_CHUNK = 64
GLA_SUB = 16
D_FF = 2816
CONV_WIDTH = 3
ROPE_THETA = 10000.0
EPS = 1e-6

LANES = 128
LOG2_E = 1.4426950408889634
NEG_BIG = -1e30
EXP_CAP = 80.0

PROJ_TILE = 512
GLA_TILE = 512
FFN_TILE = 256
VMEM_LIMIT = 56 * 1024 * 1024

f32 = jnp.float32
bf16 = jnp.bfloat16


def _dot(a, b):
    return jnp.dot(a, b, preferred_element_type=f32)


def _dot_nt(a, b):
    return lax.dot_general(a, b, (((1,), (1,)), ((), ())), preferred_element_type=f32)


def _dot_tn(a, b):
    return lax.dot_general(a, b, (((0,), (0,)), ((), ())), preferred_element_type=f32)


def _rms(xf, g):
    return xf * lax.rsqrt(jnp.mean(xf * xf, axis=-1, keepdims=True) + EPS) * g


def _proj_kernel(x_ref, g_ref, wqt_ref, wk_ref, wvt_ref, wg_ref, wgu_ref, bg_ref,
                 cos_t_ref, sin_t_ref, cos_r_ref, sin_r_ref,
                 qt_ref, k_ref, vt_ref, kmean_ref, gq_ref, gk_ref, gv_ref, gr_ref, la_ref):
    T = x_ref.shape[1]
    xn = _rms(x_ref[0], g_ref[...]).astype(bf16)

    qt = _dot_nt(wqt_ref[...], xn)
    cos_t = cos_t_ref[...]
    sin_t = sin_t_ref[...]
    half = MOBA_HEAD_DIM // 2
    scale = MOBA_HEAD_DIM ** -0.5 * LOG2_E
    nblk = T // MOBA_BLOCK
    for h in range(MOBA_HEADS):
        r0 = h * MOBA_HEAD_DIM
        t1 = qt[r0:r0 + half]
        t2 = qt[r0 + half:r0 + MOBA_HEAD_DIM]
        lo = ((t1 * cos_t - t2 * sin_t) * scale).astype(bf16)
        hi = ((t2 * cos_t + t1 * sin_t) * scale).astype(bf16)
        for j in range(nblk):
            qt_ref[0, j, r0:r0 + half, :] = lo[:, j * MOBA_BLOCK:(j + 1) * MOBA_BLOCK]
            qt_ref[0, j, r0 + half:r0 + MOBA_HEAD_DIM, :] = hi[:, j * MOBA_BLOCK:(j + 1) * MOBA_BLOCK]

    k = _dot(xn, wk_ref[...])
    cos_r = cos_r_ref[...]
    sin_r = sin_r_ref[...]
    lane = lax.broadcasted_iota(jnp.int32, (T, LANES), 1)
    first_half = (lane % MOBA_HEAD_DIM) < half
    for p in range(MOBA_WIDTH // LANES):
        kp = k[:, p * LANES:(p + 1) * LANES]
        rot = jnp.where(first_half, pltpu.roll(kp, LANES - half, 1), pltpu.roll(kp, half, 1))
        kr = kp * cos_r + rot * sin_r
        k_ref[0, :, p * LANES:(p + 1) * LANES] = kr.astype(bf16)
        for j in range(nblk):
            kmean_ref[0, 0, j:j + 1, p * LANES:(p + 1) * LANES] = jnp.mean(
                kr[j * MOBA_BLOCK:(j + 1) * MOBA_BLOCK], axis=0, keepdims=True)

    vt = _dot_nt(wvt_ref[...], xn).astype(bf16)
    for j in range(nblk):
        vt_ref[0, j] = vt[:, j * MOBA_BLOCK:(j + 1) * MOBA_BLOCK]

    pg = _dot(xn, wg_ref[...])
    o_gk = GLA_KWIDTH
    o_gv = o_gk + GLA_KWIDTH
    o_gr = o_gv + GLA_VWIDTH
    o_gg = o_gr + GLA_VWIDTH
    gq_ref[0] = pg[:, :o_gk] * (GLA_KEY_DIM ** -0.5)
    gk_ref[0] = pg[:, o_gk:o_gv]
    gv_ref[0] = pg[:, o_gv:o_gr]
    gr_ref[0] = pg[:, o_gr:o_gg]
    gate_lr = pg[:, o_gg:]
    z = jnp.dot(gate_lr, wgu_ref[...], preferred_element_type=f32,
                precision=lax.Precision.HIGHEST) + bg_ref[...]
    log_sig = jnp.minimum(z, 0.0) - jnp.log1p(jnp.exp(-jnp.abs(z)))
    la_ref[0] = log_sig * (1.0 / GLA_GATE_TAU)


def _proj_call(x, g, wqt, wk, wvt, wg, wgu, bg, cos_t, sin_t, cos_r, sin_r):
    B, S, D = x.shape
    T = PROJ_TILE
    nb_t = T // MOBA_BLOCK
    const = lambda shape: pl.BlockSpec(shape, lambda b, i: (0,) * len(shape))
    tok = lambda w: pl.BlockSpec((1, T, w), lambda b, i: (b, i, 0))
    out_shape = (
        jax.ShapeDtypeStruct((B, S // MOBA_BLOCK, MOBA_WIDTH, MOBA_BLOCK), bf16),
        jax.ShapeDtypeStruct((B, S, MOBA_WIDTH), bf16),
        jax.ShapeDtypeStruct((B, S // MOBA_BLOCK, MOBA_WIDTH, MOBA_BLOCK), bf16),
        jax.ShapeDtypeStruct((B, S // T, nb_t, MOBA_WIDTH), f32),
        jax.ShapeDtypeStruct((B, S, GLA_KWIDTH), f32),
        jax.ShapeDtypeStruct((B, S, GLA_KWIDTH), f32),
        jax.ShapeDtypeStruct((B, S, GLA_VWIDTH), f32),
        jax.ShapeDtypeStruct((B, S, GLA_VWIDTH), f32),
        jax.ShapeDtypeStruct((B, S, GLA_KWIDTH), f32),
    )
    out_specs = (
        pl.BlockSpec((1, nb_t, MOBA_WIDTH, MOBA_BLOCK), lambda b, i: (b, i, 0, 0)),
        tok(MOBA_WIDTH),
        pl.BlockSpec((1, nb_t, MOBA_WIDTH, MOBA_BLOCK), lambda b, i: (b, i, 0, 0)),
        pl.BlockSpec((1, 1, nb_t, MOBA_WIDTH), lambda b, i: (b, i, 0, 0)),
        tok(GLA_KWIDTH), tok(GLA_KWIDTH), tok(GLA_VWIDTH), tok(GLA_VWIDTH), tok(GLA_KWIDTH),
    )
    in_specs = [
        tok(D),
        const((1, D)),
        const(wqt.shape), const(wk.shape), const(wvt.shape), const(wg.shape),
        const(wgu.shape), const((1, GLA_KWIDTH)),
        pl.BlockSpec((MOBA_HEAD_DIM // 2, T), lambda b, i: (0, i)),
        pl.BlockSpec((MOBA_HEAD_DIM // 2, T), lambda b, i: (0, i)),
        pl.BlockSpec((T, LANES), lambda b, i: (i, 0)),
        pl.BlockSpec((T, LANES), lambda b, i: (i, 0)),
    ]
    return pl.pallas_call(
        _proj_kernel,
        grid=(B, S // T),
        in_specs=in_specs,
        out_specs=out_specs,
        out_shape=out_shape,
        compiler_params=pltpu.CompilerParams(
            dimension_semantics=("parallel", "parallel"), vmem_limit_bytes=VMEM_LIMIT),
        name="in_proj",
    )(x, g, wqt, wk, wvt, wg, wgu, bg, cos_t, sin_t, cos_r, sin_r)


def _moba_kernel(qt_ref, k_ref, vt_ref, kmean_ref, onehot_ref, o_ref, s_ref, p_ref):
    NT = qt_ref.shape[1]
    QB = qt_ref.shape[3]
    NB = kmean_ref.shape[1]
    HD = MOBA_HEAD_DIM
    U = MOBA_UNROLL
    BLK = MOBA_BLOCK
    ROWS = MOBA_ROWS

    km = kmean_ref[0]
    km_hi = km.astype(bf16)
    km_lo = (km - km_hi.astype(f32)).astype(bf16)
    blk = lax.broadcasted_iota(jnp.int32, (NB, QB), 0)
    zeros_half = jnp.zeros((HD, QB), bf16)
    zeros_tail = jnp.zeros((LANES - NB, QB), bf16)
    in_causal = (lax.broadcasted_iota(jnp.int32, (BLK, QB), 0)
                 <= lax.broadcasted_iota(jnp.int32, (BLK, QB), 1))

    def select(qi):
        qzs = []
        for h in range(2):
            q_h = qt_ref[0, qi, h * HD:(h + 1) * HD, :]
            qz = jnp.concatenate([q_h, zeros_half] if h == 0 else [zeros_half, q_h], axis=0)
            gate = _dot(km_hi, qz) + _dot(km_lo, qz)
            gate = jnp.where(blk < qi, gate, -jnp.inf)
            sel = jnp.zeros((NB, QB), jnp.bool_)
            for _ in range(MOBA_TOPK):
                top = jnp.max(gate, axis=0, keepdims=True)
                idx = jnp.min(jnp.where(gate == top, blk, NB), axis=0, keepdims=True)
                pick = blk == idx
                sel = jnp.logical_or(sel, jnp.logical_and(pick, top > -jnp.inf))
                gate = jnp.where(pick, -jnp.inf, gate)
            bias = jnp.where(sel, 0.0, NEG_BIG).astype(bf16)
            qzs.append(jnp.concatenate([qz, bias, zeros_tail], axis=0))
        return tuple(qzs)

    def block_scores(qzs, j, u, slot, tops):
        r0 = pl.multiple_of(j * BLK, BLK)
        k_aug = jnp.concatenate(
            [k_ref[0, pl.ds(r0, BLK), :], onehot_ref[pl.ds(r0, BLK), :]], axis=1)
        out = []
        for h in range(2):
            s = _dot(k_aug, qzs[h])
            s_ref[slot, h, u * BLK:(u + 1) * BLK, :] = s
            top = jnp.max(s, axis=0, keepdims=True)
            out.append(top if tops is None else jnp.maximum(tops[h], top))
        return tuple(out)

    def own_scores(qzs, qi):
        k_own = k_ref[0, pl.ds(pl.multiple_of(qi * BLK, BLK), BLK), :]
        out = []
        for h in range(2):
            s = jnp.where(in_causal, _dot(k_own, qzs[h][:LANES]), NEG_BIG)
            out.append((s, jnp.max(s, axis=0, keepdims=True)))
        return tuple(out)

    def block_softmax(load_rows, u, m_new, part, h):
        for r in range(u * BLK, (u + 1) * BLK, ROWS):
            e = jnp.exp2(load_rows(r) - m_new)
            part = part + e
            p_ref[h, r:r + ROWS, :] = e.astype(bf16)
        return part

    def block_values(j, u, pv, h):
        d = _dot(vt_ref[0, j, h * HD:(h + 1) * HD, :], p_ref[h, u * BLK:(u + 1) * BLK, :])
        return d if pv is None else pv + d

    def group(t, tops, mls, accs, load_rows, prefetch):
        m_new = [jnp.maximum(mls[h][0], tops[h]) for h in range(2)]
        alpha = [jnp.exp2(mls[h][0] - m_new[h]) for h in range(2)]
        part = [jnp.zeros((ROWS, QB), f32)] * 2
        pv = [None, None]
        tops_next = None
        for u in range(U):
            tops_next = prefetch(u, tops_next)
            for h in range(2):
                part[h] = block_softmax(load_rows[h], u, m_new[h], part[h], h)
            for h in range(2):
                pv[h] = block_values(t * U + u, u, pv[h], h)
        mls = tuple((m_new[h], alpha[h] * mls[h][1] + jnp.sum(part[h], axis=0, keepdims=True))
                    for h in range(2))
        accs = tuple(alpha[h] * accs[h] + pv[h] for h in range(2))
        return tops_next, mls, accs

    def slot_rows(slot):
        return [lambda r, h=h: s_ref[slot, h, r:r + ROWS, :] for h in range(2)]

    ml0 = (jnp.full((1, QB), -jnp.inf, f32), jnp.zeros((1, QB), f32))
    acc0 = jnp.zeros((HD, QB), f32)

    def tile(qi, carry):
        qzs, qzs_next, own, tops0, phase = carry
        n_groups = jnp.maximum((qi + U - 1) // U, 1)

        def step(t, c, cur):
            tops, mls, accs = c
            return group(t, tops, mls, accs, slot_rows(cur),
                         lambda u, tn: block_scores(qzs, (t + 1) * U + u, u, 1 - cur, tn))

        def body(t, c):
            return lax.cond((t + phase) % 2 == 0,
                            functools.partial(step, t, cur=0), functools.partial(step, t, cur=1), c)

        state = lax.fori_loop(0, n_groups - 1, body, (tops0, (ml0, ml0), (acc0, acc0)))

        def boundary(c, cur):
            tops, mls, accs = c
            m_new = [jnp.maximum(mls[h][0], jnp.maximum(tops[h], own[h][1])) for h in range(2)]
            alpha = [jnp.exp2(mls[h][0] - m_new[h]) for h in range(2)]
            part = [jnp.zeros((ROWS, QB), f32)] * 2
            pv = [None, None]
            tops_next = None
            load_rows = slot_rows(cur)
            q_next = jnp.minimum(qi + 1, NT - 1)
            for u in range(U + 1):
                if u < U:
                    tops_next = block_scores(qzs_next, u, u, 1 - cur, tops_next)
                else:
                    own_next = own_scores(qzs_next, q_next)
                for h in range(2):
                    if u == 0:
                        for r in range(0, BLK, ROWS):
                            e = jnp.exp2(own[h][0][r:r + ROWS] - m_new[h])
                            part[h] = part[h] + e
                            p_ref[h, U * BLK + r:U * BLK + r + ROWS, :] = e.astype(bf16)
                    else:
                        part[h] = block_softmax(load_rows[h], u - 1, m_new[h], part[h], h)
                for h in range(2):
                    if u == 0:
                        pv[h] = _dot(vt_ref[0, qi, h * HD:(h + 1) * HD, :],
                                     p_ref[h, U * BLK:(U + 1) * BLK, :])
                    else:
                        pv[h] = block_values((n_groups - 1) * U + u - 1, u - 1, pv[h], h)
            qzs_after = select(jnp.minimum(qi + 2, NT - 1))
            outs = []
            for h in range(2):
                l = alpha[h] * mls[h][1] + jnp.sum(part[h], axis=0, keepdims=True)
                outs.append((alpha[h] * accs[h] + pv[h]) / l)
            o_t = jnp.concatenate(outs, axis=0)
            r0 = pl.multiple_of(qi * BLK, BLK)
            o_ref[0, pl.ds(r0, BLK), :] = o_t.T.astype(o_ref.dtype)
            return qzs_after, own_next, tops_next

        last_slot = (n_groups - 1 + phase) % 2
        qzs_after, own_next, tops_next = lax.cond(
            last_slot == 0, functools.partial(boundary, cur=0), functools.partial(boundary, cur=1),
            state)
        return qzs_next, qzs_after, own_next, tops_next, 1 - last_slot

    qzs0 = select(0)
    qzs1 = select(1)
    tops0 = None
    for u in range(U):
        tops0 = block_scores(qzs0, u, u, 0, tops0)
    lax.fori_loop(0, NT, tile, (qzs0, qzs1, own_scores(qzs0, 0), tops0, jnp.int32(0)))


def _moba_call(qt, k, vt, kmean):
    B, S, W = k.shape
    NB = S // MOBA_BLOCK
    QB = MOBA_BLOCK
    n_hp = W // LANES
    assert NB % MOBA_UNROLL == 0 and NB <= LANES
    onehot = (jnp.arange(S)[:, None] // MOBA_BLOCK == jnp.arange(LANES)[None, :]).astype(bf16)
    return pl.pallas_call(
        _moba_kernel,
        grid=(B, n_hp),
        in_specs=[
            pl.BlockSpec((1, NB, LANES, QB), lambda b, hp: (b, 0, hp, 0)),
            pl.BlockSpec((1, S, LANES), lambda b, hp: (b, 0, hp)),
            pl.BlockSpec((1, NB, LANES, MOBA_BLOCK), lambda b, hp: (b, 0, hp, 0)),
            pl.BlockSpec((1, NB, LANES), lambda b, hp: (b, 0, hp)),
            pl.BlockSpec((S, LANES), lambda b, hp: (0, 0)),
        ],
        out_specs=pl.BlockSpec((1, S, LANES), lambda b, hp: (b, 0, hp)),
        out_shape=jax.ShapeDtypeStruct((B, S, W), bf16),
        scratch_shapes=[
            pltpu.VMEM((2, 2, MOBA_UNROLL * MOBA_BLOCK, QB), f32),
            pltpu.VMEM((2, (MOBA_UNROLL + 1) * MOBA_BLOCK, QB), bf16),
        ],
        compiler_params=pltpu.CompilerParams(
            dimension_semantics=("parallel", "parallel"),
            vmem_limit_bytes=VMEM_LIMIT),
        name="moba",
    )(qt, k, vt, kmean, onehot)


def _split3(a):
    hi = a.astype(bf16)
    r1 = a - hi.astype(f32)
    mid = r1.astype(bf16)
    lo = (r1 - mid.astype(f32)).astype(bf16)
    return hi, mid, lo


def _gla_kernel(gq_ref, gk_ref, gv_ref, gr_ref, la_ref, gn_ref, y_ref, st_ref):
    C = GLA_CHUNK
    H = GLA_HEADS
    KW = GLA_KWIDTH
    DV = GLA_VAL_DIM
    NSUB = C // GLA_SUB
    T = gq_ref.shape[1]

    @pl.when(pl.program_id(1) == 0)
    def _():
        st_ref[...] = jnp.zeros_like(st_ref)

    row = lax.broadcasted_iota(jnp.int32, (C, C), 0)
    col = lax.broadcasted_iota(jnp.int32, (C, C), 1)
    tril = (col <= row).astype(bf16)
    lane_head = lax.broadcasted_iota(jnp.int32, (C, KW), 1) // GLA_KEY_DIM
    rt = lax.broadcasted_iota(jnp.int32, (C, NSUB * C), 0)
    rc = lax.broadcasted_iota(jnp.int32, (C, NSUB * C), 1)
    keep = jnp.logical_and(rc // C == rt // GLA_SUB, rc % C <= rt)
    st_lane_head = lax.broadcasted_iota(jnp.int32, (DV, KW), 1) // GLA_KEY_DIM

    for c in range(T // C):
        rows = slice(c * C, (c + 1) * C)
        q = gq_ref[0, rows, :]
        k = gk_ref[0, rows, :]
        v = gv_ref[0, rows, :].astype(bf16)
        la = la_ref[0, rows, :]

        hi, mid, lo = _split3(la)
        G = _dot(tril, hi) + _dot(tril, mid) + _dot(tril, lo)
        g_last = G[C - 1:C, :]

        g_ref_rows = [G[i * GLA_SUB:i * GLA_SUB + 1, :] for i in range(NSUB)]
        g_own = jnp.concatenate(
            [jnp.broadcast_to(g, (GLA_SUB, KW)) for g in g_ref_rows], axis=0)
        q_in = q * jnp.exp(G - g_own)
        k_in = jnp.concatenate(
            [(k * jnp.exp(jnp.minimum(g - G, EXP_CAP))).astype(bf16) for g in g_ref_rows],
            axis=0)
        q_st = q * jnp.exp(G)
        k_st = (k * jnp.exp(g_last - G)).astype(bf16)

        qz_in = jnp.concatenate(
            [jnp.where(lane_head == h, q_in, 0.0).astype(bf16) for h in range(H)], axis=0)
        qz_st = jnp.concatenate(
            [jnp.where(lane_head == h, q_st, 0.0).astype(bf16) for h in range(H)], axis=0)

        r = _dot_nt(qz_in, k_in)
        keep_all = jnp.concatenate([keep] * H, axis=0)
        r = jnp.where(keep_all, r, 0.0).astype(bf16)
        v_rep = jnp.concatenate([v] * NSUB, axis=0)
        o_intra = _dot(r, v_rep)

        st = st_ref[...]
        o_inter = _dot_nt(qz_st, st.astype(bf16))

        upd = _dot_tn(v, k_st)
        new_st = st * jnp.exp(g_last)
        for h in range(H):
            new_st = new_st + jnp.where(st_lane_head == h, upd[h * DV:(h + 1) * DV, :], 0.0)
        st_ref[...] = new_st

        for h in range(H):
            o = o_intra[h * C:(h + 1) * C, h * DV:(h + 1) * DV] + o_inter[h * C:(h + 1) * C, :]
            o = o * lax.rsqrt(jnp.mean(o * o, axis=-1, keepdims=True) + EPS)
            o = o * gn_ref[:, h * DV:(h + 1) * DV]
            gr = gr_ref[0, rows, h * DV:(h + 1) * DV]
            y = o * (gr * jax.nn.sigmoid(gr))
            y_ref[0, rows, h * DV:(h + 1) * DV] = y.astype(y_ref.dtype)


def _gla_call(gq, gk, gv, gr, la, gn):
    B, S, _ = gq.shape
    T = GLA_TILE
    tok = lambda w: pl.BlockSpec((1, T, w), lambda b, i: (b, i, 0))
    return pl.pallas_call(
        _gla_kernel,
        grid=(B, S // T),
        in_specs=[tok(GLA_KWIDTH), tok(GLA_KWIDTH), tok(GLA_VWIDTH), tok(GLA_VWIDTH),
                  tok(GLA_KWIDTH), pl.BlockSpec((1, GLA_VWIDTH), lambda b, i: (0, 0))],
        out_specs=tok(GLA_VWIDTH),
        out_shape=jax.ShapeDtypeStruct((B, S, GLA_VWIDTH), bf16),
        scratch_shapes=[pltpu.VMEM((GLA_VAL_DIM, GLA_KWIDTH), f32)],
        compiler_params=pltpu.CompilerParams(
            dimension_semantics=("parallel", "arbitrary"), vmem_limit_bytes=VMEM_LIMIT),
        name="gla",
    )(gq, gk, gv, gr, la, gn)


def _ffn_kernel(x_ref, ym_ref, yg_ref, wo_m_ref, wo_g_ref, fg_ref, wup_ref, cw_ref, cb_ref,
                wdn_ref, og_ref, out_ref, u_ref):
    T = x_ref.shape[1]
    PAD = 8

    @pl.when(pl.program_id(1) == 0)
    def _():
        u_ref[0:PAD, :] = jnp.zeros((PAD, u_ref.shape[1]), f32)

    h = x_ref[0] + _dot(ym_ref[0], wo_m_ref[...]) + _dot(yg_ref[0], wo_g_ref[...])
    hn = _rms(h, fg_ref[...]).astype(bf16)
    u_ref[PAD:PAD + T, :] = _dot(hn, wup_ref[...])
    cw = cw_ref[...]
    conv = (cw[0:1] * u_ref[PAD - 2:PAD - 2 + T, :]
            + cw[1:2] * u_ref[PAD - 1:PAD - 1 + T, :]
            + cw[2:3] * u_ref[PAD:PAD + T, :]
            + cb_ref[...])
    u_ref[0:PAD, :] = u_ref[T:T + PAD, :]
    hg = conv[:, :D_FF]
    act = (hg * jax.nn.sigmoid(hg) * conv[:, D_FF:]).astype(bf16)
    y = h + _dot(act, wdn_ref[...])
    out_ref[0] = _rms(y, og_ref[...])


def _ffn_call(x, ym, yg, wo_m, wo_g, fg, wup, cw, cb, wdn, og):
    B, S, D = x.shape
    T = FFN_TILE
    tok = lambda w: pl.BlockSpec((1, T, w), lambda b, i: (b, i, 0))
    const = lambda a: pl.BlockSpec(a.shape, lambda b, i: (0,) * a.ndim, pipeline_mode=pl.Buffered(1))
    return pl.pallas_call(
        _ffn_kernel,
        grid=(B, S // T),
        in_specs=[tok(D), tok(MOBA_WIDTH), tok(GLA_VWIDTH),
                  const(wo_m), const(wo_g), const(fg), const(wup), const(cw), const(cb),
                  const(wdn), const(og)],
        out_specs=tok(D),
        out_shape=jax.ShapeDtypeStruct((B, S, D), x.dtype),
        scratch_shapes=[pltpu.VMEM((T + 8, 2 * D_FF), f32)],
        compiler_params=pltpu.CompilerParams(
            dimension_semantics=("parallel", "arbitrary"), vmem_limit_bytes=VMEM_LIMIT),
        name="out_ffn",
    )(x, ym, yg, wo_m, wo_g, fg, wup, cw, cb, wdn, og)


def _rope_tables(S):
    hd = MOBA_HEAD_DIM
    inv_freq = 1.0 / (ROPE_THETA ** (jnp.arange(0, hd, 2, dtype=f32) / hd))
    ang = jnp.arange(S).astype(f32)[:, None] * inv_freq[None, :]
    cos, sin = jnp.cos(ang), jnp.sin(ang)
    cos_r = jnp.tile(cos, (1, LANES // (hd // 2)))
    sin_r = jnp.tile(jnp.concatenate([-sin, sin], axis=1), (1, LANES // hd))
    return cos.T, sin.T, cos_r, sin_r


def kernel(x, attn_norm_g, w_in, w_gate_up, b_gate, gla_norm_g, w_out, ffn_norm_g, w_ffn_up,
           conv_w, conv_b, w_ffn_down, final_norm_g):
    B, S, D = x.shape
    l = 0
    o_mk = MOBA_WIDTH
    o_mv = 2 * MOBA_WIDTH
    o_gq = 3 * MOBA_WIDTH
    o_gg = o_gq + 2 * GLA_KWIDTH + 2 * GLA_VWIDTH
    w = w_in[l]
    wqt = w[:, :o_mk].T.astype(bf16)
    wk = w[:, o_mk:o_mv].astype(bf16)
    wvt = w[:, o_mv:o_gq].T.astype(bf16)
    wg = jnp.pad(w[:, o_gq:], ((0, 0), (0, LANES - GLA_GATE_RANK))).astype(bf16)
    wgu = jnp.pad(w_gate_up[l], ((0, LANES - GLA_GATE_RANK), (0, 0)))
    cos_t, sin_t, cos_r, sin_r = _rope_tables(S)

    qt, k, vt, kmean, gq, gk, gv, gr, la = _proj_call(
        x, attn_norm_g[l][None, :], wqt, wk, wvt, wg, wgu, b_gate[l][None, :],
        cos_t, sin_t, cos_r, sin_r)
    kmean = kmean.reshape(B, S // MOBA_BLOCK, MOBA_WIDTH)

    y_moba = _moba_call(qt, k, vt, kmean)
    y_gla = _gla_call(gq, gk, gv, gr, la, gla_norm_g[l].reshape(1, GLA_VWIDTH))

    wo = w_out[l].astype(bf16)
    return _ffn_call(
        x, y_moba, y_gla, wo[:MOBA_WIDTH], wo[MOBA_WIDTH:], ffn_norm_g[l][None, :],
        w_ffn_up[l].astype(bf16), conv_w[l], conv_b[l][None, :], w_ffn_down[l].astype(bf16),
        final_norm_g[None, :])
```

```python
import functools
import math

import jax
import jax.numpy as jnp
from jax import lax
from jax.experimental import pallas as pl
from jax.experimental.pallas import tpu as pltpu

D_MODEL = 1024
MOBA_HEAD_DIM = 64
MOBA_HEADS = 8
MOBA_WIDTH = 512
MOBA_BLOCK = 256
MOBA_TOPK = 3
MOBA_UNROLL = 4
MOBA_ROWS = 32
GLA_HEADS = 4
GLA_VAL_DIM = 128
GLA_KEY_DIM = 64
GLA_VWIDTH = 512
GLA_KWIDTH = 256
GLA_GATE_RANK = 16
GLA_GATE_TAU = 16.0
GLA_CHUNK = 64
GLA_SUB = 16
D_FF = 2816
CONV_WIDTH = 3
ROPE_THETA = 10000.0
EPS = 1e-6

LANES = 128
LOG2_E = 1.4426950408889634
NEG_BIG = -1e30
EXP_CAP = 80.0

PROJ_TILE = 512
GLA_TILE = 512
FFN_TILE = 512
VMEM_LIMIT = 56 * 1024 * 1024

f32 = jnp.float32
bf16 = jnp.bfloat16


def _dot(a, b):
    return jnp.dot(a, b, preferred_element_type=f32)


def _dot_nt(a, b):
    return lax.dot_general(a, b, (((1,), (1,)), ((), ())), preferred_element_type=f32)


def _dot_tn(a, b):
    return lax.dot_general(a, b, (((0,), (0,)), ((), ())), preferred_element_type=f32)


def _rms(xf, g):
    return xf * lax.rsqrt(jnp.mean(xf * xf, axis=-1, keepdims=True) + EPS) * g


def _proj_kernel(x_ref, g_ref, wqt_ref, wk_ref, wvt_ref, wg_ref, wgu_hi_ref, wgu_lo_ref, bg_ref,
                 cos_t_ref, sin_t_ref, cos_r_ref, sin_r_ref,
                 qt_ref, k_ref, vt_ref, kmean_ref, gq_ref, gk_ref, gv_ref, gr_ref, la_ref):
    T = x_ref.shape[1]
    xn = _rms(x_ref[0], g_ref[...]).astype(bf16)

    pg = _dot(xn, wg_ref[...])
    o_gk = GLA_KWIDTH
    o_gv = o_gk + GLA_KWIDTH
    o_gr = o_gv + GLA_VWIDTH
    o_gg = o_gr + GLA_VWIDTH
    gq_ref[0] = pg[:, :o_gk] * (GLA_KEY_DIM ** -0.5)
    gk_ref[0] = pg[:, o_gk:o_gv]
    gv_ref[0] = pg[:, o_gv:o_gr]
    gr_ref[0] = pg[:, o_gr:o_gg]
    gate_lr = pg[:, o_gg:]
    lr_hi = gate_lr.astype(bf16)
    lr_lo = (gate_lr - lr_hi.astype(f32)).astype(bf16)
    z = (_dot(lr_hi, wgu_hi_ref[...]) + _dot(lr_lo, wgu_hi_ref[...]) + _dot(lr_hi, wgu_lo_ref[...])
         + bg_ref[...])
    log_sig = jnp.minimum(z, 0.0) - jnp.log1p(jnp.exp(-jnp.abs(z)))
    la_ref[0] = log_sig * (1.0 / GLA_GATE_TAU)

    qt = _dot_nt(wqt_ref[...], xn)
    cos_t = cos_t_ref[...]
    sin_t = sin_t_ref[...]
    half = MOBA_HEAD_DIM // 2
    scale = MOBA_HEAD_DIM ** -0.5 * LOG2_E
    nblk = T // MOBA_BLOCK
    for h in range(MOBA_HEADS):
        r0 = h * MOBA_HEAD_DIM
        t1 = qt[r0:r0 + half]
        t2 = qt[r0 + half:r0 + MOBA_HEAD_DIM]
        lo = ((t1 * cos_t - t2 * sin_t) * scale).astype(bf16)
        hi = ((t2 * cos_t + t1 * sin_t) * scale).astype(bf16)
        for j in range(nblk):
            qt_ref[0, j, r0:r0 + half, :] = lo[:, j * MOBA_BLOCK:(j + 1) * MOBA_BLOCK]
            qt_ref[0, j, r0 + half:r0 + MOBA_HEAD_DIM, :] = hi[:, j * MOBA_BLOCK:(j + 1) * MOBA_BLOCK]

    k = _dot(xn, wk_ref[...])
    cos_r = cos_r_ref[...]
    sin_r = sin_r_ref[...]
    lane = lax.broadcasted_iota(jnp.int32, (T, LANES), 1)
    first_half = (lane % MOBA_HEAD_DIM) < half
    for p in range(MOBA_WIDTH // LANES):
        kp = k[:, p * LANES:(p + 1) * LANES]
        rot = jnp.where(first_half, pltpu.roll(kp, LANES - half, 1), pltpu.roll(kp, half, 1))
        kr = kp * cos_r + rot * sin_r
        k_ref[0, :, p * LANES:(p + 1) * LANES] = kr.astype(bf16)
        for j in range(nblk):
            kmean_ref[0, 0, j:j + 1, p * LANES:(p + 1) * LANES] = jnp.mean(
                kr[j * MOBA_BLOCK:(j + 1) * MOBA_BLOCK], axis=0, keepdims=True)

    vt = _dot_nt(wvt_ref[...], xn).astype(bf16)
    for j in range(nblk):
        vt_ref[0, j] = vt[:, j * MOBA_BLOCK:(j + 1) * MOBA_BLOCK]


def _proj_call(x, g, wqt, wk, wvt, wg, wgu_hi, wgu_lo, bg, cos_t, sin_t, cos_r, sin_r):
    B, S, D = x.shape
    T = PROJ_TILE
    nb_t = T // MOBA_BLOCK
    const = lambda shape: pl.BlockSpec(shape, lambda b, i: (0,) * len(shape))
    tok = lambda w: pl.BlockSpec((1, T, w), lambda b, i: (b, i, 0))
    out_shape = (
        jax.ShapeDtypeStruct((B, S // MOBA_BLOCK, MOBA_WIDTH, MOBA_BLOCK), bf16),
        jax.ShapeDtypeStruct((B, S, MOBA_WIDTH), bf16),
        jax.ShapeDtypeStruct((B, S // MOBA_BLOCK, MOBA_WIDTH, MOBA_BLOCK), bf16),
        jax.ShapeDtypeStruct((B, S // T, nb_t, MOBA_WIDTH), f32),
        jax.ShapeDtypeStruct((B, S, GLA_KWIDTH), f32),
        jax.ShapeDtypeStruct((B, S, GLA_KWIDTH), f32),
        jax.ShapeDtypeStruct((B, S, GLA_VWIDTH), f32),
        jax.ShapeDtypeStruct((B, S, GLA_VWIDTH), f32),
        jax.ShapeDtypeStruct((B, S, GLA_KWIDTH), f32),
    )
    out_specs = (
        pl.BlockSpec((1, nb_t, MOBA_WIDTH, MOBA_BLOCK), lambda b, i: (b, i, 0, 0)),
        tok(MOBA_WIDTH),
        pl.BlockSpec((1, nb_t, MOBA_WIDTH, MOBA_BLOCK), lambda b, i: (b, i, 0, 0)),
        pl.BlockSpec((1, 1, nb_t, MOBA_WIDTH), lambda b, i: (b, i, 0, 0)),
        tok(GLA_KWIDTH), tok(GLA_KWIDTH), tok(GLA_VWIDTH), tok(GLA_VWIDTH), tok(GLA_KWIDTH),
    )
    in_specs = [
        tok(D),
        const((1, D)),
        const(wqt.shape), const(wk.shape), const(wvt.shape), const(wg.shape),
        const(wgu_hi.shape), const(wgu_lo.shape), const((1, GLA_KWIDTH)),
        pl.BlockSpec((MOBA_HEAD_DIM // 2, T), lambda b, i: (0, i)),
        pl.BlockSpec((MOBA_HEAD_DIM // 2, T), lambda b, i: (0, i)),
        pl.BlockSpec((T, LANES), lambda b, i: (i, 0)),
        pl.BlockSpec((T, LANES), lambda b, i: (i, 0)),
    ]
    return pl.pallas_call(
        _proj_kernel,
        grid=(B, S // T),
        in_specs=in_specs,
        out_specs=out_specs,
        out_shape=out_shape,
        compiler_params=pltpu.CompilerParams(
            dimension_semantics=("parallel", "parallel"), vmem_limit_bytes=VMEM_LIMIT),
        name="in_proj",
    )(x, g, wqt, wk, wvt, wg, wgu_hi, wgu_lo, bg, cos_t, sin_t, cos_r, sin_r)


def _moba_kernel(qt_ref, k_ref, vt_ref, kmean_ref, onehot_ref, o_ref, s_ref, p_ref):
    NT = qt_ref.shape[1]
    QB = qt_ref.shape[3]
    NB = kmean_ref.shape[1]
    HD = MOBA_HEAD_DIM
    U = MOBA_UNROLL
    BLK = MOBA_BLOCK
    ROWS = MOBA_ROWS

    km = kmean_ref[0]
    km_hi = km.astype(bf16)
    km_lo = (km - km_hi.astype(f32)).astype(bf16)
    blk = lax.broadcasted_iota(jnp.int32, (NB, QB), 0)
    zeros_half = jnp.zeros((HD, QB), bf16)
    zeros_tail = jnp.zeros((LANES - NB, QB), bf16)
    in_causal = (lax.broadcasted_iota(jnp.int32, (BLK, QB), 0)
                 <= lax.broadcasted_iota(jnp.int32, (BLK, QB), 1))

    def select(qi):
        qzs = []
        for h in range(2):
            q_h = qt_ref[0, qi, h * HD:(h + 1) * HD, :]
            qz = jnp.concatenate([q_h, zeros_half] if h == 0 else [zeros_half, q_h], axis=0)
            gate = _dot(km_hi, qz) + _dot(km_lo, qz)
            gate = jnp.where(blk < qi, gate, -jnp.inf)
            sel = jnp.zeros((NB, QB), jnp.bool_)
            for _ in range(MOBA_TOPK):
                top = jnp.max(gate, axis=0, keepdims=True)
                idx = jnp.min(jnp.where(gate == top, blk, NB), axis=0, keepdims=True)
                pick = blk == idx
                sel = jnp.logical_or(sel, jnp.logical_and(pick, top > -jnp.inf))
                gate = jnp.where(pick, -jnp.inf, gate)
            bias = jnp.where(sel, 0.0, NEG_BIG).astype(bf16)
            qzs.append(jnp.concatenate([qz, bias, zeros_tail], axis=0))
        return tuple(qzs)

    def block_scores(qzs, j, u, slot, tops):
        r0 = pl.multiple_of(j * BLK, BLK)
        k_aug = jnp.concatenate(
            [k_ref[0, pl.ds(r0, BLK), :], onehot_ref[pl.ds(r0, BLK), :]], axis=1)
        out = []
        for h in range(2):
            s = _dot(k_aug, qzs[h])
            s_ref[slot, h, u * BLK:(u + 1) * BLK, :] = s
            top = jnp.max(s, axis=0, keepdims=True)
            out.append(top if tops is None else jnp.maximum(tops[h], top))
        return tuple(out)

    def own_scores(qzs, qi):
        k_own = k_ref[0, pl.ds(pl.multiple_of(qi * BLK, BLK), BLK), :]
        out = []
        for h in range(2):
            s = jnp.where(in_causal, _dot(k_own, qzs[h][:LANES]), NEG_BIG)
            out.append((s, jnp.max(s, axis=0, keepdims=True)))
        return tuple(out)

    def block_softmax(load_rows, u, m_new, part, h):
        for r in range(u * BLK, (u + 1) * BLK, ROWS):
            e = jnp.exp2(load_rows(r) - m_new)
            part = part + e
            p_ref[h, r:r + ROWS, :] = e.astype(bf16)
        return part

    def block_values(j, u, pv, h):
        d = _dot(vt_ref[0, j, h * HD:(h + 1) * HD, :], p_ref[h, u * BLK:(u + 1) * BLK, :])
        return d if pv is None else pv + d

    def group(t, tops, mls, accs, load_rows, prefetch):
        m_new = [jnp.maximum(mls[h][0], tops[h]) for h in range(2)]
        alpha = [jnp.exp2(mls[h][0] - m_new[h]) for h in range(2)]
        part = [jnp.zeros((ROWS, QB), f32)] * 2
        pv = [None, None]
        tops_next = None
        for u in range(U):
            tops_next = prefetch(u, tops_next)
            for h in range(2):
                part[h] = block_softmax(load_rows[h], u, m_new[h], part[h], h)
            for h in range(2):
                pv[h] = block_values(t * U + u, u, pv[h], h)
        mls = tuple((m_new[h], alpha[h] * mls[h][1] + jnp.sum(part[h], axis=0, keepdims=True))
                    for h in range(2))
        accs = tuple(alpha[h] * accs[h] + pv[h] for h in range(2))
        return tops_next, mls, accs

    def slot_rows(slot):
        return [lambda r, h=h: s_ref[slot, h, r:r + ROWS, :] for h in range(2)]

    ml0 = (jnp.full((1, QB), -jnp.inf, f32), jnp.zeros((1, QB), f32))
    acc0 = jnp.zeros((HD, QB), f32)

    def tile(qi, carry):
        qzs, qzs_next, own, tops0, phase = carry
        n_groups = jnp.maximum((qi + U - 1) // U, 1)

        def step(t, c, cur):
            tops, mls, accs = c
            return group(t, tops, mls, accs, slot_rows(cur),
                         lambda u, tn: block_scores(qzs, (t + 1) * U + u, u, 1 - cur, tn))

        def body(t, c):
            return lax.cond((t + phase) % 2 == 0,
                            functools.partial(step, t, cur=0), functools.partial(step, t, cur=1), c)

        state = lax.fori_loop(0, n_groups - 1, body, (tops0, (ml0, ml0), (acc0, acc0)))

        def boundary(c, cur):
            tops, mls, accs = c
            m_new = [jnp.maximum(mls[h][0], jnp.maximum(tops[h], own[h][1])) for h in range(2)]
            alpha = [jnp.exp2(mls[h][0] - m_new[h]) for h in range(2)]
            part = [jnp.zeros((ROWS, QB), f32)] * 2
            pv = [None, None]
            tops_next = None
            load_rows = slot_rows(cur)
            q_next = jnp.minimum(qi + 1, NT - 1)
            for u in range(U + 1):
                if u < U:
                    tops_next = block_scores(qzs_next, u, u, 1 - cur, tops_next)
                else:
                    own_next = own_scores(qzs_next, q_next)
                for h in range(2):
                    if u == 0:
                        for r in range(0, BLK, ROWS):
                            e = jnp.exp2(own[h][0][r:r + ROWS] - m_new[h])
                            part[h] = part[h] + e
                            p_ref[h, U * BLK + r:U * BLK + r + ROWS, :] = e.astype(bf16)
                    else:
                        part[h] = block_softmax(load_rows[h], u - 1, m_new[h], part[h], h)
                for h in range(2):
                    if u == 0:
                        pv[h] = _dot(vt_ref[0, qi, h * HD:(h + 1) * HD, :],
                                     p_ref[h, U * BLK:(U + 1) * BLK, :])
                    else:
                        pv[h] = block_values((n_groups - 1) * U + u - 1, u - 1, pv[h], h)
            qzs_after = select(jnp.minimum(qi + 2, NT - 1))
            outs = []
            for h in range(2):
                l = alpha[h] * mls[h][1] + jnp.sum(part[h], axis=0, keepdims=True)
                outs.append((alpha[h] * accs[h] + pv[h]) / l)
            o_t = jnp.concatenate(outs, axis=0)
            r0 = pl.multiple_of(qi * BLK, BLK)
            o_ref[0, pl.ds(r0, BLK), :] = o_t.T.astype(o_ref.dtype)
            return qzs_after, own_next, tops_next

        last_slot = (n_groups - 1 + phase) % 2
        qzs_after, own_next, tops_next = lax.cond(
            last_slot == 0, functools.partial(boundary, cur=0), functools.partial(boundary, cur=1),
            state)
        return qzs_next, qzs_after, own_next, tops_next, 1 - last_slot

    qzs0 = select(0)
    qzs1 = select(1)
    tops0 = None
    for u in range(U):
        tops0 = block_scores(qzs0, u, u, 0, tops0)
    lax.fori_loop(0, NT, tile, (qzs0, qzs1, own_scores(qzs0, 0), tops0, jnp.int32(0)))


def _moba_call(qt, k, vt, kmean):
    B, S, W = k.shape
    NB = S // MOBA_BLOCK
    QB = MOBA_BLOCK
    n_hp = W // LANES
    assert NB % MOBA_UNROLL == 0 and NB <= LANES
    onehot = (jnp.arange(S)[:, None] // MOBA_BLOCK == jnp.arange(LANES)[None, :]).astype(bf16)
    return pl.pallas_call(
        _moba_kernel,
        grid=(B, n_hp),
        in_specs=[
            pl.BlockSpec((1, NB, LANES, QB), lambda b, hp: (b, 0, hp, 0)),
            pl.BlockSpec((1, S, LANES), lambda b, hp: (b, 0, hp)),
            pl.BlockSpec((1, NB, LANES, MOBA_BLOCK), lambda b, hp: (b, 0, hp, 0)),
            pl.BlockSpec((1, NB, LANES), lambda b, hp: (b, 0, hp)),
            pl.BlockSpec((S, LANES), lambda b, hp: (0, 0)),
        ],
        out_specs=pl.BlockSpec((1, S, LANES), lambda b, hp: (b, 0, hp)),
        out_shape=jax.ShapeDtypeStruct((B, S, W), bf16),
        scratch_shapes=[
            pltpu.VMEM((2, 2, MOBA_UNROLL * MOBA_BLOCK, QB), f32),
            pltpu.VMEM((2, (MOBA_UNROLL + 1) * MOBA_BLOCK, QB), bf16),
        ],
        compiler_params=pltpu.CompilerParams(
            dimension_semantics=("parallel", "parallel"),
            vmem_limit_bytes=VMEM_LIMIT),
        name="moba",
    )(qt, k, vt, kmean, onehot)


def _split3(a):
    hi = a.astype(bf16)
    r1 = a - hi.astype(f32)
    mid = r1.astype(bf16)
    lo = (r1 - mid.astype(f32)).astype(bf16)
    return hi, mid, lo


def _gla_kernel(gq_ref, gk_ref, gv_ref, gr_ref, la_ref, gn_ref, y_ref, st_ref):
    C = GLA_CHUNK
    H = GLA_HEADS
    KW = GLA_KWIDTH
    DV = GLA_VAL_DIM
    NSUB = C // GLA_SUB
    T = gq_ref.shape[1]

    @pl.when(pl.program_id(1) == 0)
    def _():
        st_ref[...] = jnp.zeros_like(st_ref)

    row = lax.broadcasted_iota(jnp.int32, (C, C), 0)
    col = lax.broadcasted_iota(jnp.int32, (C, C), 1)
    tril = (col <= row).astype(bf16)
    lane_head = lax.broadcasted_iota(jnp.int32, (C, KW), 1) // GLA_KEY_DIM
    rt = lax.broadcasted_iota(jnp.int32, (C, NSUB * C), 0)
    rc = lax.broadcasted_iota(jnp.int32, (C, NSUB * C), 1)
    keep = jnp.logical_and(rc // C == rt // GLA_SUB, rc % C <= rt)
    st_lane_head = lax.broadcasted_iota(jnp.int32, (DV, KW), 1) // GLA_KEY_DIM

    n_chunks = T // C
    chunk_rows = [slice(c * C, (c + 1) * C) for c in range(n_chunks)]

    Gs = []
    for rows in chunk_rows:
        hi, mid, lo = _split3(la_ref[0, rows, :])
        Gs.append(_dot(tril, hi) + _dot(tril, mid) + _dot(tril, lo))

    rs, upds, qz_sts, decays, vs = [], [], [], [], []
    for rows, G in zip(chunk_rows, Gs):
        q = gq_ref[0, rows, :]
        k = gk_ref[0, rows, :]
        v = gv_ref[0, rows, :].astype(bf16)
        g_last = G[C - 1:C, :]
        g_ref_rows = [G[i * GLA_SUB:i * GLA_SUB + 1, :] for i in range(NSUB)]
        g_own = jnp.concatenate(
            [jnp.broadcast_to(g, (GLA_SUB, KW)) for g in g_ref_rows], axis=0)
        q_in = q * jnp.exp(G - g_own)
        k_in = jnp.concatenate(
            [(k * jnp.exp(jnp.minimum(g - G, EXP_CAP))).astype(bf16) for g in g_ref_rows],
            axis=0)
        q_st = q * jnp.exp(G)
        k_st = (k * jnp.exp(g_last - G)).astype(bf16)
        qz_in = jnp.concatenate(
            [jnp.where(lane_head == h, q_in, 0.0).astype(bf16) for h in range(H)], axis=0)
        qz_sts.append(jnp.concatenate(
            [jnp.where(lane_head == h, q_st, 0.0).astype(bf16) for h in range(H)], axis=0))
        rs.append(_dot_nt(qz_in, k_in))
        upds.append(_dot_tn(v, k_st))
        decays.append(jnp.exp(g_last))
        vs.append(v)

    o_intras = []
    for r, v in zip(rs, vs):
        per_head = []
        for h in range(H):
            r_h = jnp.where(keep, r[h * C:(h + 1) * C, :], 0.0).astype(bf16)
            v_rep = jnp.concatenate([v[:, h * DV:(h + 1) * DV]] * NSUB, axis=0)
            per_head.append(_dot(r_h, v_rep))
        o_intras.append(per_head)

    st = st_ref[...]
    o_inters = []
    for qz_st, upd, decay in zip(qz_sts, upds, decays):
        o_inters.append(_dot_nt(qz_st, st.astype(bf16)))
        st = st * decay
        for h in range(H):
            st = st + jnp.where(st_lane_head == h, upd[h * DV:(h + 1) * DV, :], 0.0)
    st_ref[...] = st

    for rows, o_intra, o_inter in zip(chunk_rows, o_intras, o_inters):
        for h in range(H):
            o = o_intra[h] + o_inter[h * C:(h + 1) * C, :]
            o = o * lax.rsqrt(jnp.mean(o * o, axis=-1, keepdims=True) + EPS)
            o = o * gn_ref[:, h * DV:(h + 1) * DV]
            gr = gr_ref[0, rows, h * DV:(h + 1) * DV]
            y = o * (gr * jax.nn.sigmoid(gr))
            y_ref[0, rows, h * DV:(h + 1) * DV] = y.astype(y_ref.dtype)


def _gla_call(gq, gk, gv, gr, la, gn):
    B, S, _ = gq.shape
    T = GLA_TILE
    tok = lambda w: pl.BlockSpec((1, T, w), lambda b, i: (b, i, 0))
    return pl.pallas_call(
        _gla_kernel,
        grid=(B, S // T),
        in_specs=[tok(GLA_KWIDTH), tok(GLA_KWIDTH), tok(GLA_VWIDTH), tok(GLA_VWIDTH),
                  tok(GLA_KWIDTH), pl.BlockSpec((1, GLA_VWIDTH), lambda b, i: (0, 0))],
        out_specs=tok(GLA_VWIDTH),
        out_shape=jax.ShapeDtypeStruct((B, S, GLA_VWIDTH), bf16),
        scratch_shapes=[pltpu.VMEM((GLA_VAL_DIM, GLA_KWIDTH), f32)],
        compiler_params=pltpu.CompilerParams(
            dimension_semantics=("parallel", "arbitrary"), vmem_limit_bytes=VMEM_LIMIT),
        name="gla",
    )(gq, gk, gv, gr, la, gn)


def _ffn_kernel(x_ref, ym_ref, yg_ref, wo_m_ref, wo_g_ref, fg_ref, wup_ref, cw_ref, cb_ref,
                wdn_ref, og_ref, out_ref, u_ref):
    T = x_ref.shape[1]
    PAD = 8

    @pl.when(pl.program_id(1) == 0)
    def _():
        u_ref[0:PAD, :] = jnp.zeros((PAD, u_ref.shape[1]), f32)

    h = x_ref[0] + _dot(ym_ref[0], wo_m_ref[...]) + _dot(yg_ref[0], wo_g_ref[...])
    hn = _rms(h, fg_ref[...]).astype(bf16)
    u_ref[PAD:PAD + T, :] = _dot(hn, wup_ref[...])
    cw = cw_ref[...]
    conv = (cw[0:1] * u_ref[PAD - 2:PAD - 2 + T, :]
            + cw[1:2] * u_ref[PAD - 1:PAD - 1 + T, :]
            + cw[2:3] * u_ref[PAD:PAD + T, :]
            + cb_ref[...])
    u_ref[0:PAD, :] = u_ref[T:T + PAD, :]
    hg = conv[:, :D_FF]
    act = (hg * jax.nn.sigmoid(hg) * conv[:, D_FF:]).astype(bf16)
    y = h + _dot(act, wdn_ref[...])
    out_ref[0] = _rms(y, og_ref[...])


def _ffn_call(x, ym, yg, wo_m, wo_g, fg, wup, cw, cb, wdn, og):
    B, S, D = x.shape
    T = FFN_TILE
    tok = lambda w: pl.BlockSpec((1, T, w), lambda b, i: (b, i, 0))
    const = lambda a: pl.BlockSpec(a.shape, lambda b, i: (0,) * a.ndim, pipeline_mode=pl.Buffered(1))
    return pl.pallas_call(
        _ffn_kernel,
        grid=(B, S // T),
        in_specs=[tok(D), tok(MOBA_WIDTH), tok(GLA_VWIDTH),
                  const(wo_m), const(wo_g), const(fg), const(wup), const(cw), const(cb),
                  const(wdn), const(og)],
        out_specs=tok(D),
        out_shape=jax.ShapeDtypeStruct((B, S, D), x.dtype),
        scratch_shapes=[pltpu.VMEM((T + 8, 2 * D_FF), f32)],
        compiler_params=pltpu.CompilerParams(
            dimension_semantics=("parallel", "arbitrary"), vmem_limit_bytes=VMEM_LIMIT),
        name="out_ffn",
    )(x, ym, yg, wo_m, wo_g, fg, wup, cw, cb, wdn, og)


def _rope_tables(S):
    hd = MOBA_HEAD_DIM
    inv_freq = 1.0 / (ROPE_THETA ** (jnp.arange(0, hd, 2, dtype=f32) / hd))
    ang = jnp.arange(S).astype(f32)[:, None] * inv_freq[None, :]
    cos, sin = jnp.cos(ang), jnp.sin(ang)
    cos_r = jnp.tile(cos, (1, LANES // (hd // 2)))
    sin_r = jnp.tile(jnp.concatenate([-sin, sin], axis=1), (1, LANES // hd))
    return cos.T, sin.T, cos_r, sin_r


def kernel(x, attn_norm_g, w_in, w_gate_up, b_gate, gla_norm_g, w_out, ffn_norm_g, w_ffn_up,
           conv_w, conv_b, w_ffn_down, final_norm_g):
    B, S, D = x.shape
    l = 0
    o_mk = MOBA_WIDTH
    o_mv = 2 * MOBA_WIDTH
    o_gq = 3 * MOBA_WIDTH
    o_gg = o_gq + 2 * GLA_KWIDTH + 2 * GLA_VWIDTH
    w = w_in[l]
    wqt = w[:, :o_mk].T.astype(bf16)
    wk = w[:, o_mk:o_mv].astype(bf16)
    wvt = w[:, o_mv:o_gq].T.astype(bf16)
    wg = jnp.pad(w[:, o_gq:], ((0, 0), (0, LANES - GLA_GATE_RANK))).astype(bf16)
    wgu = jnp.pad(w_gate_up[l], ((0, LANES - GLA_GATE_RANK), (0, 0)))
    wgu_hi = wgu.astype(bf16)
    wgu_lo = (wgu - wgu_hi.astype(f32)).astype(bf16)
    cos_t, sin_t, cos_r, sin_r = _rope_tables(S)

    qt, k, vt, kmean, gq, gk, gv, gr, la = _proj_call(
        x, attn_norm_g[l][None, :], wqt, wk, wvt, wg, wgu_hi, wgu_lo, b_gate[l][None, :],
        cos_t, sin_t, cos_r, sin_r)
    kmean = kmean.reshape(B, S // MOBA_BLOCK, MOBA_WIDTH)

    y_moba = _moba_call(qt, k, vt, kmean)
    y_gla = _gla_call(gq, gk, gv, gr, la, gla_norm_g[l].reshape(1, GLA_VWIDTH))

    wo = w_out[l].astype(bf16)
    return _ffn_call(
        x, y_moba, y_gla, wo[:MOBA_WIDTH], wo[MOBA_WIDTH:], ffn_norm_g[l][None, :],
        w_ffn_up[l].astype(bf16), conv_w[l], conv_b[l][None, :], w_ffn_down[l].astype(bf16),
        final_norm_g[None, :])
```

```python
import functools
import math

import jax
import jax.numpy as jnp
from jax import lax
from jax.experimental import pallas as pl
from jax.experimental.pallas import tpu as pltpu

D_MODEL = 1024
MOBA_HEAD_DIM = 64
MOBA_HEADS = 8
MOBA_WIDTH = 512
MOBA_BLOCK = 256
MOBA_TOPK = 3
MOBA_UNROLL = 8
MOBA_ROWS = 32
GLA_HEADS = 4
GLA_VAL_DIM = 128
GLA_KEY_DIM = 64
GLA_VWIDTH = 512
GLA_KWIDTH = 256
GLA_GATE_RANK = 16
GLA_GATE_TAU = 16.0
GLA_CHUNK = 64
GLA_SUB = 16
D_FF = 2816
CONV_WIDTH = 3
ROPE_THETA = 10000.0
EPS = 1e-6

LANES = 128
LOG2_E = 1.4426950408889634
NEG_BIG = -1e30
EXP_CAP = 80.0

PROJ_TILE = 512
GLA_TILE = 512
FFN_TILE = 512
VMEM_LIMIT = 56 * 1024 * 1024

f32 = jnp.float32
bf16 = jnp.bfloat16


def _dot(a, b):
    return jnp.dot(a, b, preferred_element_type=f32)


def _dot_nt(a, b):
    return lax.dot_general(a, b, (((1,), (1,)), ((), ())), preferred_element_type=f32)


def _dot_tn(a, b):
    return lax.dot_general(a, b, (((0,), (0,)), ((), ())), preferred_element_type=f32)


def _rms(xf, g):
    return xf * lax.rsqrt(jnp.mean(xf * xf, axis=-1, keepdims=True) + EPS) * g


def _proj_kernel(x_ref, g_ref, wqt_ref, wk_ref, wvt_ref, wg_ref, wgu_hi_ref, wgu_lo_ref, bg_ref,
                 cos_t_ref, sin_t_ref, cos_r_ref, sin_r_ref,
                 qt_ref, k_ref, vt_ref, kmean_ref, gq_ref, gk_ref, gv_ref, gr_ref, la_ref):
    T = x_ref.shape[1]
    xn = _rms(x_ref[0], g_ref[...]).astype(bf16)

    pg = _dot(xn, wg_ref[...])
    o_gk = GLA_KWIDTH
    o_gv = o_gk + GLA_KWIDTH
    o_gr = o_gv + GLA_VWIDTH
    o_gg = o_gr + GLA_VWIDTH
    gq_ref[0] = pg[:, :o_gk] * (GLA_KEY_DIM ** -0.5)
    gk_ref[0] = pg[:, o_gk:o_gv]
    gv_ref[0] = pg[:, o_gv:o_gr]
    gr_ref[0] = pg[:, o_gr:o_gg]
    gate_lr = pg[:, o_gg:]
    lr_hi = gate_lr.astype(bf16)
    lr_lo = (gate_lr - lr_hi.astype(f32)).astype(bf16)
    z = (_dot(lr_hi, wgu_hi_ref[...]) + _dot(lr_lo, wgu_hi_ref[...]) + _dot(lr_hi, wgu_lo_ref[...])
         + bg_ref[...])
    log_sig = jnp.minimum(z, 0.0) - jnp.log1p(jnp.exp(-jnp.abs(z)))
    la_ref[0] = log_sig * (1.0 / GLA_GATE_TAU)

    qt = _dot_nt(wqt_ref[...], xn)
    cos_t = cos_t_ref[...]
    sin_t = sin_t_ref[...]
    half = MOBA_HEAD_DIM // 2
    scale = MOBA_HEAD_DIM ** -0.5 * LOG2_E
    nblk = T // MOBA_BLOCK
    for h in range(MOBA_HEADS):
        r0 = h * MOBA_HEAD_DIM
        t1 = qt[r0:r0 + half]
        t2 = qt[r0 + half:r0 + MOBA_HEAD_DIM]
        lo = ((t1 * cos_t - t2 * sin_t) * scale).astype(bf16)
        hi = ((t2 * cos_t + t1 * sin_t) * scale).astype(bf16)
        for j in range(nblk):
            qt_ref[0, j, r0:r0 + half, :] = lo[:, j * MOBA_BLOCK:(j + 1) * MOBA_BLOCK]
            qt_ref[0, j, r0 + half:r0 + MOBA_HEAD_DIM, :] = hi[:, j * MOBA_BLOCK:(j + 1) * MOBA_BLOCK]

    k = _dot(xn, wk_ref[...])
    cos_r = cos_r_ref[...]
    sin_r = sin_r_ref[...]
    lane = lax.broadcasted_iota(jnp.int32, (T, LANES), 1)
    first_half = (lane % MOBA_HEAD_DIM) < half
    for p in range(MOBA_WIDTH // LANES):
        kp = k[:, p * LANES:(p + 1) * LANES]
        rot = jnp.where(first_half, pltpu.roll(kp, LANES - half, 1), pltpu.roll(kp, half, 1))
        kr = kp * cos_r + rot * sin_r
        k_ref[0, :, p * LANES:(p + 1) * LANES] = kr.astype(bf16)
        for j in range(nblk):
            kmean_ref[0, 0, j:j + 1, p * LANES:(p + 1) * LANES] = jnp.mean(
                kr[j * MOBA_BLOCK:(j + 1) * MOBA_BLOCK], axis=0, keepdims=True)

    vt = _dot_nt(wvt_ref[...], xn).astype(bf16)
    for j in range(nblk):
        vt_ref[0, j] = vt[:, j * MOBA_BLOCK:(j + 1) * MOBA_BLOCK]


def _proj_call(x, g, wqt, wk, wvt, wg, wgu_hi, wgu_lo, bg, cos_t, sin_t, cos_r, sin_r):
    B, S, D = x.shape
    T = PROJ_TILE
    nb_t = T // MOBA_BLOCK
    const = lambda shape: pl.BlockSpec(shape, lambda b, i: (0,) * len(shape))
    tok = lambda w: pl.BlockSpec((1, T, w), lambda b, i: (b, i, 0))
    out_shape = (
        jax.ShapeDtypeStruct((B, S // MOBA_BLOCK, MOBA_WIDTH, MOBA_BLOCK), bf16),
        jax.ShapeDtypeStruct((B, S, MOBA_WIDTH), bf16),
        jax.ShapeDtypeStruct((B, S // MOBA_BLOCK, MOBA_WIDTH, MOBA_BLOCK), bf16),
        jax.ShapeDtypeStruct((B, S // T, nb_t, MOBA_WIDTH), f32),
        jax.ShapeDtypeStruct((B, S, GLA_KWIDTH), f32),
        jax.ShapeDtypeStruct((B, S, GLA_KWIDTH), f32),
        jax.ShapeDtypeStruct((B, S, GLA_VWIDTH), f32),
        jax.ShapeDtypeStruct((B, S, GLA_VWIDTH), f32),
        jax.ShapeDtypeStruct((B, S, GLA_KWIDTH), f32),
    )
    out_specs = (
        pl.BlockSpec((1, nb_t, MOBA_WIDTH, MOBA_BLOCK), lambda b, i: (b, i, 0, 0)),
        tok(MOBA_WIDTH),
        pl.BlockSpec((1, nb_t, MOBA_WIDTH, MOBA_BLOCK), lambda b, i: (b, i, 0, 0)),
        pl.BlockSpec((1, 1, nb_t, MOBA_WIDTH), lambda b, i: (b, i, 0, 0)),
        tok(GLA_KWIDTH), tok(GLA_KWIDTH), tok(GLA_VWIDTH), tok(GLA_VWIDTH), tok(GLA_KWIDTH),
    )
    in_specs = [
        tok(D),
        const((1, D)),
        const(wqt.shape), const(wk.shape), const(wvt.shape), const(wg.shape),
        const(wgu_hi.shape), const(wgu_lo.shape), const((1, GLA_KWIDTH)),
        pl.BlockSpec((MOBA_HEAD_DIM // 2, T), lambda b, i: (0, i)),
        pl.BlockSpec((MOBA_HEAD_DIM // 2, T), lambda b, i: (0, i)),
        pl.BlockSpec((T, LANES), lambda b, i: (i, 0)),
        pl.BlockSpec((T, LANES), lambda b, i: (i, 0)),
    ]
    return pl.pallas_call(
        _proj_kernel,
        grid=(B, S // T),
        in_specs=in_specs,
        out_specs=out_specs,
        out_shape=out_shape,
        compiler_params=pltpu.CompilerParams(
            dimension_semantics=("parallel", "parallel"), vmem_limit_bytes=VMEM_LIMIT),
        name="in_proj",
    )(x, g, wqt, wk, wvt, wg, wgu_hi, wgu_lo, bg, cos_t, sin_t, cos_r, sin_r)


def _moba_kernel(qt_ref, k_ref, vt_ref, kmean_ref, onehot_ref, o_ref, s_ref, p_ref):
    NT = qt_ref.shape[1]
    QB = qt_ref.shape[3]
    NB = kmean_ref.shape[1]
    HD = MOBA_HEAD_DIM
    U = MOBA_UNROLL
    BLK = MOBA_BLOCK
    ROWS = MOBA_ROWS

    km = kmean_ref[0]
    km_hi = km.astype(bf16)
    km_lo = (km - km_hi.astype(f32)).astype(bf16)
    blk = lax.broadcasted_iota(jnp.int32, (NB, QB), 0)
    zeros_half = jnp.zeros((HD, QB), bf16)
    zeros_tail = jnp.zeros((LANES - NB, QB), bf16)
    in_causal = (lax.broadcasted_iota(jnp.int32, (BLK, QB), 0)
                 <= lax.broadcasted_iota(jnp.int32, (BLK, QB), 1))

    def select(qi):
        qzs = []
        for h in range(2):
            q_h = qt_ref[0, qi, h * HD:(h + 1) * HD, :]
            qz = jnp.concatenate([q_h, zeros_half] if h == 0 else [zeros_half, q_h], axis=0)
            gate = _dot(km_hi, qz) + _dot(km_lo, qz)
            gate = jnp.where(blk < qi, gate, -jnp.inf)
            sel = jnp.zeros((NB, QB), jnp.bool_)
            for _ in range(MOBA_TOPK):
                top = jnp.max(gate, axis=0, keepdims=True)
                idx = jnp.min(jnp.where(gate == top, blk, NB), axis=0, keepdims=True)
                pick = blk == idx
                sel = jnp.logical_or(sel, jnp.logical_and(pick, top > -jnp.inf))
                gate = jnp.where(pick, -jnp.inf, gate)
            bias = jnp.where(sel, 0.0, NEG_BIG).astype(bf16)
            qzs.append(jnp.concatenate([qz, bias, zeros_tail], axis=0))
        return tuple(qzs)

    def block_scores(qzs, j, u, slot, tops):
        r0 = pl.multiple_of(j * BLK, BLK)
        k_aug = jnp.concatenate(
            [k_ref[0, pl.ds(r0, BLK), :], onehot_ref[pl.ds(r0, BLK), :]], axis=1)
        out = []
        for h in range(2):
            s = _dot(k_aug, qzs[h])
            s_ref[slot, h, u * BLK:(u + 1) * BLK, :] = s
            top = jnp.max(s, axis=0, keepdims=True)
            out.append(top if tops is None else jnp.maximum(tops[h], top))
        return tuple(out)

    def own_scores(qzs, qi):
        k_own = k_ref[0, pl.ds(pl.multiple_of(qi * BLK, BLK), BLK), :]
        out = []
        for h in range(2):
            s = jnp.where(in_causal, _dot(k_own, qzs[h][:LANES]), NEG_BIG)
            out.append((s, jnp.max(s, axis=0, keepdims=True)))
        return tuple(out)

    def block_softmax(load_rows, u, m_new, part, h):
        for r in range(u * BLK, (u + 1) * BLK, ROWS):
            e = jnp.exp2(load_rows(r) - m_new)
            part = part + e
            p_ref[h, r:r + ROWS, :] = e.astype(bf16)
        return part

    def block_values(j, u, pv, h):
        d = _dot(vt_ref[0, j, h * HD:(h + 1) * HD, :], p_ref[h, u * BLK:(u + 1) * BLK, :])
        return d if pv is None else pv + d

    def group(t, tops, mls, accs, load_rows, prefetch):
        m_new = [jnp.maximum(mls[h][0], tops[h]) for h in range(2)]
        alpha = [jnp.exp2(mls[h][0] - m_new[h]) for h in range(2)]
        part = [jnp.zeros((ROWS, QB), f32)] * 2
        pv = [None, None]
        tops_next = None
        for u in range(U):
            tops_next = prefetch(u, tops_next)
            for h in range(2):
                part[h] = block_softmax(load_rows[h], u, m_new[h], part[h], h)
            for h in range(2):
                pv[h] = block_values(t * U + u, u, pv[h], h)
        mls = tuple((m_new[h], alpha[h] * mls[h][1] + jnp.sum(part[h], axis=0, keepdims=True))
                    for h in range(2))
        accs = tuple(alpha[h] * accs[h] + pv[h] for h in range(2))
        return tops_next, mls, accs

    def slot_rows(slot):
        return [lambda r, h=h: s_ref[slot, h, r:r + ROWS, :] for h in range(2)]

    ml0 = (jnp.full((1, QB), -jnp.inf, f32), jnp.zeros((1, QB), f32))
    acc0 = jnp.zeros((HD, QB), f32)

    def tile(qi, carry):
        qzs, qzs_next, own, tops0, phase = carry
        n_groups = jnp.maximum((qi + U - 1) // U, 1)

        def step(t, c, cur):
            tops, mls, accs = c
            return group(t, tops, mls, accs, slot_rows(cur),
                         lambda u, tn: block_scores(qzs, (t + 1) * U + u, u, 1 - cur, tn))

        def body(t, c):
            return lax.cond((t + phase) % 2 == 0,
                            functools.partial(step, t, cur=0), functools.partial(step, t, cur=1), c)

        state = lax.fori_loop(0, n_groups - 1, body, (tops0, (ml0, ml0), (acc0, acc0)))

        def boundary(c, cur):
            tops, mls, accs = c
            m_new = [jnp.maximum(mls[h][0], jnp.maximum(tops[h], own[h][1])) for h in range(2)]
            alpha = [jnp.exp2(mls[h][0] - m_new[h]) for h in range(2)]
            part = [jnp.zeros((ROWS, QB), f32)] * 2
            pv = [None, None]
            tops_next = None
            load_rows = slot_rows(cur)
            q_next = jnp.minimum(qi + 1, NT - 1)
            for u in range(U + 1):
                if u < U:
                    tops_next = block_scores(qzs_next, u, u, 1 - cur, tops_next)
                else:
                    own_next = own_scores(qzs_next, q_next)
                for h in range(2):
                    if u == 0:
                        for r in range(0, BLK, ROWS):
                            e = jnp.exp2(own[h][0][r:r + ROWS] - m_new[h])
                            part[h] = part[h] + e
                            p_ref[h, U * BLK + r:U * BLK + r + ROWS, :] = e.astype(bf16)
                    else:
                        part[h] = block_softmax(load_rows[h], u - 1, m_new[h], part[h], h)
                for h in range(2):
                    if u == 0:
                        pv[h] = _dot(vt_ref[0, qi, h * HD:(h + 1) * HD, :],
                                     p_ref[h, U * BLK:(U + 1) * BLK, :])
                    else:
                        pv[h] = block_values((n_groups - 1) * U + u - 1, u - 1, pv[h], h)
            qzs_after = select(jnp.minimum(qi + 2, NT - 1))
            outs = []
            for h in range(2):
                l = alpha[h] * mls[h][1] + jnp.sum(part[h], axis=0, keepdims=True)
                outs.append((alpha[h] * accs[h] + pv[h]) / l)
            o_t = jnp.concatenate(outs, axis=0)
            r0 = pl.multiple_of(qi * BLK, BLK)
            o_ref[0, pl.ds(r0, BLK), :] = o_t.T.astype(o_ref.dtype)
            return qzs_after, own_next, tops_next

        last_slot = (n_groups - 1 + phase) % 2
        qzs_after, own_next, tops_next = lax.cond(
            last_slot == 0, functools.partial(boundary, cur=0), functools.partial(boundary, cur=1),
            state)
        return qzs_next, qzs_after, own_next, tops_next, 1 - last_slot

    qzs0 = select(0)
    qzs1 = select(1)
    tops0 = None
    for u in range(U):
        tops0 = block_scores(qzs0, u, u, 0, tops0)
    lax.fori_loop(0, NT, tile, (qzs0, qzs1, own_scores(qzs0, 0), tops0, jnp.int32(0)))


def _moba_call(qt, k, vt, kmean):
    B, S, W = k.shape
    NB = S // MOBA_BLOCK
    QB = MOBA_BLOCK
    n_hp = W // LANES
    assert NB % MOBA_UNROLL == 0 and NB <= LANES
    onehot = (jnp.arange(S)[:, None] // MOBA_BLOCK == jnp.arange(LANES)[None, :]).astype(bf16)
    return pl.pallas_call(
        _moba_kernel,
        grid=(B, n_hp),
        in_specs=[
            pl.BlockSpec((1, NB, LANES, QB), lambda b, hp: (b, 0, hp, 0)),
            pl.BlockSpec((1, S, LANES), lambda b, hp: (b, 0, hp)),
            pl.BlockSpec((1, NB, LANES, MOBA_BLOCK), lambda b, hp: (b, 0, hp, 0)),
            pl.BlockSpec((1, NB, LANES), lambda b, hp: (b, 0, hp)),
            pl.BlockSpec((S, LANES), lambda b, hp: (0, 0)),
        ],
        out_specs=pl.BlockSpec((1, S, LANES), lambda b, hp: (b, 0, hp)),
        out_shape=jax.ShapeDtypeStruct((B, S, W), bf16),
        scratch_shapes=[
            pltpu.VMEM((2, 2, MOBA_UNROLL * MOBA_BLOCK, QB), f32),
            pltpu.VMEM((2, (MOBA_UNROLL + 1) * MOBA_BLOCK, QB), bf16),
        ],
        compiler_params=pltpu.CompilerParams(
            dimension_semantics=("parallel", "parallel"),
            vmem_limit_bytes=VMEM_LIMIT),
        name="moba",
    )(qt, k, vt, kmean, onehot)


def _split3(a):
    hi = a.astype(bf16)
    r1 = a - hi.astype(f32)
    mid = r1.astype(bf16)
    lo = (r1 - mid.astype(f32)).astype(bf16)
    return hi, mid, lo


def _gla_kernel(gq_ref, gk_ref, gv_ref, gr_ref, la_ref, gn_ref, y_ref, st_ref):
    C = GLA_CHUNK
    H = GLA_HEADS
    KW = GLA_KWIDTH
    DV = GLA_VAL_DIM
    NSUB = C // GLA_SUB
    T = gq_ref.shape[1]

    @pl.when(pl.program_id(1) == 0)
    def _():
        st_ref[...] = jnp.zeros_like(st_ref)

    row = lax.broadcasted_iota(jnp.int32, (C, C), 0)
    col = lax.broadcasted_iota(jnp.int32, (C, C), 1)
    tril = (col <= row).astype(bf16)
    lane_head = lax.broadcasted_iota(jnp.int32, (C, KW), 1) // GLA_KEY_DIM
    rt = lax.broadcasted_iota(jnp.int32, (C, NSUB * C), 0)
    rc = lax.broadcasted_iota(jnp.int32, (C, NSUB * C), 1)
    keep = jnp.logical_and(rc // C == rt // GLA_SUB, rc % C <= rt)
    st_lane_head = lax.broadcasted_iota(jnp.int32, (DV, KW), 1) // GLA_KEY_DIM

    n_chunks = T // C
    chunk_rows = [slice(c * C, (c + 1) * C) for c in range(n_chunks)]

    Gs = []
    for rows in chunk_rows:
        hi, mid, lo = _split3(la_ref[0, rows, :])
        Gs.append(_dot(tril, hi) + _dot(tril, mid) + _dot(tril, lo))

    rs, upds, qz_sts, decays, vs = [], [], [], [], []
    for rows, G in zip(chunk_rows, Gs):
        q = gq_ref[0, rows, :]
        k = gk_ref[0, rows, :]
        v = gv_ref[0, rows, :].astype(bf16)
        g_last = G[C - 1:C, :]
        g_ref_rows = [G[i * GLA_SUB:i * GLA_SUB + 1, :] for i in range(NSUB)]
        g_own = jnp.concatenate(
            [jnp.broadcast_to(g, (GLA_SUB, KW)) for g in g_ref_rows], axis=0)
        q_in = q * jnp.exp(G - g_own)
        k_in = jnp.concatenate(
            [(k * jnp.exp(jnp.minimum(g - G, EXP_CAP))).astype(bf16) for g in g_ref_rows],
            axis=0)
        q_st = q * jnp.exp(G)
        k_st = (k * jnp.exp(g_last - G)).astype(bf16)
        qz_in = jnp.concatenate(
            [jnp.where(lane_head == h, q_in, 0.0).astype(bf16) for h in range(H)], axis=0)
        qz_sts.append(jnp.concatenate(
            [jnp.where(lane_head == h, q_st, 0.0).astype(bf16) for h in range(H)], axis=0))
        rs.append(_dot_nt(qz_in, k_in))
        upds.append(_dot_tn(v, k_st))
        decays.append(jnp.exp(g_last))
        vs.append(v)

    o_intras = []
    for r, v in zip(rs, vs):
        per_head = []
        for h in range(H):
            r_h = jnp.where(keep, r[h * C:(h + 1) * C, :], 0.0).astype(bf16)
            v_rep = jnp.concatenate([v[:, h * DV:(h + 1) * DV]] * NSUB, axis=0)
            per_head.append(_dot(r_h, v_rep))
        o_intras.append(per_head)

    st = st_ref[...]
    o_inters = []
    for qz_st, upd, decay in zip(qz_sts, upds, decays):
        o_inters.append(_dot_nt(qz_st, st.astype(bf16)))
        st = st * decay
        for h in range(H):
            st = st + jnp.where(st_lane_head == h, upd[h * DV:(h + 1) * DV, :], 0.0)
    st_ref[...] = st

    for rows, o_intra, o_inter in zip(chunk_rows, o_intras, o_inters):
        for h in range(H):
            o = o_intra[h] + o_inter[h * C:(h + 1) * C, :]
            o = o * lax.rsqrt(jnp.mean(o * o, axis=-1, keepdims=True) + EPS)
            o = o * gn_ref[:, h * DV:(h + 1) * DV]
            gr = gr_ref[0, rows, h * DV:(h + 1) * DV]
            y = o * (gr * jax.nn.sigmoid(gr))
            y_ref[0, rows, h * DV:(h + 1) * DV] = y.astype(y_ref.dtype)


def _gla_call(gq, gk, gv, gr, la, gn):
    B, S, _ = gq.shape
    T = GLA_TILE
    tok = lambda w: pl.BlockSpec((1, T, w), lambda b, i: (b, i, 0))
    return pl.pallas_call(
        _gla_kernel,
        grid=(B, S // T),
        in_specs=[tok(GLA_KWIDTH), tok(GLA_KWIDTH), tok(GLA_VWIDTH), tok(GLA_VWIDTH),
                  tok(GLA_KWIDTH), pl.BlockSpec((1, GLA_VWIDTH), lambda b, i: (0, 0))],
        out_specs=tok(GLA_VWIDTH),
        out_shape=jax.ShapeDtypeStruct((B, S, GLA_VWIDTH), bf16),
        scratch_shapes=[pltpu.VMEM((GLA_VAL_DIM, GLA_KWIDTH), f32)],
        compiler_params=pltpu.CompilerParams(
            dimension_semantics=("parallel", "arbitrary"), vmem_limit_bytes=VMEM_LIMIT),
        name="gla",
    )(gq, gk, gv, gr, la, gn)


def _ffn_kernel(x_ref, ym_ref, yg_ref, wo_m_ref, wo_g_ref, fg_ref, wup_ref, cw_ref, cb_ref,
                wdn_ref, og_ref, out_ref, u_ref):
    T = x_ref.shape[1]
    PAD = 8

    @pl.when(pl.program_id(1) == 0)
    def _():
        u_ref[0:PAD, :] = jnp.zeros((PAD, u_ref.shape[1]), f32)

    h = x_ref[0] + _dot(ym_ref[0], wo_m_ref[...]) + _dot(yg_ref[0], wo_g_ref[...])
    hn = _rms(h, fg_ref[...]).astype(bf16)
    u_ref[PAD:PAD + T, :] = _dot(hn, wup_ref[...])
    cw = cw_ref[...]
    conv = (cw[0:1] * u_ref[PAD - 2:PAD - 2 + T, :]
            + cw[1:2] * u_ref[PAD - 1:PAD - 1 + T, :]
            + cw[2:3] * u_ref[PAD:PAD + T, :]
            + cb_ref[...])
    u_ref[0:PAD, :] = u_ref[T:T + PAD, :]
    hg = conv[:, :D_FF]
    act = (hg * jax.nn.sigmoid(hg) * conv[:, D_FF:]).astype(bf16)
    y = h + _dot(act, wdn_ref[...])
    out_ref[0] = _rms(y, og_ref[...])


def _ffn_call(x, ym, yg, wo_m, wo_g, fg, wup, cw, cb, wdn, og):
    B, S, D = x.shape
    T = FFN_TILE
    tok = lambda w: pl.BlockSpec((1, T, w), lambda b, i: (b, i, 0))
    const = lambda a: pl.BlockSpec(a.shape, lambda b, i: (0,) * a.ndim, pipeline_mode=pl.Buffered(1))
    return pl.pallas_call(
        _ffn_kernel,
        grid=(B, S // T),
        in_specs=[tok(D), tok(MOBA_WIDTH), tok(GLA_VWIDTH),
                  const(wo_m), const(wo_g), const(fg), const(wup), const(cw), const(cb),
                  const(wdn), const(og)],
        out_specs=tok(D),
        out_shape=jax.ShapeDtypeStruct((B, S, D), x.dtype),
        scratch_shapes=[pltpu.VMEM((T + 8, 2 * D_FF), f32)],
        compiler_params=pltpu.CompilerParams(
            dimension_semantics=("parallel", "arbitrary"), vmem_limit_bytes=VMEM_LIMIT),
        name="out_ffn",
    )(x, ym, yg, wo_m, wo_g, fg, wup, cw, cb, wdn, og)


def _rope_tables(S):
    hd = MOBA_HEAD_DIM
    inv_freq = 1.0 / (ROPE_THETA ** (jnp.arange(0, hd, 2, dtype=f32) / hd))
    ang = jnp.arange(S).astype(f32)[:, None] * inv_freq[None, :]
    cos, sin = jnp.cos(ang), jnp.sin(ang)
    cos_r = jnp.tile(cos, (1, LANES // (hd // 2)))
    sin_r = jnp.tile(jnp.concatenate([-sin, sin], axis=1), (1, LANES // hd))
    return cos.T, sin.T, cos_r, sin_r


def kernel(x, attn_norm_g, w_in, w_gate_up, b_gate, gla_norm_g, w_out, ffn_norm_g, w_ffn_up,
           conv_w, conv_b, w_ffn_down, final_norm_g):
    B, S, D = x.shape
    l = 0
    o_mk = MOBA_WIDTH
    o_mv = 2 * MOBA_WIDTH
    o_gq = 3 * MOBA_WIDTH
    o_gg = o_gq + 2 * GLA_KWIDTH + 2 * GLA_VWIDTH
    w = w_in[l]
    wqt = w[:, :o_mk].T.astype(bf16)
    wk = w[:, o_mk:o_mv].astype(bf16)
    wvt = w[:, o_mv:o_gq].T.astype(bf16)
    wg = jnp.pad(w[:, o_gq:], ((0, 0), (0, LANES - GLA_GATE_RANK))).astype(bf16)
    wgu = jnp.pad(w_gate_up[l], ((0, LANES - GLA_GATE_RANK), (0, 0)))
    wgu_hi = wgu.astype(bf16)
    wgu_lo = (wgu - wgu_hi.astype(f32)).astype(bf16)
    cos_t, sin_t, cos_r, sin_r = _rope_tables(S)

    qt, k, vt, kmean, gq, gk, gv, gr, la = _proj_call(
        x, attn_norm_g[l][None, :], wqt, wk, wvt, wg, wgu_hi, wgu_lo, b_gate[l][None, :],
        cos_t, sin_t, cos_r, sin_r)
    kmean = kmean.reshape(B, S // MOBA_BLOCK, MOBA_WIDTH)

    y_moba = _moba_call(qt, k, vt, kmean)
    y_gla = _gla_call(gq, gk, gv, gr, la, gla_norm_g[l].reshape(1, GLA_VWIDTH))

    wo = w_out[l].astype(bf16)
    return _ffn_call(
        x, y_moba, y_gla, wo[:MOBA_WIDTH], wo[MOBA_WIDTH:], ffn_norm_g[l][None, :],
        w_ffn_up[l].astype(bf16), conv_w[l], conv_b[l][None, :], w_ffn_down[l].astype(bf16),
        final_norm_g[None, :])
```

```python
import functools
import math

import jax
import jax.numpy as jnp
from jax import lax
from jax.experimental import pallas as pl
from jax.experimental.pallas import tpu as pltpu

D_MODEL = 1024
MOBA_HEAD_DIM = 64
MOBA_HEADS = 8
MOBA_WIDTH = 512
MOBA_BLOCK = 256
MOBA_TOPK = 3
MOBA_UNROLL = 8
MOBA_ROWS = 32
GLA_HEADS = 4
GLA_VAL_DIM = 128
GLA_KEY_DIM = 64
GLA_VWIDTH = 512
GLA_KWIDTH = 256
GLA_GATE_RANK = 16
GLA_GATE_TAU = 16.0
GLA_CHUNK = 64
GLA_SUB = 16
D_FF = 2816
CONV_WIDTH = 3
ROPE_THETA = 10000.0
EPS = 1e-6

LANES = 128
LOG2_E = 1.4426950408889634
NEG_BIG = -1e30
EXP_CAP = 80.0

PROJ_TILE = 512
GLA_TILE = 512
FFN_TILE = 512
VMEM_LIMIT = 56 * 1024 * 1024

f32 = jnp.float32
bf16 = jnp.bfloat16


def _dot(a, b):
    return jnp.dot(a, b, preferred_element_type=f32)


def _dot_nt(a, b):
    return lax.dot_general(a, b, (((1,), (1,)), ((), ())), preferred_element_type=f32)


def _dot_tn(a, b):
    return lax.dot_general(a, b, (((0,), (0,)), ((), ())), preferred_element_type=f32)


def _rms(xf, g):
    return xf * lax.rsqrt(jnp.mean(xf * xf, axis=-1, keepdims=True) + EPS) * g


def _proj_kernel(x_ref, g_ref, wqt_ref, wk_ref, wvt_ref, wg_ref, wgu_hi_ref, wgu_lo_ref, bg_ref,
                 cos_t_ref, sin_t_ref, cos_r_ref, sin_r_ref,
                 qt_ref, k_ref, vt_ref, kmean_ref, gq_ref, gk_ref, gv_ref, gr_ref, la_ref):
    T = x_ref.shape[1]
    xn = _rms(x_ref[0], g_ref[...]).astype(bf16)

    nblk = T // MOBA_BLOCK
    vg = _dot_nt(wvt_ref[...], xn)
    vt = vg[:MOBA_WIDTH].astype(bf16)
    for j in range(nblk):
        vt_ref[0, j] = vt[:, j * MOBA_BLOCK:(j + 1) * MOBA_BLOCK]

    gate_lr = vg[MOBA_WIDTH:]
    lr_hi = gate_lr.astype(bf16)
    lr_lo = (gate_lr - lr_hi.astype(f32)).astype(bf16)
    z = (_dot_tn(lr_hi, wgu_hi_ref[...]) + _dot_tn(lr_lo, wgu_hi_ref[...])
         + _dot_tn(lr_hi, wgu_lo_ref[...]) + bg_ref[...])
    log_sig = jnp.minimum(z, 0.0) - jnp.log1p(jnp.exp(-jnp.abs(z)))
    la_ref[0] = log_sig * (1.0 / GLA_GATE_TAU)

    qt = _dot_nt(wqt_ref[...], xn)
    cos_t = cos_t_ref[...]
    sin_t = sin_t_ref[...]
    half = MOBA_HEAD_DIM // 2
    scale = MOBA_HEAD_DIM ** -0.5 * LOG2_E
    for h in range(MOBA_HEADS):
        r0 = h * MOBA_HEAD_DIM
        t1 = qt[r0:r0 + half]
        t2 = qt[r0 + half:r0 + MOBA_HEAD_DIM]
        lo = ((t1 * cos_t - t2 * sin_t) * scale).astype(bf16)
        hi = ((t2 * cos_t + t1 * sin_t) * scale).astype(bf16)
        for j in range(nblk):
            qt_ref[0, j, r0:r0 + half, :] = lo[:, j * MOBA_BLOCK:(j + 1) * MOBA_BLOCK]
            qt_ref[0, j, r0 + half:r0 + MOBA_HEAD_DIM, :] = hi[:, j * MOBA_BLOCK:(j + 1) * MOBA_BLOCK]

    k = _dot(xn, wk_ref[...])
    cos_r = cos_r_ref[...]
    sin_r = sin_r_ref[...]
    lane = lax.broadcasted_iota(jnp.int32, (T, LANES), 1)
    first_half = (lane % MOBA_HEAD_DIM) < half
    for p in range(MOBA_WIDTH // LANES):
        kp = k[:, p * LANES:(p + 1) * LANES]
        rot = jnp.where(first_half, pltpu.roll(kp, LANES - half, 1), pltpu.roll(kp, half, 1))
        kr = kp * cos_r + rot * sin_r
        k_ref[0, :, p * LANES:(p + 1) * LANES] = kr.astype(bf16)
        for j in range(nblk):
            kmean_ref[0, 0, j:j + 1, p * LANES:(p + 1) * LANES] = jnp.mean(
                kr[j * MOBA_BLOCK:(j + 1) * MOBA_BLOCK], axis=0, keepdims=True)

    pg = _dot(xn, wg_ref[...])
    o_gk = GLA_KWIDTH
    o_gv = o_gk + GLA_KWIDTH
    o_gr = o_gv + GLA_VWIDTH
    gq_ref[0] = pg[:, :o_gk] * (GLA_KEY_DIM ** -0.5)
    gk_ref[0] = pg[:, o_gk:o_gv]
    gv_ref[0] = pg[:, o_gv:o_gr]
    gr_ref[0] = pg[:, o_gr:]


def _proj_call(x, g, wqt, wk, wvt, wg, wgu_hi, wgu_lo, bg, cos_t, sin_t, cos_r, sin_r):
    B, S, D = x.shape
    T = PROJ_TILE
    nb_t = T // MOBA_BLOCK
    const = lambda shape: pl.BlockSpec(shape, lambda b, i: (0,) * len(shape))
    tok = lambda w: pl.BlockSpec((1, T, w), lambda b, i: (b, i, 0))
    out_shape = (
        jax.ShapeDtypeStruct((B, S // MOBA_BLOCK, MOBA_WIDTH, MOBA_BLOCK), bf16),
        jax.ShapeDtypeStruct((B, S, MOBA_WIDTH), bf16),
        jax.ShapeDtypeStruct((B, S // MOBA_BLOCK, MOBA_WIDTH, MOBA_BLOCK), bf16),
        jax.ShapeDtypeStruct((B, S // T, nb_t, MOBA_WIDTH), f32),
        jax.ShapeDtypeStruct((B, S, GLA_KWIDTH), f32),
        jax.ShapeDtypeStruct((B, S, GLA_KWIDTH), f32),
        jax.ShapeDtypeStruct((B, S, GLA_VWIDTH), f32),
        jax.ShapeDtypeStruct((B, S, GLA_VWIDTH), f32),
        jax.ShapeDtypeStruct((B, S, GLA_KWIDTH), f32),
    )
    out_specs = (
        pl.BlockSpec((1, nb_t, MOBA_WIDTH, MOBA_BLOCK), lambda b, i: (b, i, 0, 0)),
        tok(MOBA_WIDTH),
        pl.BlockSpec((1, nb_t, MOBA_WIDTH, MOBA_BLOCK), lambda b, i: (b, i, 0, 0)),
        pl.BlockSpec((1, 1, nb_t, MOBA_WIDTH), lambda b, i: (b, i, 0, 0)),
        tok(GLA_KWIDTH), tok(GLA_KWIDTH), tok(GLA_VWIDTH), tok(GLA_VWIDTH), tok(GLA_KWIDTH),
    )
    in_specs = [
        tok(D),
        const((1, D)),
        const(wqt.shape), const(wk.shape), const(wvt.shape), const(wg.shape),
        const(wgu_hi.shape), const(wgu_lo.shape), const((1, GLA_KWIDTH)),
        pl.BlockSpec((MOBA_HEAD_DIM // 2, T), lambda b, i: (0, i)),
        pl.BlockSpec((MOBA_HEAD_DIM // 2, T), lambda b, i: (0, i)),
        pl.BlockSpec((T, LANES), lambda b, i: (i, 0)),
        pl.BlockSpec((T, LANES), lambda b, i: (i, 0)),
    ]
    return pl.pallas_call(
        _proj_kernel,
        grid=(B, S // T),
        in_specs=in_specs,
        out_specs=out_specs,
        out_shape=out_shape,
        compiler_params=pltpu.CompilerParams(
            dimension_semantics=("parallel", "parallel"), vmem_limit_bytes=VMEM_LIMIT),
        name="in_proj",
    )(x, g, wqt, wk, wvt, wg, wgu_hi, wgu_lo, bg, cos_t, sin_t, cos_r, sin_r)


def _moba_kernel(qt_ref, k_ref, vt_ref, kmean_ref, onehot_ref, o_ref, s_ref, p_ref):
    NT = qt_ref.shape[1]
    QB = qt_ref.shape[3]
    NB = kmean_ref.shape[1]
    HD = MOBA_HEAD_DIM
    U = MOBA_UNROLL
    BLK = MOBA_BLOCK
    ROWS = MOBA_ROWS

    km = kmean_ref[0]
    km_hi = km.astype(bf16)
    km_lo = (km - km_hi.astype(f32)).astype(bf16)
    blk = lax.broadcasted_iota(jnp.int32, (NB, QB), 0)
    zeros_half = jnp.zeros((HD, QB), bf16)
    zeros_tail = jnp.zeros((LANES - NB, QB), bf16)
    in_causal = (lax.broadcasted_iota(jnp.int32, (BLK, QB), 0)
                 <= lax.broadcasted_iota(jnp.int32, (BLK, QB), 1))

    def select(qi):
        qzs = []
        for h in range(2):
            q_h = qt_ref[0, qi, h * HD:(h + 1) * HD, :]
            qz = jnp.concatenate([q_h, zeros_half] if h == 0 else [zeros_half, q_h], axis=0)
            gate = _dot(km_hi, qz) + _dot(km_lo, qz)
            gate = jnp.where(blk < qi, gate, -jnp.inf)
            sel = jnp.zeros((NB, QB), jnp.bool_)
            for _ in range(MOBA_TOPK):
                top = jnp.max(gate, axis=0, keepdims=True)
                idx = jnp.min(jnp.where(gate == top, blk, NB), axis=0, keepdims=True)
                pick = blk == idx
                sel = jnp.logical_or(sel, jnp.logical_and(pick, top > -jnp.inf))
                gate = jnp.where(pick, -jnp.inf, gate)
            bias = jnp.where(sel, 0.0, NEG_BIG).astype(bf16)
            qzs.append(jnp.concatenate([qz, bias, zeros_tail], axis=0))
        return tuple(qzs)

    def block_scores(qzs, j, u, slot, tops):
        r0 = pl.multiple_of(j * BLK, BLK)
        k_aug = jnp.concatenate(
            [k_ref[0, pl.ds(r0, BLK), :], onehot_ref[pl.ds(r0, BLK), :]], axis=1)
        out = []
        for h in range(2):
            s = _dot(k_aug, qzs[h])
            s_ref[slot, h, u * BLK:(u + 1) * BLK, :] = s
            top = jnp.max(s, axis=0, keepdims=True)
            out.append(top if tops is None else jnp.maximum(tops[h], top))
        return tuple(out)

    def own_scores(qzs, qi):
        k_own = k_ref[0, pl.ds(pl.multiple_of(qi * BLK, BLK), BLK), :]
        out = []
        for h in range(2):
            s = jnp.where(in_causal, _dot(k_own, qzs[h][:LANES]), NEG_BIG)
            out.append((s, jnp.max(s, axis=0, keepdims=True)))
        return tuple(out)

    def block_softmax(load_rows, u, m_new, part, h):
        for r in range(u * BLK, (u + 1) * BLK, ROWS):
            e = jnp.exp2(load_rows(r) - m_new)
            part = part + e
            p_ref[h, r:r + ROWS, :] = e.astype(bf16)
        return part

    def block_values(j, u, pv, h):
        d = _dot(vt_ref[0, j, h * HD:(h + 1) * HD, :], p_ref[h, u * BLK:(u + 1) * BLK, :])
        return d if pv is None else pv + d

    def group(t, tops, mls, accs, load_rows, prefetch):
        m_new = [jnp.maximum(mls[h][0], tops[h]) for h in range(2)]
        alpha = [jnp.exp2(mls[h][0] - m_new[h]) for h in range(2)]
        part = [jnp.zeros((ROWS, QB), f32)] * 2
        pv = [None, None]
        tops_next = None
        for u in range(U):
            tops_next = prefetch(u, tops_next)
            for h in range(2):
                part[h] = block_softmax(load_rows[h], u, m_new[h], part[h], h)
            for h in range(2):
                pv[h] = block_values(t * U + u, u, pv[h], h)
        mls = tuple((m_new[h], alpha[h] * mls[h][1] + jnp.sum(part[h], axis=0, keepdims=True))
                    for h in range(2))
        accs = tuple(alpha[h] * accs[h] + pv[h] for h in range(2))
        return tops_next, mls, accs

    def slot_rows(slot):
        return [lambda r, h=h: s_ref[slot, h, r:r + ROWS, :] for h in range(2)]

    ml0 = (jnp.full((1, QB), -jnp.inf, f32), jnp.zeros((1, QB), f32))
    acc0 = jnp.zeros((HD, QB), f32)

    def tile(qi, carry):
        qzs, qzs_next, own, tops0, phase = carry
        n_groups = jnp.maximum((qi + U - 1) // U, 1)

        def step(t, c, cur):
            tops, mls, accs = c
            return group(t, tops, mls, accs, slot_rows(cur),
                         lambda u, tn: block_scores(qzs, (t + 1) * U + u, u, 1 - cur, tn))

        def body(t, c):
            return lax.cond((t + phase) % 2 == 0,
                            functools.partial(step, t, cur=0), functools.partial(step, t, cur=1), c)

        state = lax.fori_loop(0, n_groups - 1, body, (tops0, (ml0, ml0), (acc0, acc0)))

        def boundary(c, cur):
            tops, mls, accs = c
            m_new = [jnp.maximum(mls[h][0], jnp.maximum(tops[h], own[h][1])) for h in range(2)]
            alpha = [jnp.exp2(mls[h][0] - m_new[h]) for h in range(2)]
            part = [jnp.zeros((ROWS, QB), f32)] * 2
            pv = [None, None]
            tops_next = None
            load_rows = slot_rows(cur)
            q_next = jnp.minimum(qi + 1, NT - 1)
            for u in range(U + 1):
                if u < U:
                    tops_next = block_scores(qzs_next, u, u, 1 - cur, tops_next)
                else:
                    own_next = own_scores(qzs_next, q_next)
                for h in range(2):
                    if u == 0:
                        for r in range(0, BLK, ROWS):
                            e = jnp.exp2(own[h][0][r:r + ROWS] - m_new[h])
                            part[h] = part[h] + e
                            p_ref[h, U * BLK + r:U * BLK + r + ROWS, :] = e.astype(bf16)
                    else:
                        part[h] = block_softmax(load_rows[h], u - 1, m_new[h], part[h], h)
                for h in range(2):
                    if u == 0:
                        pv[h] = _dot(vt_ref[0, qi, h * HD:(h + 1) * HD, :],
                                     p_ref[h, U * BLK:(U + 1) * BLK, :])
                    else:
                        pv[h] = block_values((n_groups - 1) * U + u - 1, u - 1, pv[h], h)
            qzs_after = select(jnp.minimum(qi + 2, NT - 1))
            outs = []
            for h in range(2):
                l = alpha[h] * mls[h][1] + jnp.sum(part[h], axis=0, keepdims=True)
                outs.append((alpha[h] * accs[h] + pv[h]) / l)
            o_t = jnp.concatenate(outs, axis=0)
            r0 = pl.multiple_of(qi * BLK, BLK)
            o_ref[0, pl.ds(r0, BLK), :] = o_t.T.astype(o_ref.dtype)
            return qzs_after, own_next, tops_next

        last_slot = (n_groups - 1 + phase) % 2
        qzs_after, own_next, tops_next = lax.cond(
            last_slot == 0, functools.partial(boundary, cur=0), functools.partial(boundary, cur=1),
            state)
        return qzs_next, qzs_after, own_next, tops_next, 1 - last_slot

    qzs0 = select(0)
    qzs1 = select(1)
    tops0 = None
    for u in range(U):
        tops0 = block_scores(qzs0, u, u, 0, tops0)
    lax.fori_loop(0, NT, tile, (qzs0, qzs1, own_scores(qzs0, 0), tops0, jnp.int32(0)))


def _moba_call(qt, k, vt, kmean):
    B, S, W = k.shape
    NB = S // MOBA_BLOCK
    QB = MOBA_BLOCK
    n_hp = W // LANES
    assert NB % MOBA_UNROLL == 0 and NB <= LANES
    onehot = (jnp.arange(S)[:, None] // MOBA_BLOCK == jnp.arange(LANES)[None, :]).astype(bf16)
    return pl.pallas_call(
        _moba_kernel,
        grid=(B, n_hp),
        in_specs=[
            pl.BlockSpec((1, NB, LANES, QB), lambda b, hp: (b, 0, hp, 0)),
            pl.BlockSpec((1, S, LANES), lambda b, hp: (b, 0, hp)),
            pl.BlockSpec((1, NB, LANES, MOBA_BLOCK), lambda b, hp: (b, 0, hp, 0)),
            pl.BlockSpec((1, NB, LANES), lambda b, hp: (b, 0, hp)),
            pl.BlockSpec((S, LANES), lambda b, hp: (0, 0)),
        ],
        out_specs=pl.BlockSpec((1, S, LANES), lambda b, hp: (b, 0, hp)),
        out_shape=jax.ShapeDtypeStruct((B, S, W), bf16),
        scratch_shapes=[
            pltpu.VMEM((2, 2, MOBA_UNROLL * MOBA_BLOCK, QB), f32),
            pltpu.VMEM((2, (MOBA_UNROLL + 1) * MOBA_BLOCK, QB), bf16),
        ],
        compiler_params=pltpu.CompilerParams(
            dimension_semantics=("parallel", "parallel"),
            vmem_limit_bytes=VMEM_LIMIT),
        name="moba",
    )(qt, k, vt, kmean, onehot)


def _split3(a):
    hi = a.astype(bf16)
    r1 = a - hi.astype(f32)
    mid = r1.astype(bf16)
    lo = (r1 - mid.astype(f32)).astype(bf16)
    return hi, mid, lo


def _gla_kernel(gq_ref, gk_ref, gv_ref, gr_ref, la_ref, gn_ref, y_ref, st_ref):
    C = GLA_CHUNK
    H = GLA_HEADS
    KW = GLA_KWIDTH
    DV = GLA_VAL_DIM
    NSUB = C // GLA_SUB
    T = gq_ref.shape[1]

    @pl.when(pl.program_id(1) == 0)
    def _():
        st_ref[...] = jnp.zeros_like(st_ref)

    row = lax.broadcasted_iota(jnp.int32, (C, C), 0)
    col = lax.broadcasted_iota(jnp.int32, (C, C), 1)
    tril = (col <= row).astype(bf16)
    lane_head = lax.broadcasted_iota(jnp.int32, (C, KW), 1) // GLA_KEY_DIM
    rt = lax.broadcasted_iota(jnp.int32, (C, NSUB * C), 0)
    rc = lax.broadcasted_iota(jnp.int32, (C, NSUB * C), 1)
    keep = jnp.logical_and(rc // C == rt // GLA_SUB, rc % C <= rt)
    st_lane_head = lax.broadcasted_iota(jnp.int32, (DV, KW), 1) // GLA_KEY_DIM

    n_chunks = T // C
    chunk_rows = [slice(c * C, (c + 1) * C) for c in range(n_chunks)]

    Gs = []
    for rows in chunk_rows:
        hi, mid, lo = _split3(la_ref[0, rows, :])
        Gs.append(_dot(tril, hi) + _dot(tril, mid) + _dot(tril, lo))

    rs, upds, qz_sts, decays, vs = [], [], [], [], []
    for rows, G in zip(chunk_rows, Gs):
        q = gq_ref[0, rows, :]
        k = gk_ref[0, rows, :]
        v = gv_ref[0, rows, :].astype(bf16)
        g_last = G[C - 1:C, :]
        g_ref_rows = [G[i * GLA_SUB:i * GLA_SUB + 1, :] for i in range(NSUB)]
        g_own = jnp.concatenate(
            [jnp.broadcast_to(g, (GLA_SUB, KW)) for g in g_ref_rows], axis=0)
        q_in = q * jnp.exp(G - g_own)
        k_in = jnp.concatenate(
            [(k * jnp.exp(jnp.minimum(g - G, EXP_CAP))).astype(bf16) for g in g_ref_rows],
            axis=0)
        q_st = q * jnp.exp(G)
        k_st = (k * jnp.exp(g_last - G)).astype(bf16)
        qz_in = jnp.concatenate(
            [jnp.where(lane_head == h, q_in, 0.0).astype(bf16) for h in range(H)], axis=0)
        qz_sts.append(jnp.concatenate(
            [jnp.where(lane_head == h, q_st, 0.0).astype(bf16) for h in range(H)], axis=0))
        rs.append(_dot_nt(qz_in, k_in))
        upds.append(_dot_tn(v, k_st))
        decays.append(jnp.exp(g_last))
        vs.append(v)

    o_intras = []
    for r, v in zip(rs, vs):
        per_head = []
        for h in range(H):
            r_h = jnp.where(keep, r[h * C:(h + 1) * C, :], 0.0).astype(bf16)
            v_rep = jnp.concatenate([v[:, h * DV:(h + 1) * DV]] * NSUB, axis=0)
            per_head.append(_dot(r_h, v_rep))
        o_intras.append(per_head)

    st = st_ref[...]
    o_inters = []
    for qz_st, upd, decay in zip(qz_sts, upds, decays):
        o_inters.append(_dot_nt(qz_st, st.astype(bf16)))
        st = st * decay
        for h in range(H):
            st = st + jnp.where(st_lane_head == h, upd[h * DV:(h + 1) * DV, :], 0.0)
    st_ref[...] = st

    for rows, o_intra, o_inter in zip(chunk_rows, o_intras, o_inters):
        for h in range(H):
            o = o_intra[h] + o_inter[h * C:(h + 1) * C, :]
            o = o * lax.rsqrt(jnp.mean(o * o, axis=-1, keepdims=True) + EPS)
            o = o * gn_ref[:, h * DV:(h + 1) * DV]
            gr = gr_ref[0, rows, h * DV:(h + 1) * DV]
            y = o * (gr * jax.nn.sigmoid(gr))
            y_ref[0, rows, h * DV:(h + 1) * DV] = y.astype(y_ref.dtype)


def _gla_call(gq, gk, gv, gr, la, gn):
    B, S, _ = gq.shape
    T = GLA_TILE
    tok = lambda w: pl.BlockSpec((1, T, w), lambda b, i: (b, i, 0))
    return pl.pallas_call(
        _gla_kernel,
        grid=(B, S // T),
        in_specs=[tok(GLA_KWIDTH), tok(GLA_KWIDTH), tok(GLA_VWIDTH), tok(GLA_VWIDTH),
                  tok(GLA_KWIDTH), pl.BlockSpec((1, GLA_VWIDTH), lambda b, i: (0, 0))],
        out_specs=tok(GLA_VWIDTH),
        out_shape=jax.ShapeDtypeStruct((B, S, GLA_VWIDTH), bf16),
        scratch_shapes=[pltpu.VMEM((GLA_VAL_DIM, GLA_KWIDTH), f32)],
        compiler_params=pltpu.CompilerParams(
            dimension_semantics=("parallel", "arbitrary"), vmem_limit_bytes=VMEM_LIMIT),
        name="gla",
    )(gq, gk, gv, gr, la, gn)


def _ffn_kernel(x_ref, ym_ref, yg_ref, wo_m_ref, wo_g_ref, fg_ref, wup_ref, cw_ref, cb_ref,
                wdn_ref, og_ref, out_ref, u_ref):
    T = x_ref.shape[1]
    PAD = 8

    @pl.when(pl.program_id(1) == 0)
    def _():
        u_ref[0:PAD, :] = jnp.zeros((PAD, u_ref.shape[1]), f32)

    h = x_ref[0] + _dot(ym_ref[0], wo_m_ref[...]) + _dot(yg_ref[0], wo_g_ref[...])
    hn = _rms(h, fg_ref[...]).astype(bf16)
    u_ref[PAD:PAD + T, :] = _dot(hn, wup_ref[...])
    cw = cw_ref[...]
    conv = (cw[0:1] * u_ref[PAD - 2:PAD - 2 + T, :]
            + cw[1:2] * u_ref[PAD - 1:PAD - 1 + T, :]
            + cw[2:3] * u_ref[PAD:PAD + T, :]
            + cb_ref[...])
    u_ref[0:PAD, :] = u_ref[T:T + PAD, :]
    hg = conv[:, :D_FF]
    act = (hg * jax.nn.sigmoid(hg) * conv[:, D_FF:]).astype(bf16)
    y = h + _dot(act, wdn_ref[...])
    out_ref[0] = _rms(y, og_ref[...])


def _ffn_call(x, ym, yg, wo_m, wo_g, fg, wup, cw, cb, wdn, og):
    B, S, D = x.shape
    T = FFN_TILE
    tok = lambda w: pl.BlockSpec((1, T, w), lambda b, i: (b, i, 0))
    const = lambda a: pl.BlockSpec(a.shape, lambda b, i: (0,) * a.ndim, pipeline_mode=pl.Buffered(1))
    return pl.pallas_call(
        _ffn_kernel,
        grid=(B, S // T),
        in_specs=[tok(D), tok(MOBA_WIDTH), tok(GLA_VWIDTH),
                  const(wo_m), const(wo_g), const(fg), const(wup), const(cw), const(cb),
                  const(wdn), const(og)],
        out_specs=tok(D),
        out_shape=jax.ShapeDtypeStruct((B, S, D), x.dtype),
        scratch_shapes=[pltpu.VMEM((T + 8, 2 * D_FF), f32)],
        compiler_params=pltpu.CompilerParams(
            dimension_semantics=("parallel", "arbitrary"), vmem_limit_bytes=VMEM_LIMIT),
        name="out_ffn",
    )(x, ym, yg, wo_m, wo_g, fg, wup, cw, cb, wdn, og)


def _rope_tables(S):
    hd = MOBA_HEAD_DIM
    inv_freq = 1.0 / (ROPE_THETA ** (jnp.arange(0, hd, 2, dtype=f32) / hd))
    pos = jnp.arange(S).astype(f32)
    ang_t = inv_freq[:, None] * pos[None, :]
    lane = jnp.arange(LANES)
    ang_r = pos[:, None] * inv_freq[lane % (hd // 2)][None, :]
    sign = jnp.where(lane % hd < hd // 2, -1.0, 1.0).astype(f32)
    return jnp.cos(ang_t), jnp.sin(ang_t), jnp.cos(ang_r), jnp.sin(ang_r) * sign[None, :]


def kernel(x, attn_norm_g, w_in, w_gate_up, b_gate, gla_norm_g, w_out, ffn_norm_g, w_ffn_up,
           conv_w, conv_b, w_ffn_down, final_norm_g):
    B, S, D = x.shape
    l = 0
    o_mk = MOBA_WIDTH
    o_mv = 2 * MOBA_WIDTH
    o_gq = 3 * MOBA_WIDTH
    o_gg = o_gq + 2 * GLA_KWIDTH + 2 * GLA_VWIDTH
    w = w_in[l]
    wqt = w[:, :o_mk].T.astype(bf16)
    wk = w[:, o_mk:o_mv].astype(bf16)
    wvt = jnp.concatenate([w[:, o_mv:o_gq], w[:, o_gg:]], axis=1).T.astype(bf16)
    wg = w[:, o_gq:o_gg].astype(bf16)
    wgu = w_gate_up[l]
    wgu_hi = wgu.astype(bf16)
    wgu_lo = (wgu - wgu_hi.astype(f32)).astype(bf16)
    cos_t, sin_t, cos_r, sin_r = _rope_tables(S)

    qt, k, vt, kmean, gq, gk, gv, gr, la = _proj_call(
        x, attn_norm_g[l][None, :], wqt, wk, wvt, wg, wgu_hi, wgu_lo, b_gate[l][None, :],
        cos_t, sin_t, cos_r, sin_r)
    kmean = kmean.reshape(B, S // MOBA_BLOCK, MOBA_WIDTH)

    y_moba = _moba_call(qt, k, vt, kmean)
    y_gla = _gla_call(gq, gk, gv, gr, la, gla_norm_g[l].reshape(1, GLA_VWIDTH))

    wo = w_out[l].astype(bf16)
    return _ffn_call(
        x, y_moba, y_gla, wo[:MOBA_WIDTH], wo[MOBA_WIDTH:], ffn_norm_g[l][None, :],
        w_ffn_up[l].astype(bf16), conv_w[l], conv_b[l][None, :], w_ffn_down[l].astype(bf16),
        final_norm_g[None, :])
```

```python
import functools
import math

import jax
import jax.numpy as jnp
from jax import lax
from jax.experimental import pallas as pl
from jax.experimental.pallas import tpu as pltpu

D_MODEL = 1024
MOBA_HEAD_DIM = 64
MOBA_HEADS = 8
MOBA_WIDTH = 512
MOBA_BLOCK = 256
MOBA_TOPK = 3
MOBA_UNROLL = 8
MOBA_ROWS = 32
MOBA_QTILE = 512
GLA_HEADS = 4
GLA_VAL_DIM = 128
GLA_KEY_DIM = 64
GLA_VWIDTH = 512
GLA_KWIDTH = 256
GLA_GATE_RANK = 16
GLA_GATE_TAU = 16.0
GLA_CHUNK = 64
GLA_SUB = 16
D_FF = 2816
CONV_WIDTH = 3
ROPE_THETA = 10000.0
EPS = 1e-6

LANES = 128
LOG2_E = 1.4426950408889634
NEG_BIG = -1e30
EXP_CAP = 80.0

PROJ_TILE = 512
GLA_TILE = 512
FFN_TILE = 512
VMEM_LIMIT = 56 * 1024 * 1024

f32 = jnp.float32
bf16 = jnp.bfloat16


def _dot(a, b):
    return jnp.dot(a, b, preferred_element_type=f32)


def _dot_nt(a, b):
    return lax.dot_general(a, b, (((1,), (1,)), ((), ())), preferred_element_type=f32)


def _dot_tn(a, b):
    return lax.dot_general(a, b, (((0,), (0,)), ((), ())), preferred_element_type=f32)


def _rms(xf, g):
    return xf * lax.rsqrt(jnp.mean(xf * xf, axis=-1, keepdims=True) + EPS) * g


def _proj_kernel(x_ref, g_ref, wqt_ref, wk_ref, wvt_ref, wg_ref, wgu_hi_ref, wgu_lo_ref, bg_ref,
                 cos_t_ref, sin_t_ref, cos_r_ref, sin_r_ref,
                 qt_ref, k_ref, vt_ref, kmean_ref, gq_ref, gk_ref, gv_ref, gr_ref, la_ref):
    T = x_ref.shape[1]
    xn = _rms(x_ref[0], g_ref[...]).astype(bf16)

    nblk = T // MOBA_BLOCK
    vg = _dot_nt(wvt_ref[...], xn)
    vt = vg[:MOBA_WIDTH].astype(bf16)
    for j in range(nblk):
        vt_ref[0, j] = vt[:, j * MOBA_BLOCK:(j + 1) * MOBA_BLOCK]

    gate_lr = vg[MOBA_WIDTH:]
    lr_hi = gate_lr.astype(bf16)
    lr_lo = (gate_lr - lr_hi.astype(f32)).astype(bf16)
    z = (_dot_tn(lr_hi, wgu_hi_ref[...]) + _dot_tn(lr_lo, wgu_hi_ref[...])
         + _dot_tn(lr_hi, wgu_lo_ref[...]) + bg_ref[...])
    log_sig = jnp.minimum(z, 0.0) - jnp.log1p(jnp.exp(-jnp.abs(z)))
    la_ref[0] = log_sig * (1.0 / GLA_GATE_TAU)

    qt = _dot_nt(wqt_ref[...], xn)
    cos_t = cos_t_ref[...]
    sin_t = sin_t_ref[...]
    half = MOBA_HEAD_DIM // 2
    scale = MOBA_HEAD_DIM ** -0.5 * LOG2_E
    for h in range(MOBA_HEADS):
        r0 = h * MOBA_HEAD_DIM
        t1 = qt[r0:r0 + half]
        t2 = qt[r0 + half:r0 + MOBA_HEAD_DIM]
        lo = ((t1 * cos_t - t2 * sin_t) * scale).astype(bf16)
        hi = ((t2 * cos_t + t1 * sin_t) * scale).astype(bf16)
        for j in range(T // MOBA_QTILE):
            qt_ref[0, j, r0:r0 + half, :] = lo[:, j * MOBA_QTILE:(j + 1) * MOBA_QTILE]
            qt_ref[0, j, r0 + half:r0 + MOBA_HEAD_DIM, :] = hi[:, j * MOBA_QTILE:(j + 1) * MOBA_QTILE]

    k = _dot(xn, wk_ref[...])
    cos_r = cos_r_ref[...]
    sin_r = sin_r_ref[...]
    lane = lax.broadcasted_iota(jnp.int32, (T, LANES), 1)
    first_half = (lane % MOBA_HEAD_DIM) < half
    for p in range(MOBA_WIDTH // LANES):
        kp = k[:, p * LANES:(p + 1) * LANES]
        rot = jnp.where(first_half, pltpu.roll(kp, LANES - half, 1), pltpu.roll(kp, half, 1))
        kr = kp * cos_r + rot * sin_r
        k_ref[0, :, p * LANES:(p + 1) * LANES] = kr.astype(bf16)
        for j in range(nblk):
            kmean_ref[0, 0, j:j + 1, p * LANES:(p + 1) * LANES] = jnp.mean(
                kr[j * MOBA_BLOCK:(j + 1) * MOBA_BLOCK], axis=0, keepdims=True)

    pg = _dot(xn, wg_ref[...])
    o_gk = GLA_KWIDTH
    o_gv = o_gk + GLA_KWIDTH
    o_gr = o_gv + GLA_VWIDTH
    gq_ref[0] = pg[:, :o_gk] * (GLA_KEY_DIM ** -0.5)
    gk_ref[0] = pg[:, o_gk:o_gv]
    gv_ref[0] = pg[:, o_gv:o_gr]
    gr_ref[0] = pg[:, o_gr:]


def _proj_call(x, g, wqt, wk, wvt, wg, wgu_hi, wgu_lo, bg, cos_t, sin_t, cos_r, sin_r):
    B, S, D = x.shape
    T = PROJ_TILE
    nb_t = T // MOBA_BLOCK
    const = lambda shape: pl.BlockSpec(shape, lambda b, i: (0,) * len(shape))
    tok = lambda w: pl.BlockSpec((1, T, w), lambda b, i: (b, i, 0))
    out_shape = (
        jax.ShapeDtypeStruct((B, S // MOBA_QTILE, MOBA_WIDTH, MOBA_QTILE), bf16),
        jax.ShapeDtypeStruct((B, S, MOBA_WIDTH), bf16),
        jax.ShapeDtypeStruct((B, S // MOBA_BLOCK, MOBA_WIDTH, MOBA_BLOCK), bf16),
        jax.ShapeDtypeStruct((B, S // T, nb_t, MOBA_WIDTH), f32),
        jax.ShapeDtypeStruct((B, S, GLA_KWIDTH), f32),
        jax.ShapeDtypeStruct((B, S, GLA_KWIDTH), f32),
        jax.ShapeDtypeStruct((B, S, GLA_VWIDTH), f32),
        jax.ShapeDtypeStruct((B, S, GLA_VWIDTH), f32),
        jax.ShapeDtypeStruct((B, S, GLA_KWIDTH), f32),
    )
    out_specs = (
        pl.BlockSpec((1, T // MOBA_QTILE, MOBA_WIDTH, MOBA_QTILE), lambda b, i: (b, i, 0, 0)),
        tok(MOBA_WIDTH),
        pl.BlockSpec((1, nb_t, MOBA_WIDTH, MOBA_BLOCK), lambda b, i: (b, i, 0, 0)),
        pl.BlockSpec((1, 1, nb_t, MOBA_WIDTH), lambda b, i: (b, i, 0, 0)),
        tok(GLA_KWIDTH), tok(GLA_KWIDTH), tok(GLA_VWIDTH), tok(GLA_VWIDTH), tok(GLA_KWIDTH),
    )
    in_specs = [
        tok(D),
        const((1, D)),
        const(wqt.shape), const(wk.shape), const(wvt.shape), const(wg.shape),
        const(wgu_hi.shape), const(wgu_lo.shape), const((1, GLA_KWIDTH)),
        pl.BlockSpec((MOBA_HEAD_DIM // 2, T), lambda b, i: (0, i)),
        pl.BlockSpec((MOBA_HEAD_DIM // 2, T), lambda b, i: (0, i)),
        pl.BlockSpec((T, LANES), lambda b, i: (i, 0)),
        pl.BlockSpec((T, LANES), lambda b, i: (i, 0)),
    ]
    return pl.pallas_call(
        _proj_kernel,
        grid=(B, S // T),
        in_specs=in_specs,
        out_specs=out_specs,
        out_shape=out_shape,
        compiler_params=pltpu.CompilerParams(
            dimension_semantics=("parallel", "parallel"), vmem_limit_bytes=VMEM_LIMIT),
        name="in_proj",
    )(x, g, wqt, wk, wvt, wg, wgu_hi, wgu_lo, bg, cos_t, sin_t, cos_r, sin_r)


def _moba_kernel(qt_ref, k_ref, vt_ref, kmean_ref, onehot_ref, o_ref, s_ref, p_ref):
    NT = qt_ref.shape[1]
    QB = qt_ref.shape[3]
    NB = kmean_ref.shape[1]
    HD = MOBA_HEAD_DIM
    U = MOBA_UNROLL
    BLK = MOBA_BLOCK
    ROWS = MOBA_ROWS
    assert QB == 2 * BLK

    km = kmean_ref[0]
    km_hi = km.astype(bf16)
    km_lo = (km - km_hi.astype(f32)).astype(bf16)
    blk = lax.broadcasted_iota(jnp.int32, (NB, QB), 0)
    second_half = lax.broadcasted_iota(jnp.int32, (NB, QB), 1) >= BLK
    zeros_half = jnp.zeros((HD, QB), bf16)
    zeros_tail = jnp.zeros((LANES - NB, QB), bf16)
    own_r = lax.broadcasted_iota(jnp.int32, (QB, QB), 0)
    own_c = lax.broadcasted_iota(jnp.int32, (QB, QB), 1)
    causal = own_r <= own_c
    same_half_or_upper = jnp.logical_or(own_r >= BLK, own_c < BLK)

    def select(qi):
        first = 2 * qi
        own_blk = jnp.where(second_half, first + 1, first)
        out = []
        for h in range(2):
            q_h = qt_ref[0, qi, h * HD:(h + 1) * HD, :]
            qz = jnp.concatenate([q_h, zeros_half] if h == 0 else [zeros_half, q_h], axis=0)
            gate = _dot(km_hi, qz) + _dot(km_lo, qz)
            gate = jnp.where(blk < own_blk, gate, -jnp.inf)
            sel = jnp.zeros((NB, QB), jnp.bool_)
            for _ in range(MOBA_TOPK):
                top = jnp.max(gate, axis=0, keepdims=True)
                idx = jnp.min(jnp.where(gate == top, blk, NB), axis=0, keepdims=True)
                pick = blk == idx
                sel = jnp.logical_or(sel, jnp.logical_and(pick, top > -jnp.inf))
                gate = jnp.where(pick, -jnp.inf, gate)
            bias = jnp.where(jnp.logical_and(sel, blk < first), 0.0, NEG_BIG).astype(bf16)
            picked_first = jnp.max(
                jnp.where(jnp.logical_and(sel, blk == first), 1.0, 0.0), axis=0, keepdims=True)
            out.append((jnp.concatenate([qz, bias, zeros_tail], axis=0), picked_first))
        return tuple(out)

    def block_scores(qzs, j, u, slot, tops):
        r0 = pl.multiple_of(j * BLK, BLK)
        k_aug = jnp.concatenate(
            [k_ref[0, pl.ds(r0, BLK), :], onehot_ref[pl.ds(r0, BLK), :]], axis=1)
        out = []
        for h in range(2):
            s = _dot(k_aug, qzs[h][0])
            s_ref[slot, h, u * BLK:(u + 1) * BLK, :] = s
            top = jnp.max(s, axis=0, keepdims=True)
            out.append(top if tops is None else jnp.maximum(tops[h], top))
        return tuple(out)

    def own_scores(qzs, qi):
        k_own = k_ref[0, pl.ds(pl.multiple_of(qi * QB, QB), QB), :]
        out = []
        for h in range(2):
            qz, picked_first = qzs[h]
            allowed = jnp.logical_and(causal, jnp.logical_or(same_half_or_upper, picked_first > 0.0))
            s = jnp.where(allowed, _dot(k_own, qz[:LANES]), NEG_BIG)
            out.append((s, jnp.max(s, axis=0, keepdims=True)))
        return tuple(out)

    def block_softmax(load_rows, u, m_new, part, h):
        for r in range(u * BLK, (u + 1) * BLK, ROWS):
            e = jnp.exp2(load_rows(r) - m_new)
            part = part + e
            p_ref[h, r:r + ROWS, :] = e.astype(bf16)
        return part

    def block_values(j, u, pv, h):
        d = _dot(vt_ref[0, j, h * HD:(h + 1) * HD, :], p_ref[h, u * BLK:(u + 1) * BLK, :])
        return d if pv is None else pv + d

    def group(t, tops, mls, accs, load_rows, prefetch):
        m_new = [jnp.maximum(mls[h][0], tops[h]) for h in range(2)]
        alpha = [jnp.exp2(mls[h][0] - m_new[h]) for h in range(2)]
        part = [jnp.zeros((ROWS, QB), f32)] * 2
        pv = [None, None]
        tops_next = None
        for u in range(U):
            tops_next = prefetch(u, tops_next)
            for h in range(2):
                part[h] = block_softmax(load_rows[h], u, m_new[h], part[h], h)
            for h in range(2):
                pv[h] = block_values(t * U + u, u, pv[h], h)
        mls = tuple((m_new[h], alpha[h] * mls[h][1] + jnp.sum(part[h], axis=0, keepdims=True))
                    for h in range(2))
        accs = tuple(alpha[h] * accs[h] + pv[h] for h in range(2))
        return tops_next, mls, accs

    def slot_rows(slot):
        return [lambda r, h=h: s_ref[slot, h, r:r + ROWS, :] for h in range(2)]

    ml0 = (jnp.full((1, QB), -jnp.inf, f32), jnp.zeros((1, QB), f32))
    acc0 = jnp.zeros((HD, QB), f32)

    def tile(qi, carry):
        qzs, qzs_next, own, tops0, phase = carry
        n_groups = jnp.maximum((2 * qi + U - 1) // U, 1)

        def step(t, c, cur):
            tops, mls, accs = c
            return group(t, tops, mls, accs, slot_rows(cur),
                         lambda u, tn: block_scores(qzs, (t + 1) * U + u, u, 1 - cur, tn))

        def body(t, c):
            return lax.cond((t + phase) % 2 == 0,
                            functools.partial(step, t, cur=0), functools.partial(step, t, cur=1), c)

        state = lax.fori_loop(0, n_groups - 1, body, (tops0, (ml0, ml0), (acc0, acc0)))

        def boundary(c, cur):
            tops, mls, accs = c
            m_new = [jnp.maximum(mls[h][0], jnp.maximum(tops[h], own[h][1])) for h in range(2)]
            alpha = [jnp.exp2(mls[h][0] - m_new[h]) for h in range(2)]
            part = [jnp.zeros((ROWS, QB), f32)] * 2
            pv = [None, None]
            tops_next = None
            load_rows = slot_rows(cur)
            q_next = jnp.minimum(qi + 1, NT - 1)
            for u in range(U + 1):
                if u < U:
                    tops_next = block_scores(qzs_next, u, u, 1 - cur, tops_next)
                else:
                    own_next = own_scores(qzs_next, q_next)
                for h in range(2):
                    if u == 0:
                        for r in range(0, QB, ROWS):
                            e = jnp.exp2(own[h][0][r:r + ROWS] - m_new[h])
                            part[h] = part[h] + e
                            p_ref[h, U * BLK + r:U * BLK + r + ROWS, :] = e.astype(bf16)
                    else:
                        part[h] = block_softmax(load_rows[h], u - 1, m_new[h], part[h], h)
                for h in range(2):
                    if u == 0:
                        for b in range(2):
                            pv[h] = block_values(2 * qi + b, U + b, pv[h], h)
                    else:
                        pv[h] = block_values((n_groups - 1) * U + u - 1, u - 1, pv[h], h)
            qzs_after = select(jnp.minimum(qi + 2, NT - 1))
            outs = []
            for h in range(2):
                l = alpha[h] * mls[h][1] + jnp.sum(part[h], axis=0, keepdims=True)
                outs.append((alpha[h] * accs[h] + pv[h]) / l)
            o_t = jnp.concatenate(outs, axis=0)
            r0 = pl.multiple_of(qi * QB, QB)
            o_ref[0, pl.ds(r0, QB), :] = o_t.T.astype(o_ref.dtype)
            return qzs_after, own_next, tops_next

        last_slot = (n_groups - 1 + phase) % 2
        qzs_after, own_next, tops_next = lax.cond(
            last_slot == 0, functools.partial(boundary, cur=0), functools.partial(boundary, cur=1),
            state)
        return qzs_next, qzs_after, own_next, tops_next, 1 - last_slot

    qzs0 = select(0)
    qzs1 = select(1)
    tops0 = None
    for u in range(U):
        tops0 = block_scores(qzs0, u, u, 0, tops0)
    lax.fori_loop(0, NT, tile, (qzs0, qzs1, own_scores(qzs0, 0), tops0, jnp.int32(0)))


def _moba_call(qt, k, vt, kmean):
    B, S, W = k.shape
    NB = S // MOBA_BLOCK
    QB = MOBA_QTILE
    n_hp = W // LANES
    assert NB % MOBA_UNROLL == 0 and NB <= LANES and S % QB == 0
    onehot = (jnp.arange(S)[:, None] // MOBA_BLOCK == jnp.arange(LANES)[None, :]).astype(bf16)
    return pl.pallas_call(
        _moba_kernel,
        grid=(B, n_hp),
        in_specs=[
            pl.BlockSpec((1, S // QB, LANES, QB), lambda b, hp: (b, 0, hp, 0)),
            pl.BlockSpec((1, S, LANES), lambda b, hp: (b, 0, hp)),
            pl.BlockSpec((1, NB, LANES, MOBA_BLOCK), lambda b, hp: (b, 0, hp, 0)),
            pl.BlockSpec((1, NB, LANES), lambda b, hp: (b, 0, hp)),
            pl.BlockSpec((S, LANES), lambda b, hp: (0, 0)),
        ],
        out_specs=pl.BlockSpec((1, S, LANES), lambda b, hp: (b, 0, hp)),
        out_shape=jax.ShapeDtypeStruct((B, S, W), bf16),
        scratch_shapes=[
            pltpu.VMEM((2, 2, MOBA_UNROLL * MOBA_BLOCK, QB), f32),
            pltpu.VMEM((2, MOBA_UNROLL * MOBA_BLOCK + QB, QB), bf16),
        ],
        compiler_params=pltpu.CompilerParams(
            dimension_semantics=("parallel", "parallel"),
            vmem_limit_bytes=VMEM_LIMIT),
        name="moba",
    )(qt, k, vt, kmean, onehot)


def _split3(a):
    hi = a.astype(bf16)
    r1 = a - hi.astype(f32)
    mid = r1.astype(bf16)
    lo = (r1 - mid.astype(f32)).astype(bf16)
    return hi, mid, lo


def _gla_kernel(gq_ref, gk_ref, gv_ref, gr_ref, la_ref, gn_ref, y_ref, st_ref):
    C = GLA_CHUNK
    H = GLA_HEADS
    KW = GLA_KWIDTH
    DV = GLA_VAL_DIM
    NSUB = C // GLA_SUB
    T = gq_ref.shape[1]

    @pl.when(pl.program_id(1) == 0)
    def _():
        st_ref[...] = jnp.zeros_like(st_ref)

    row = lax.broadcasted_iota(jnp.int32, (C, C), 0)
    col = lax.broadcasted_iota(jnp.int32, (C, C), 1)
    tril = (col <= row).astype(bf16)
    lane_head = lax.broadcasted_iota(jnp.int32, (C, KW), 1) // GLA_KEY_DIM
    rt = lax.broadcasted_iota(jnp.int32, (C, NSUB * C), 0)
    rc = lax.broadcasted_iota(jnp.int32, (C, NSUB * C), 1)
    keep = jnp.logical_and(rc // C == rt // GLA_SUB, rc % C <= rt)
    st_lane_head = lax.broadcasted_iota(jnp.int32, (DV, KW), 1) // GLA_KEY_DIM

    n_chunks = T // C
    chunk_rows = [slice(c * C, (c + 1) * C) for c in range(n_chunks)]

    Gs = []
    for rows in chunk_rows:
        hi, mid, lo = _split3(la_ref[0, rows, :])
        Gs.append(_dot(tril, hi) + _dot(tril, mid) + _dot(tril, lo))

    rs, upds, qz_sts, decays, vs = [], [], [], [], []
    for rows, G in zip(chunk_rows, Gs):
        q = gq_ref[0, rows, :]
        k = gk_ref[0, rows, :]
        v = gv_ref[0, rows, :].astype(bf16)
        g_last = G[C - 1:C, :]
        g_ref_rows = [G[i * GLA_SUB:i * GLA_SUB + 1, :] for i in range(NSUB)]
        g_own = jnp.concatenate(
            [jnp.broadcast_to(g, (GLA_SUB, KW)) for g in g_ref_rows], axis=0)
        q_in = q * jnp.exp(G - g_own)
        k_in = jnp.concatenate(
            [(k * jnp.exp(jnp.minimum(g - G, EXP_CAP))).astype(bf16) for g in g_ref_rows],
            axis=0)
        q_st = q * jnp.exp(G)
        k_st = (k * jnp.exp(g_last - G)).astype(bf16)
        qz_in = jnp.concatenate(
            [jnp.where(lane_head == h, q_in, 0.0).astype(bf16) for h in range(H)], axis=0)
        qz_sts.append(jnp.concatenate(
            [jnp.where(lane_head == h, q_st, 0.0).astype(bf16) for h in range(H)], axis=0))
        rs.append(_dot_nt(qz_in, k_in))
        upds.append(_dot_tn(v, k_st))
        decays.append(jnp.exp(g_last))
        vs.append(v)

    o_intras = []
    for r, v in zip(rs, vs):
        per_head = []
        for h in range(H):
            r_h = jnp.where(keep, r[h * C:(h + 1) * C, :], 0.0).astype(bf16)
            v_rep = jnp.concatenate([v[:, h * DV:(h + 1) * DV]] * NSUB, axis=0)
            per_head.append(_dot(r_h, v_rep))
        o_intras.append(per_head)

    st = st_ref[...]
    o_inters = []
    for qz_st, upd, decay in zip(qz_sts, upds, decays):
        o_inters.append(_dot_nt(qz_st, st.astype(bf16)))
        st = st * decay
        for h in range(H):
            st = st + jnp.where(st_lane_head == h, upd[h * DV:(h + 1) * DV, :], 0.0)
    st_ref[...] = st

    for rows, o_intra, o_inter in zip(chunk_rows, o_intras, o_inters):
        for h in range(H):
            o = o_intra[h] + o_inter[h * C:(h + 1) * C, :]
            o = o * lax.rsqrt(jnp.mean(o * o, axis=-1, keepdims=True) + EPS)
            o = o * gn_ref[:, h * DV:(h + 1) * DV]
            gr = gr_ref[0, rows, h * DV:(h + 1) * DV]
            y = o * (gr * jax.nn.sigmoid(gr))
            y_ref[0, rows, h * DV:(h + 1) * DV] = y.astype(y_ref.dtype)


def _gla_call(gq, gk, gv, gr, la, gn):
    B, S, _ = gq.shape
    T = GLA_TILE
    tok = lambda w: pl.BlockSpec((1, T, w), lambda b, i: (b, i, 0))
    return pl.pallas_call(
        _gla_kernel,
        grid=(B, S // T),
        in_specs=[tok(GLA_KWIDTH), tok(GLA_KWIDTH), tok(GLA_VWIDTH), tok(GLA_VWIDTH),
                  tok(GLA_KWIDTH), pl.BlockSpec((1, GLA_VWIDTH), lambda b, i: (0, 0))],
        out_specs=tok(GLA_VWIDTH),
        out_shape=jax.ShapeDtypeStruct((B, S, GLA_VWIDTH), bf16),
        scratch_shapes=[pltpu.VMEM((GLA_VAL_DIM, GLA_KWIDTH), f32)],
        compiler_params=pltpu.CompilerParams(
            dimension_semantics=("parallel", "arbitrary"), vmem_limit_bytes=VMEM_LIMIT),
        name="gla",
    )(gq, gk, gv, gr, la, gn)


def _ffn_kernel(x_ref, ym_ref, yg_ref, wo_m_ref, wo_g_ref, fg_ref, wup_ref, cw_ref, cb_ref,
                wdn_ref, og_ref, out_ref, u_ref):
    T = x_ref.shape[1]
    PAD = 8

    @pl.when(pl.program_id(1) == 0)
    def _():
        u_ref[0:PAD, :] = jnp.zeros((PAD, u_ref.shape[1]), f32)

    h = x_ref[0] + _dot(ym_ref[0], wo_m_ref[...]) + _dot(yg_ref[0], wo_g_ref[...])
    hn = _rms(h, fg_ref[...]).astype(bf16)
    u_ref[PAD:PAD + T, :] = _dot(hn, wup_ref[...])
    cw = cw_ref[...]
    conv = (cw[0:1] * u_ref[PAD - 2:PAD - 2 + T, :]
            + cw[1:2] * u_ref[PAD - 1:PAD - 1 + T, :]
            + cw[2:3] * u_ref[PAD:PAD + T, :]
            + cb_ref[...])
    u_ref[0:PAD, :] = u_ref[T:T + PAD, :]
    hg = conv[:, :D_FF]
    act = (hg * jax.nn.sigmoid(hg) * conv[:, D_FF:]).astype(bf16)
    y = h + _dot(act, wdn_ref[...])
    out_ref[0] = _rms(y, og_ref[...])


def _ffn_call(x, ym, yg, wo_m, wo_g, fg, wup, cw, cb, wdn, og):
    B, S, D = x.shape
    T = FFN_TILE
    tok = lambda w: pl.BlockSpec((1, T, w), lambda b, i: (b, i, 0))
    const = lambda a: pl.BlockSpec(a.shape, lambda b, i: (0,) * a.ndim, pipeline_mode=pl.Buffered(1))
    return pl.pallas_call(
        _ffn_kernel,
        grid=(B, S // T),
        in_specs=[tok(D), tok(MOBA_WIDTH), tok(GLA_VWIDTH),
                  const(wo_m), const(wo_g), const(fg), const(wup), const(cw), const(cb),
                  const(wdn), const(og)],
        out_specs=tok(D),
        out_shape=jax.ShapeDtypeStruct((B, S, D), x.dtype),
        scratch_shapes=[pltpu.VMEM((T + 8, 2 * D_FF), f32)],
        compiler_params=pltpu.CompilerParams(
            dimension_semantics=("parallel", "arbitrary"), vmem_limit_bytes=VMEM_LIMIT),
        name="out_ffn",
    )(x, ym, yg, wo_m, wo_g, fg, wup, cw, cb, wdn, og)


def _rope_tables(S):
    hd = MOBA_HEAD_DIM
    inv_freq = 1.0 / (ROPE_THETA ** (jnp.arange(0, hd, 2, dtype=f32) / hd))
    ang = jnp.arange(S).astype(f32)[:, None] * inv_freq[None, :]
    cos, sin = jnp.cos(ang), jnp.sin(ang)
    cos_r = jnp.tile(cos, (1, LANES // (hd // 2)))
    sin_r = jnp.tile(jnp.concatenate([-sin, sin], axis=1), (1, LANES // hd))
    return cos.T, sin.T, cos_r, sin_r


def kernel(x, attn_norm_g, w_in, w_gate_up, b_gate, gla_norm_g, w_out, ffn_norm_g, w_ffn_up,
           conv_w, conv_b, w_ffn_down, final_norm_g):
    B, S, D = x.shape
    l = 0
    o_mk = MOBA_WIDTH
    o_mv = 2 * MOBA_WIDTH
    o_gq = 3 * MOBA_WIDTH
    o_gg = o_gq + 2 * GLA_KWIDTH + 2 * GLA_VWIDTH
    w = w_in[l]
    wqt = w[:, :o_mk].T.astype(bf16)
    wk = w[:, o_mk:o_mv].astype(bf16)
    wvt = jnp.concatenate([w[:, o_mv:o_gq], w[:, o_gg:]], axis=1).T.astype(bf16)
    wg = w[:, o_gq:o_gg].astype(bf16)
    wgu = w_gate_up[l]
    wgu_hi = wgu.astype(bf16)
    wgu_lo = (wgu - wgu_hi.astype(f32)).astype(bf16)
    cos_t, sin_t, cos_r, sin_r = _rope_tables(S)

    qt, k, vt, kmean, gq, gk, gv, gr, la = _proj_call(
        x, attn_norm_g[l][None, :], wqt, wk, wvt, wg, wgu_hi, wgu_lo, b_gate[l][None, :],
        cos_t, sin_t, cos_r, sin_r)
    kmean = kmean.reshape(B, S // MOBA_BLOCK, MOBA_WIDTH)

    y_moba = _moba_call(qt, k, vt, kmean)
    y_gla = _gla_call(gq, gk, gv, gr, la, gla_norm_g[l].reshape(1, GLA_VWIDTH))

    wo = w_out[l].astype(bf16)
    return _ffn_call(
        x, y_moba, y_gla, wo[:MOBA_WIDTH], wo[MOBA_WIDTH:], ffn_norm_g[l][None, :],
        w_ffn_up[l].astype(bf16), conv_w[l], conv_b[l][None, :], w_ffn_down[l].astype(bf16),
        final_norm_g[None, :])
```

```python
import functools
import math

import jax
import jax.numpy as jnp
from jax import lax
from jax.experimental import pallas as pl
from jax.experimental.pallas import tpu as pltpu

D_MODEL = 1024
MOBA_HEAD_DIM = 64
MOBA_HEADS = 8
MOBA_WIDTH = 512
MOBA_BLOCK = 256
MOBA_TOPK = 3
MOBA_UNROLL = 8
MOBA_ROWS = 32
GLA_HEADS = 4
GLA_VAL_DIM = 128
GLA_KEY_DIM = 64
GLA_VWIDTH = 512
GLA_KWIDTH = 256
GLA_GATE_RANK = 16
GLA_GATE_TAU = 16.0
GLA_CHUNK = 64
GLA_SUB = 16
D_FF = 2816
CONV_WIDTH = 3
ROPE_THETA = 10000.0
EPS = 1e-6

LANES = 128
LOG2_E = 1.4426950408889634
NEG_BIG = -1e30
EXP_CAP = 80.0

PROJ_TILE = 512
GLA_TILE = 512
FFN_TILE = 512
VMEM_LIMIT = 56 * 1024 * 1024

f32 = jnp.float32
bf16 = jnp.bfloat16


def _dot(a, b):
    return jnp.dot(a, b, preferred_element_type=f32)


def _dot_nt(a, b):
    return lax.dot_general(a, b, (((1,), (1,)), ((), ())), preferred_element_type=f32)


def _dot_tn(a, b):
    return lax.dot_general(a, b, (((0,), (0,)), ((), ())), preferred_element_type=f32)


def _rms(xf, g):
    return xf * lax.rsqrt(jnp.mean(xf * xf, axis=-1, keepdims=True) + EPS) * g


def _proj_kernel(x_ref, g_ref, wqt_ref, wk_ref, wvt_ref, wg_ref, wgu_hi_ref, wgu_lo_ref, bg_ref,
                 cos_t_ref, sin_t_ref, cos_r_ref, sin_r_ref,
                 qt_ref, k_ref, vt_ref, kmean_ref, gq_ref, gk_ref, gv_ref, gr_ref, la_ref):
    T = x_ref.shape[1]
    xn = _rms(x_ref[0], g_ref[...]).astype(bf16)

    nblk = T // MOBA_BLOCK
    vg = _dot_nt(wvt_ref[...], xn)
    vt = vg[:MOBA_WIDTH].astype(bf16)
    for j in range(nblk):
        vt_ref[0, j] = vt[:, j * MOBA_BLOCK:(j + 1) * MOBA_BLOCK]

    gate_lr = vg[MOBA_WIDTH:]
    lr_hi = gate_lr.astype(bf16)
    lr_lo = (gate_lr - lr_hi.astype(f32)).astype(bf16)
    z = (_dot_tn(lr_hi, wgu_hi_ref[...]) + _dot_tn(lr_lo, wgu_hi_ref[...])
         + _dot_tn(lr_hi, wgu_lo_ref[...]) + bg_ref[...])
    log_sig = jnp.minimum(z, 0.0) - jnp.log1p(jnp.exp(-jnp.abs(z)))
    la_ref[0] = log_sig * (1.0 / GLA_GATE_TAU)

    qt = _dot_nt(wqt_ref[...], xn)
    cos_t = cos_t_ref[...]
    sin_t = sin_t_ref[...]
    half = MOBA_HEAD_DIM // 2
    scale = MOBA_HEAD_DIM ** -0.5 * LOG2_E
    for h in range(MOBA_HEADS):
        r0 = h * MOBA_HEAD_DIM
        t1 = qt[r0:r0 + half]
        t2 = qt[r0 + half:r0 + MOBA_HEAD_DIM]
        lo = ((t1 * cos_t - t2 * sin_t) * scale).astype(bf16)
        hi = ((t2 * cos_t + t1 * sin_t) * scale).astype(bf16)
        for j in range(nblk):
            qt_ref[0, j, r0:r0 + half, :] = lo[:, j * MOBA_BLOCK:(j + 1) * MOBA_BLOCK]
            qt_ref[0, j, r0 + half:r0 + MOBA_HEAD_DIM, :] = hi[:, j * MOBA_BLOCK:(j + 1) * MOBA_BLOCK]

    k = _dot(xn, wk_ref[...])
    cos_r = cos_r_ref[...]
    sin_r = sin_r_ref[...]
    lane = lax.broadcasted_iota(jnp.int32, (T, LANES), 1)
    first_half = (lane % MOBA_HEAD_DIM) < half
    for p in range(MOBA_WIDTH // LANES):
        kp = k[:, p * LANES:(p + 1) * LANES]
        rot = jnp.where(first_half, pltpu.roll(kp, LANES - half, 1), pltpu.roll(kp, half, 1))
        kr = kp * cos_r + rot * sin_r
        k_ref[0, :, p * LANES:(p + 1) * LANES] = kr.astype(bf16)
        for j in range(nblk):
            kmean_ref[0, 0, j:j + 1, p * LANES:(p + 1) * LANES] = jnp.mean(
                kr[j * MOBA_BLOCK:(j + 1) * MOBA_BLOCK], axis=0, keepdims=True)

    pg = _dot(xn, wg_ref[...])
    o_gk = GLA_KWIDTH
    o_gv = o_gk + GLA_KWIDTH
    o_gr = o_gv + GLA_VWIDTH
    gq_ref[0] = pg[:, :o_gk] * (GLA_KEY_DIM ** -0.5)
    gk_ref[0] = pg[:, o_gk:o_gv]
    gv_ref[0] = pg[:, o_gv:o_gr]
    gr_ref[0] = pg[:, o_gr:]


def _proj_call(x, g, wqt, wk, wvt, wg, wgu_hi, wgu_lo, bg, cos_t, sin_t, cos_r, sin_r):
    B, S, D = x.shape
    T = PROJ_TILE
    nb_t = T // MOBA_BLOCK
    const = lambda shape: pl.BlockSpec(shape, lambda b, i: (0,) * len(shape))
    tok = lambda w: pl.BlockSpec((1, T, w), lambda b, i: (b, i, 0))
    out_shape = (
        jax.ShapeDtypeStruct((B, S // MOBA_BLOCK, MOBA_WIDTH, MOBA_BLOCK), bf16),
        jax.ShapeDtypeStruct((B, S, MOBA_WIDTH), bf16),
        jax.ShapeDtypeStruct((B, S // MOBA_BLOCK, MOBA_WIDTH, MOBA_BLOCK), bf16),
        jax.ShapeDtypeStruct((B, S // T, nb_t, MOBA_WIDTH), f32),
        jax.ShapeDtypeStruct((B, S, GLA_KWIDTH), f32),
        jax.ShapeDtypeStruct((B, S, GLA_KWIDTH), f32),
        jax.ShapeDtypeStruct((B, S, GLA_VWIDTH), f32),
        jax.ShapeDtypeStruct((B, S, GLA_VWIDTH), f32),
        jax.ShapeDtypeStruct((B, S, GLA_KWIDTH), f32),
    )
    out_specs = (
        pl.BlockSpec((1, nb_t, MOBA_WIDTH, MOBA_BLOCK), lambda b, i: (b, i, 0, 0)),
        tok(MOBA_WIDTH),
        pl.BlockSpec((1, nb_t, MOBA_WIDTH, MOBA_BLOCK), lambda b, i: (b, i, 0, 0)),
        pl.BlockSpec((1, 1, nb_t, MOBA_WIDTH), lambda b, i: (b, i, 0, 0)),
        tok(GLA_KWIDTH), tok(GLA_KWIDTH), tok(GLA_VWIDTH), tok(GLA_VWIDTH), tok(GLA_KWIDTH),
    )
    in_specs = [
        tok(D),
        const((1, D)),
        const(wqt.shape), const(wk.shape), const(wvt.shape), const(wg.shape),
        const(wgu_hi.shape), const(wgu_lo.shape), const((1, GLA_KWIDTH)),
        pl.BlockSpec((MOBA_HEAD_DIM // 2, T), lambda b, i: (0, i)),
        pl.BlockSpec((MOBA_HEAD_DIM // 2, T), lambda b, i: (0, i)),
        pl.BlockSpec((T, LANES), lambda b, i: (i, 0)),
        pl.BlockSpec((T, LANES), lambda b, i: (i, 0)),
    ]
    return pl.pallas_call(
        _proj_kernel,
        grid=(B, S // T),
        in_specs=in_specs,
        out_specs=out_specs,
        out_shape=out_shape,
        compiler_params=pltpu.CompilerParams(
            dimension_semantics=("parallel", "parallel"), vmem_limit_bytes=VMEM_LIMIT),
        name="in_proj",
    )(x, g, wqt, wk, wvt, wg, wgu_hi, wgu_lo, bg, cos_t, sin_t, cos_r, sin_r)


def _moba_kernel(qt_ref, k_ref, vt_ref, kmean_ref, onehot_ref, o_ref, s_ref, p_ref):
    NT = qt_ref.shape[1]
    QB = qt_ref.shape[3]
    NB = kmean_ref.shape[1]
    HD = MOBA_HEAD_DIM
    U = MOBA_UNROLL
    BLK = MOBA_BLOCK
    ROWS = MOBA_ROWS

    km = kmean_ref[0]
    km_hi = km.astype(bf16)
    km_lo = (km - km_hi.astype(f32)).astype(bf16)
    blk = lax.broadcasted_iota(jnp.int32, (NB, QB), 0)
    zeros_half = jnp.zeros((HD, QB), bf16)
    zeros_tail = jnp.zeros((LANES - NB, QB), bf16)
    in_causal = (lax.broadcasted_iota(jnp.int32, (BLK, QB), 0)
                 <= lax.broadcasted_iota(jnp.int32, (BLK, QB), 1))

    def select(qi):
        qzs = []
        for h in range(2):
            q_h = qt_ref[0, qi, h * HD:(h + 1) * HD, :]
            qz = jnp.concatenate([q_h, zeros_half] if h == 0 else [zeros_half, q_h], axis=0)
            gate = _dot(km_hi, qz) + _dot(km_lo, qz)
            gate = jnp.where(blk < qi, gate, -jnp.inf)
            sel = jnp.zeros((NB, QB), jnp.bool_)
            for _ in range(MOBA_TOPK):
                top = jnp.max(gate, axis=0, keepdims=True)
                idx = jnp.min(jnp.where(gate == top, blk, NB), axis=0, keepdims=True)
                pick = blk == idx
                sel = jnp.logical_or(sel, jnp.logical_and(pick, top > -jnp.inf))
                gate = jnp.where(pick, -jnp.inf, gate)
            bias = jnp.where(sel, 0.0, NEG_BIG).astype(bf16)
            qzs.append(jnp.concatenate([qz, bias, zeros_tail], axis=0))
        return tuple(qzs)

    def block_scores(qzs, j, u, slot, tops):
        r0 = pl.multiple_of(j * BLK, BLK)
        k_aug = jnp.concatenate(
            [k_ref[0, pl.ds(r0, BLK), :], onehot_ref[pl.ds(r0, BLK), :]], axis=1)
        out = []
        for h in range(2):
            s = _dot(k_aug, qzs[h])
            s_ref[slot, h, u * BLK:(u + 1) * BLK, :] = s
            top = jnp.max(s, axis=0, keepdims=True)
            out.append(top if tops is None else jnp.maximum(tops[h], top))
        return tuple(out)

    def own_scores(qzs, qi):
        k_own = k_ref[0, pl.ds(pl.multiple_of(qi * BLK, BLK), BLK), :]
        out = []
        for h in range(2):
            s = jnp.where(in_causal, _dot(k_own, qzs[h][:LANES]), NEG_BIG)
            out.append((s, jnp.max(s, axis=0, keepdims=True)))
        return tuple(out)

    def block_softmax(load_rows, u, m_new, part, h):
        for r in range(u * BLK, (u + 1) * BLK, ROWS):
            e = jnp.exp2(load_rows(r) - m_new)
            part = part + e
            p_ref[h, r:r + ROWS, :] = e.astype(bf16)
        return part

    def block_values(j, u, pv, h):
        d = _dot(vt_ref[0, j, h * HD:(h + 1) * HD, :], p_ref[h, u * BLK:(u + 1) * BLK, :])
        return d if pv is None else pv + d

    def group(t, tops, mls, accs, load_rows, prefetch):
        m_new = [jnp.maximum(mls[h][0], tops[h]) for h in range(2)]
        alpha = [jnp.exp2(mls[h][0] - m_new[h]) for h in range(2)]
        part = [jnp.zeros((ROWS, QB), f32)] * 2
        pv = [None, None]
        tops_next = None
        for u in range(U):
            tops_next = prefetch(u, tops_next)
            for h in range(2):
                part[h] = block_softmax(load_rows[h], u, m_new[h], part[h], h)
            for h in range(2):
                pv[h] = block_values(t * U + u, u, pv[h], h)
        mls = tuple((m_new[h], alpha[h] * mls[h][1] + jnp.sum(part[h], axis=0, keepdims=True))
                    for h in range(2))
        accs = tuple(alpha[h] * accs[h] + pv[h] for h in range(2))
        return tops_next, mls, accs

    def slot_rows(slot):
        return [lambda r, h=h: s_ref[slot, h, r:r + ROWS, :] for h in range(2)]

    ml0 = (jnp.full((1, QB), -jnp.inf, f32), jnp.zeros((1, QB), f32))
    acc0 = jnp.zeros((HD, QB), f32)

    def tile(qi, carry):
        qzs, qzs_next, own, tops0, phase = carry
        n_groups = jnp.maximum((qi + U - 1) // U, 1)

        def step(t, c, cur):
            tops, mls, accs = c
            return group(t, tops, mls, accs, slot_rows(cur),
                         lambda u, tn: block_scores(qzs, (t + 1) * U + u, u, 1 - cur, tn))

        def body(t, c):
            return lax.cond((t + phase) % 2 == 0,
                            functools.partial(step, t, cur=0), functools.partial(step, t, cur=1), c)

        state = lax.fori_loop(0, n_groups - 1, body, (tops0, (ml0, ml0), (acc0, acc0)))

        def boundary(c, cur):
            tops, mls, accs = c
            m_new = [jnp.maximum(mls[h][0], jnp.maximum(tops[h], own[h][1])) for h in range(2)]
            alpha = [jnp.exp2(mls[h][0] - m_new[h]) for h in range(2)]
            part = [jnp.zeros((ROWS, QB), f32)] * 2
            pv = [None, None]
            tops_next = None
            load_rows = slot_rows(cur)
            q_next = jnp.minimum(qi + 1, NT - 1)
            for u in range(U + 1):
                if u < U:
                    tops_next = block_scores(qzs_next, u, u, 1 - cur, tops_next)
                else:
                    own_next = own_scores(qzs_next, q_next)
                for h in range(2):
                    if u == 0:
                        for r in range(0, BLK, ROWS):
                            e = jnp.exp2(own[h][0][r:r + ROWS] - m_new[h])
                            part[h] = part[h] + e
                            p_ref[h, U * BLK + r:U * BLK + r + ROWS, :] = e.astype(bf16)
                    else:
                        part[h] = block_softmax(load_rows[h], u - 1, m_new[h], part[h], h)
                for h in range(2):
                    if u == 0:
                        pv[h] = _dot(vt_ref[0, qi, h * HD:(h + 1) * HD, :],
                                     p_ref[h, U * BLK:(U + 1) * BLK, :])
                    else:
                        pv[h] = block_values((n_groups - 1) * U + u - 1, u - 1, pv[h], h)
            qzs_after = select(jnp.minimum(qi + 2, NT - 1))
            outs = []
            for h in range(2):
                l = alpha[h] * mls[h][1] + jnp.sum(part[h], axis=0, keepdims=True)
                outs.append((alpha[h] * accs[h] + pv[h]) / l)
            o_t = jnp.concatenate(outs, axis=0)
            r0 = pl.multiple_of(qi * BLK, BLK)
            o_ref[0, pl.ds(r0, BLK), :] = o_t.T.astype(o_ref.dtype)
            return qzs_after, own_next, tops_next

        last_slot = (n_groups - 1 + phase) % 2
        qzs_after, own_next, tops_next = lax.cond(
            last_slot == 0, functools.partial(boundary, cur=0), functools.partial(boundary, cur=1),
            state)
        return qzs_next, qzs_after, own_next, tops_next, 1 - last_slot

    qzs0 = select(0)
    qzs1 = select(1)
    tops0 = None
    for u in range(U):
        tops0 = block_scores(qzs0, u, u, 0, tops0)
    lax.fori_loop(0, NT, tile, (qzs0, qzs1, own_scores(qzs0, 0), tops0, jnp.int32(0)))


def _moba_call(qt, k, vt, kmean):
    B, S, W = k.shape
    NB = S // MOBA_BLOCK
    QB = MOBA_BLOCK
    n_hp = W // LANES
    assert NB % MOBA_UNROLL == 0 and NB <= LANES
    onehot = (jnp.arange(S)[:, None] // MOBA_BLOCK == jnp.arange(LANES)[None, :]).astype(bf16)
    return pl.pallas_call(
        _moba_kernel,
        grid=(B, n_hp),
        in_specs=[
            pl.BlockSpec((1, NB, LANES, QB), lambda b, hp: (b, 0, hp, 0)),
            pl.BlockSpec((1, S, LANES), lambda b, hp: (b, 0, hp)),
            pl.BlockSpec((1, NB, LANES, MOBA_BLOCK), lambda b, hp: (b, 0, hp, 0)),
            pl.BlockSpec((1, NB, LANES), lambda b, hp: (b, 0, hp)),
            pl.BlockSpec((S, LANES), lambda b, hp: (0, 0)),
        ],
        out_specs=pl.BlockSpec((1, S, LANES), lambda b, hp: (b, 0, hp)),
        out_shape=jax.ShapeDtypeStruct((B, S, W), bf16),
        scratch_shapes=[
            pltpu.VMEM((2, 2, MOBA_UNROLL * MOBA_BLOCK, QB), f32),
            pltpu.VMEM((2, (MOBA_UNROLL + 1) * MOBA_BLOCK, QB), bf16),
        ],
        compiler_params=pltpu.CompilerParams(
            dimension_semantics=("parallel", "parallel"),
            vmem_limit_bytes=VMEM_LIMIT),
        name="moba",
    )(qt, k, vt, kmean, onehot)


def _split3(a):
    hi = a.astype(bf16)
    r1 = a - hi.astype(f32)
    mid = r1.astype(bf16)
    lo = (r1 - mid.astype(f32)).astype(bf16)
    return hi, mid, lo


def _gla_kernel(gq_ref, gk_ref, gv_ref, gr_ref, la_ref, gn_ref, y_ref, st_ref):
    C = GLA_CHUNK
    H = GLA_HEADS
    KW = GLA_KWIDTH
    DV = GLA_VAL_DIM
    NSUB = C // GLA_SUB
    T = gq_ref.shape[1]

    @pl.when(pl.program_id(1) == 0)
    def _():
        st_ref[...] = jnp.zeros_like(st_ref)

    row = lax.broadcasted_iota(jnp.int32, (C, C), 0)
    col = lax.broadcasted_iota(jnp.int32, (C, C), 1)
    tril = (col <= row).astype(bf16)
    lane_head = lax.broadcasted_iota(jnp.int32, (C, KW), 1) // GLA_KEY_DIM
    rt = lax.broadcasted_iota(jnp.int32, (C, NSUB * C), 0)
    rc = lax.broadcasted_iota(jnp.int32, (C, NSUB * C), 1)
    keep = jnp.logical_and(rc // C == rt // GLA_SUB, rc % C <= rt)
    st_lane_head = lax.broadcasted_iota(jnp.int32, (DV, KW), 1) // GLA_KEY_DIM

    n_chunks = T // C
    chunk_rows = [slice(c * C, (c + 1) * C) for c in range(n_chunks)]

    Gs = []
    for rows in chunk_rows:
        hi, mid, lo = _split3(la_ref[0, rows, :])
        Gs.append(_dot(tril, hi) + _dot(tril, mid) + _dot(tril, lo))

    rs, upds, qz_sts, decays, vs = [], [], [], [], []
    for rows, G in zip(chunk_rows, Gs):
        q = gq_ref[0, rows, :]
        k = gk_ref[0, rows, :]
        v = gv_ref[0, rows, :].astype(bf16)
        g_last = G[C - 1:C, :]
        g_ref_rows = [G[i * GLA_SUB:i * GLA_SUB + 1, :] for i in range(NSUB)]
        g_own = jnp.concatenate(
            [jnp.broadcast_to(g, (GLA_SUB, KW)) for g in g_ref_rows], axis=0)
        q_in = q * jnp.exp(G - g_own)
        k_in = jnp.concatenate(
            [(k * jnp.exp(jnp.minimum(g - G, EXP_CAP))).astype(bf16) for g in g_ref_rows],
            axis=0)
        q_st = q * jnp.exp(G)
        k_st = (k * jnp.exp(g_last - G)).astype(bf16)
        qz_in = jnp.concatenate(
            [jnp.where(lane_head == h, q_in, 0.0).astype(bf16) for h in range(H)], axis=0)
        qz_sts.append(jnp.concatenate(
            [jnp.where(lane_head == h, q_st, 0.0).astype(bf16) for h in range(H)], axis=0))
        rs.append(_dot_nt(qz_in, k_in))
        upds.append(_dot_tn(v, k_st))
        decays.append(jnp.exp(g_last))
        vs.append(v)

    o_intras = []
    for r, v in zip(rs, vs):
        per_head = []
        for h in range(H):
            r_h = jnp.where(keep, r[h * C:(h + 1) * C, :], 0.0).astype(bf16)
            v_rep = jnp.concatenate([v[:, h * DV:(h + 1) * DV]] * NSUB, axis=0)
            per_head.append(_dot(r_h, v_rep))
        o_intras.append(per_head)

    st = st_ref[...]
    o_inters = []
    for qz_st, upd, decay in zip(qz_sts, upds, decays):
        o_inters.append(_dot_nt(qz_st, st.astype(bf16)))
        st = st * decay
        for h in range(H):
            st = st + jnp.where(st_lane_head == h, upd[h * DV:(h + 1) * DV, :], 0.0)
    st_ref[...] = st

    for rows, o_intra, o_inter in zip(chunk_rows, o_intras, o_inters):
        for h in range(H):
            o = o_intra[h] + o_inter[h * C:(h + 1) * C, :]
            o = o * lax.rsqrt(jnp.mean(o * o, axis=-1, keepdims=True) + EPS)
            o = o * gn_ref[:, h * DV:(h + 1) * DV]
            gr = gr_ref[0, rows, h * DV:(h + 1) * DV]
            y = o * (gr * jax.nn.sigmoid(gr))
            y_ref[0, rows, h * DV:(h + 1) * DV] = y.astype(y_ref.dtype)


def _gla_call(gq, gk, gv, gr, la, gn):
    B, S, _ = gq.shape
    T = GLA_TILE
    tok = lambda w: pl.BlockSpec((1, T, w), lambda b, i: (b, i, 0))
    return pl.pallas_call(
        _gla_kernel,
        grid=(B, S // T),
        in_specs=[tok(GLA_KWIDTH), tok(GLA_KWIDTH), tok(GLA_VWIDTH), tok(GLA_VWIDTH),
                  tok(GLA_KWIDTH), pl.BlockSpec((1, GLA_VWIDTH), lambda b, i: (0, 0))],
        out_specs=tok(GLA_VWIDTH),
        out_shape=jax.ShapeDtypeStruct((B, S, GLA_VWIDTH), bf16),
        scratch_shapes=[pltpu.VMEM((GLA_VAL_DIM, GLA_KWIDTH), f32)],
        compiler_params=pltpu.CompilerParams(
            dimension_semantics=("parallel", "arbitrary"), vmem_limit_bytes=VMEM_LIMIT),
        name="gla",
    )(gq, gk, gv, gr, la, gn)


def _ffn_kernel(x_ref, ym_ref, yg_ref, wo_m_ref, wo_g_ref, fg_ref, wup_ref, cw_ref, cb_ref,
                wdn_ref, og_ref, out_ref, u_ref):
    T = x_ref.shape[1]
    PAD = 8

    @pl.when(pl.program_id(1) == 0)
    def _():
        u_ref[0:PAD, :] = jnp.zeros((PAD, u_ref.shape[1]), f32)

    h = x_ref[0] + _dot(ym_ref[0], wo_m_ref[...]) + _dot(yg_ref[0], wo_g_ref[...])
    hn = _rms(h, fg_ref[...]).astype(bf16)
    u_ref[PAD:PAD + T, :] = _dot(hn, wup_ref[...])
    cw = cw_ref[...]
    conv = (cw[0:1] * u_ref[PAD - 2:PAD - 2 + T, :]
            + cw[1:2] * u_ref[PAD - 1:PAD - 1 + T, :]
            + cw[2:3] * u_ref[PAD:PAD + T, :]
            + cb_ref[...])
    u_ref[0:PAD, :] = u_ref[T:T + PAD, :]
    hg = conv[:, :D_FF]
    act = (hg * jax.nn.sigmoid(hg) * conv[:, D_FF:]).astype(bf16)
    y = h + _dot(act, wdn_ref[...])
    out_ref[0] = _rms(y, og_ref[...])


def _ffn_call(x, ym, yg, wo_m, wo_g, fg, wup, cw, cb, wdn, og):
    B, S, D = x.shape
    T = FFN_TILE
    tok = lambda w: pl.BlockSpec((1, T, w), lambda b, i: (b, i, 0))
    const = lambda a: pl.BlockSpec(a.shape, lambda b, i: (0,) * a.ndim, pipeline_mode=pl.Buffered(1))
    return pl.pallas_call(
        _ffn_kernel,
        grid=(B, S // T),
        in_specs=[tok(D), tok(MOBA_WIDTH), tok(GLA_VWIDTH),
                  const(wo_m), const(wo_g), const(fg), const(wup), const(cw), const(cb),
                  const(wdn), const(og)],
        out_specs=tok(D),
        out_shape=jax.ShapeDtypeStruct((B, S, D), x.dtype),
        scratch_shapes=[pltpu.VMEM((T + 8, 2 * D_FF), f32)],
        compiler_params=pltpu.CompilerParams(
            dimension_semantics=("parallel", "arbitrary"), vmem_limit_bytes=VMEM_LIMIT),
        name="out_ffn",
    )(x, ym, yg, wo_m, wo_g, fg, wup, cw, cb, wdn, og)


def _rope_tables(S):
    hd = MOBA_HEAD_DIM
    inv_freq = 1.0 / (ROPE_THETA ** (jnp.arange(0, hd, 2, dtype=f32) / hd))
    ang = jnp.arange(S).astype(f32)[:, None] * inv_freq[None, :]
    cos, sin = jnp.cos(ang), jnp.sin(ang)
    cos_r = jnp.tile(cos, (1, LANES // (hd // 2)))
    sin_r = jnp.tile(jnp.concatenate([-sin, sin], axis=1), (1, LANES // hd))
    return cos.T, sin.T, cos_r, sin_r


def kernel(x, attn_norm_g, w_in, w_gate_up, b_gate, gla_norm_g, w_out, ffn_norm_g, w_ffn_up,
           conv_w, conv_b, w_ffn_down, final_norm_g):
    B, S, D = x.shape
    l = 0
    o_mk = MOBA_WIDTH
    o_mv = 2 * MOBA_WIDTH
    o_gq = 3 * MOBA_WIDTH
    o_gg = o_gq + 2 * GLA_KWIDTH + 2 * GLA_VWIDTH
    w = w_in[l]
    wqt = w[:, :o_mk].T.astype(bf16)
    wk = w[:, o_mk:o_mv].astype(bf16)
    wvt = jnp.concatenate([w[:, o_mv:o_gq], w[:, o_gg:]], axis=1).T.astype(bf16)
    wg = w[:, o_gq:o_gg].astype(bf16)
    wgu = w_gate_up[l]
    wgu_hi = wgu.astype(bf16)
    wgu_lo = (wgu - wgu_hi.astype(f32)).astype(bf16)
    cos_t, sin_t, cos_r, sin_r = _rope_tables(S)

    qt, k, vt, kmean, gq, gk, gv, gr, la = _proj_call(
        x, attn_norm_g[l][None, :], wqt, wk, wvt, wg, wgu_hi, wgu_lo, b_gate[l][None, :],
        cos_t, sin_t, cos_r, sin_r)
    kmean = kmean.reshape(B, S // MOBA_BLOCK, MOBA_WIDTH)

    y_moba = _moba_call(qt, k, vt, kmean)
    y_gla = _gla_call(gq, gk, gv, gr, la, gla_norm_g[l].reshape(1, GLA_VWIDTH))

    wo = w_out[l].astype(bf16)
    return _ffn_call(
        x, y_moba, y_gla, wo[:MOBA_WIDTH], wo[MOBA_WIDTH:], ffn_norm_g[l][None, :],
        w_ffn_up[l].astype(bf16), conv_w[l], conv_b[l][None, :], w_ffn_down[l].astype(bf16),
        final_norm_g[None, :])
```

```python
import functools
import math

import jax
import jax.numpy as jnp
from jax import lax
from jax.experimental import pallas as pl
from jax.experimental.pallas import tpu as pltpu

D_MODEL = 1024
MOBA_HEAD_DIM = 64
MOBA_HEADS = 8
MOBA_WIDTH = 512
MOBA_BLOCK = 256
MOBA_TOPK = 3
MOBA_UNROLL = 8
MOBA_ROWS = 32
GLA_HEADS = 4
GLA_VAL_DIM = 128
GLA_KEY_DIM = 64
GLA_VWIDTH = 512
GLA_KWIDTH = 256
GLA_GATE_RANK = 16
GLA_GATE_TAU = 16.0
GLA_CHUNK = 64
GLA_SUB = 16
D_FF = 2816
CONV_WIDTH = 3
ROPE_THETA = 10000.0
EPS = 1e-6

LANES = 128
LOG2_E = 1.4426950408889634
NEG_BIG = -1e30
EXP_CAP = 80.0

PROJ_TILE = 512
GLA_TILE = 512
FFN_TILE = 512
VMEM_LIMIT = 56 * 1024 * 1024

f32 = jnp.float32
bf16 = jnp.bfloat16


def _dot(a, b):
    return jnp.dot(a, b, preferred_element_type=f32)


def _dot_nt(a, b):
    return lax.dot_general(a, b, (((1,), (1,)), ((), ())), preferred_element_type=f32)


def _dot_tn(a, b):
    return lax.dot_general(a, b, (((0,), (0,)), ((), ())), preferred_element_type=f32)


def _rms(xf, g):
    return xf * lax.rsqrt(jnp.mean(xf * xf, axis=-1, keepdims=True) + EPS) * g


def _proj_kernel(x_ref, g_ref, wqt_ref, wk_ref, wvt_ref, wg_ref, wgu_hi_ref, wgu_lo_ref, bg_ref,
                 cos_t_ref, sin_t_ref, cos_r_ref, sin_r_ref,
                 qt_ref, k_ref, vt_ref, kmean_ref, gq_ref, gk_ref, gv_ref, gr_ref, la_ref):
    T = x_ref.shape[1]
    xn = _rms(x_ref[0], g_ref[...]).astype(bf16)

    nblk = T // MOBA_BLOCK
    vg = _dot_nt(wvt_ref[...], xn)
    vt = vg[:MOBA_WIDTH].astype(bf16)
    for j in range(nblk):
        vt_ref[0, j] = vt[:, j * MOBA_BLOCK:(j + 1) * MOBA_BLOCK]

    gate_lr = vg[MOBA_WIDTH:]
    lr_hi = gate_lr.astype(bf16)
    lr_lo = (gate_lr - lr_hi.astype(f32)).astype(bf16)
    z = (_dot_tn(lr_hi, wgu_hi_ref[...]) + _dot_tn(lr_lo, wgu_hi_ref[...])
         + _dot_tn(lr_hi, wgu_lo_ref[...]) + bg_ref[...])
    log_sig = jnp.minimum(z, 0.0) - jnp.log1p(jnp.exp(-jnp.abs(z)))
    la_ref[0] = log_sig * (1.0 / GLA_GATE_TAU)

    qt = _dot_nt(wqt_ref[...], xn)
    cos_t = cos_t_ref[...]
    sin_t = sin_t_ref[...]
    half = MOBA_HEAD_DIM // 2
    scale = MOBA_HEAD_DIM ** -0.5 * LOG2_E
    for h in range(MOBA_HEADS):
        r0 = h * MOBA_HEAD_DIM
        t1 = qt[r0:r0 + half]
        t2 = qt[r0 + half:r0 + MOBA_HEAD_DIM]
        lo = ((t1 * cos_t - t2 * sin_t) * scale).astype(bf16)
        hi = ((t2 * cos_t + t1 * sin_t) * scale).astype(bf16)
        for j in range(nblk):
            qt_ref[0, j, r0:r0 + half, :] = lo[:, j * MOBA_BLOCK:(j + 1) * MOBA_BLOCK]
            qt_ref[0, j, r0 + half:r0 + MOBA_HEAD_DIM, :] = hi[:, j * MOBA_BLOCK:(j + 1) * MOBA_BLOCK]

    k = _dot(xn, wk_ref[...])
    cos_r = cos_r_ref[...]
    sin_r = sin_r_ref[...]
    lane = lax.broadcasted_iota(jnp.int32, (T, LANES), 1)
    first_half = (lane % MOBA_HEAD_DIM) < half
    for p in range(MOBA_WIDTH // LANES):
        kp = k[:, p * LANES:(p + 1) * LANES]
        rot = jnp.where(first_half, pltpu.roll(kp, LANES - half, 1), pltpu.roll(kp, half, 1))
        kr = kp * cos_r + rot * sin_r
        k_ref[0, :, p * LANES:(p + 1) * LANES] = kr.astype(bf16)
        for j in range(nblk):
            kmean_ref[0, 0, j:j + 1, p * LANES:(p + 1) * LANES] = jnp.mean(
                kr[j * MOBA_BLOCK:(j + 1) * MOBA_BLOCK], axis=0, keepdims=True)

    pg = _dot(xn, wg_ref[...])
    o_gk = GLA_KWIDTH
    o_gv = o_gk + GLA_KWIDTH
    o_gr = o_gv + GLA_VWIDTH
    gq_ref[0] = pg[:, :o_gk] * (GLA_KEY_DIM ** -0.5)
    gk_ref[0] = pg[:, o_gk:o_gv]
    gv_ref[0] = pg[:, o_gv:o_gr]
    gr_ref[0] = pg[:, o_gr:]


def _proj_call(x, g, wqt, wk, wvt, wg, wgu_hi, wgu_lo, bg, cos_t, sin_t, cos_r, sin_r):
    B, S, D = x.shape
    T = PROJ_TILE
    nb_t = T // MOBA_BLOCK
    const = lambda shape: pl.BlockSpec(shape, lambda b, i: (0,) * len(shape))
    tok = lambda w: pl.BlockSpec((1, T, w), lambda b, i: (b, i, 0))
    out_shape = (
        jax.ShapeDtypeStruct((B, S // MOBA_BLOCK, MOBA_WIDTH, MOBA_BLOCK), bf16),
        jax.ShapeDtypeStruct((B, S, MOBA_WIDTH), bf16),
        jax.ShapeDtypeStruct((B, S // MOBA_BLOCK, MOBA_WIDTH, MOBA_BLOCK), bf16),
        jax.ShapeDtypeStruct((B, S // T, nb_t, MOBA_WIDTH), f32),
        jax.ShapeDtypeStruct((B, S, GLA_KWIDTH), f32),
        jax.ShapeDtypeStruct((B, S, GLA_KWIDTH), f32),
        jax.ShapeDtypeStruct((B, S, GLA_VWIDTH), f32),
        jax.ShapeDtypeStruct((B, S, GLA_VWIDTH), f32),
        jax.ShapeDtypeStruct((B, S, GLA_KWIDTH), f32),
    )
    out_specs = (
        pl.BlockSpec((1, nb_t, MOBA_WIDTH, MOBA_BLOCK), lambda b, i: (b, i, 0, 0)),
        tok(MOBA_WIDTH),
        pl.BlockSpec((1, nb_t, MOBA_WIDTH, MOBA_BLOCK), lambda b, i: (b, i, 0, 0)),
        pl.BlockSpec((1, 1, nb_t, MOBA_WIDTH), lambda b, i: (b, i, 0, 0)),
        tok(GLA_KWIDTH), tok(GLA_KWIDTH), tok(GLA_VWIDTH), tok(GLA_VWIDTH), tok(GLA_KWIDTH),
    )
    in_specs = [
        tok(D),
        const((1, D)),
        const(wqt.shape), const(wk.shape), const(wvt.shape), const(wg.shape),
        const(wgu_hi.shape), const(wgu_lo.shape), const((1, GLA_KWIDTH)),
        pl.BlockSpec((MOBA_HEAD_DIM // 2, T), lambda b, i: (0, i)),
        pl.BlockSpec((MOBA_HEAD_DIM // 2, T), lambda b, i: (0, i)),
        pl.BlockSpec((T, LANES), lambda b, i: (i, 0)),
        pl.BlockSpec((T, LANES), lambda b, i: (i, 0)),
    ]
    return pl.pallas_call(
        _proj_kernel,
        grid=(B, S // T),
        in_specs=in_specs,
        out_specs=out_specs,
        out_shape=out_shape,
        compiler_params=pltpu.CompilerParams(
            dimension_semantics=("parallel", "parallel"), vmem_limit_bytes=VMEM_LIMIT),
        name="in_proj",
    )(x, g, wqt, wk, wvt, wg, wgu_hi, wgu_lo, bg, cos_t, sin_t, cos_r, sin_r)


def _moba_kernel(qt_ref, k_ref, vt_ref, kmean_ref, o_ref, s_ref, p_ref):
    NT = qt_ref.shape[1]
    QB = qt_ref.shape[3]
    NB = kmean_ref.shape[1]
    HD = MOBA_HEAD_DIM
    U = MOBA_UNROLL
    BLK = MOBA_BLOCK
    ROWS = MOBA_ROWS

    km = kmean_ref[0]
    km_hi = km.astype(bf16)
    km_lo = (km - km_hi.astype(f32)).astype(bf16)
    blk = lax.broadcasted_iota(jnp.int32, (NB, QB), 0)
    zeros_half = jnp.zeros((HD, QB), bf16)
    zeros_tail = jnp.zeros((LANES - NB, QB), bf16)
    in_causal = (lax.broadcasted_iota(jnp.int32, (BLK, QB), 0)
                 <= lax.broadcasted_iota(jnp.int32, (BLK, QB), 1))
    lane8 = lax.broadcasted_iota(jnp.int32, (8, LANES), 1)

    def select(qi):
        qzs = []
        for h in range(2):
            q_h = qt_ref[0, qi, h * HD:(h + 1) * HD, :]
            qz = jnp.concatenate([q_h, zeros_half] if h == 0 else [zeros_half, q_h], axis=0)
            gate = _dot(km_hi, qz) + _dot(km_lo, qz)
            gate = jnp.where(blk < qi, gate, -jnp.inf)
            sel = jnp.zeros((NB, QB), jnp.bool_)
            for _ in range(MOBA_TOPK):
                top = jnp.max(gate, axis=0, keepdims=True)
                idx = jnp.min(jnp.where(gate == top, blk, NB), axis=0, keepdims=True)
                pick = blk == idx
                sel = jnp.logical_or(sel, jnp.logical_and(pick, top > -jnp.inf))
                gate = jnp.where(pick, -jnp.inf, gate)
            bias = jnp.where(sel, 0.0, NEG_BIG).astype(bf16)
            qzs.append(jnp.concatenate([qz, bias, zeros_tail], axis=0))
        return tuple(qzs)

    def block_scores(qzs, j, u, slot, tops):
        r0 = pl.multiple_of(j * BLK, BLK)
        onehot = jnp.tile(jnp.where(lane8 == j, 1.0, 0.0), (BLK // 8, 1)).astype(bf16)
        k_aug = jnp.concatenate([k_ref[0, pl.ds(r0, BLK), :], onehot], axis=1)
        out = []
        for h in range(2):
            s = _dot(k_aug, qzs[h])
            s_ref[slot, h, u * BLK:(u + 1) * BLK, :] = s
            top = jnp.max(s, axis=0, keepdims=True)
            out.append(top if tops is None else jnp.maximum(tops[h], top))
        return tuple(out)

    def own_scores(qzs, qi):
        k_own = k_ref[0, pl.ds(pl.multiple_of(qi * BLK, BLK), BLK), :]
        out = []
        for h in range(2):
            s = jnp.where(in_causal, _dot(k_own, qzs[h][:LANES]), NEG_BIG)
            out.append((s, jnp.max(s, axis=0, keepdims=True)))
        return tuple(out)

    def block_softmax(load_rows, u, m_new, part, h):
        for r in range(u * BLK, (u + 1) * BLK, ROWS):
            e = jnp.exp2(load_rows(r) - m_new)
            part = part + e
            p_ref[h, r:r + ROWS, :] = e.astype(bf16)
        return part

    def block_values(j, u, pv, h):
        d = _dot(vt_ref[0, j, h * HD:(h + 1) * HD, :], p_ref[h, u * BLK:(u + 1) * BLK, :])
        return d if pv is None else pv + d

    def group(t, tops, mls, accs, load_rows, prefetch):
        m_new = [jnp.maximum(mls[h][0], tops[h]) for h in range(2)]
        alpha = [jnp.exp2(mls[h][0] - m_new[h]) for h in range(2)]
        part = [jnp.zeros((ROWS, QB), f32)] * 2
        pv = [None, None]
        tops_next = None
        for u in range(U):
            tops_next = prefetch(u, tops_next)
            for h in range(2):
                part[h] = block_softmax(load_rows[h], u, m_new[h], part[h], h)
            for h in range(2):
                pv[h] = block_values(t * U + u, u, pv[h], h)
        mls = tuple((m_new[h], alpha[h] * mls[h][1] + jnp.sum(part[h], axis=0, keepdims=True))
                    for h in range(2))
        accs = tuple(alpha[h] * accs[h] + pv[h] for h in range(2))
        return tops_next, mls, accs

    def slot_rows(slot):
        return [lambda r, h=h: s_ref[slot, h, r:r + ROWS, :] for h in range(2)]

    ml0 = (jnp.full((1, QB), -jnp.inf, f32), jnp.zeros((1, QB), f32))
    acc0 = jnp.zeros((HD, QB), f32)

    def tile(qi, carry):
        qzs, qzs_next, own, tops0, phase = carry
        n_groups = jnp.maximum((qi + U - 1) // U, 1)

        def step(t, c, cur):
            tops, mls, accs = c
            return group(t, tops, mls, accs, slot_rows(cur),
                         lambda u, tn: block_scores(qzs, (t + 1) * U + u, u, 1 - cur, tn))

        def body(t, c):
            return lax.cond((t + phase) % 2 == 0,
                            functools.partial(step, t, cur=0), functools.partial(step, t, cur=1), c)

        state = lax.fori_loop(0, n_groups - 1, body, (tops0, (ml0, ml0), (acc0, acc0)))

        def boundary(c, cur):
            tops, mls, accs = c
            m_new = [jnp.maximum(mls[h][0], jnp.maximum(tops[h], own[h][1])) for h in range(2)]
            alpha = [jnp.exp2(mls[h][0] - m_new[h]) for h in range(2)]
            part = [jnp.zeros((ROWS, QB), f32)] * 2
            pv = [None, None]
            tops_next = None
            load_rows = slot_rows(cur)
            q_next = jnp.minimum(qi + 1, NT - 1)
            for u in range(U + 1):
                if u < U:
                    tops_next = block_scores(qzs_next, u, u, 1 - cur, tops_next)
                else:
                    own_next = own_scores(qzs_next, q_next)
                for h in range(2):
                    if u == 0:
                        for r in range(0, BLK, ROWS):
                            e = jnp.exp2(own[h][0][r:r + ROWS] - m_new[h])
                            part[h] = part[h] + e
                            p_ref[h, U * BLK + r:U * BLK + r + ROWS, :] = e.astype(bf16)
                    else:
                        part[h] = block_softmax(load_rows[h], u - 1, m_new[h], part[h], h)
                for h in range(2):
                    if u == 0:
                        pv[h] = _dot(vt_ref[0, qi, h * HD:(h + 1) * HD, :],
                                     p_ref[h, U * BLK:(U + 1) * BLK, :])
                    else:
                        pv[h] = block_values((n_groups - 1) * U + u - 1, u - 1, pv[h], h)
            qzs_after = select(jnp.minimum(qi + 2, NT - 1))
            outs = []
            for h in range(2):
                l = alpha[h] * mls[h][1] + jnp.sum(part[h], axis=0, keepdims=True)
                outs.append((alpha[h] * accs[h] + pv[h]) / l)
            o_t = jnp.concatenate(outs, axis=0)
            r0 = pl.multiple_of(qi * BLK, BLK)
            o_ref[0, pl.ds(r0, BLK), :] = o_t.T.astype(o_ref.dtype)
            return qzs_after, own_next, tops_next

        last_slot = (n_groups - 1 + phase) % 2
        qzs_after, own_next, tops_next = lax.cond(
            last_slot == 0, functools.partial(boundary, cur=0), functools.partial(boundary, cur=1),
            state)
        return qzs_next, qzs_after, own_next, tops_next, 1 - last_slot

    qzs0 = select(0)
    qzs1 = select(1)
    tops0 = None
    for u in range(U):
        tops0 = block_scores(qzs0, u, u, 0, tops0)
    lax.fori_loop(0, NT, tile, (qzs0, qzs1, own_scores(qzs0, 0), tops0, jnp.int32(0)))


def _moba_call(qt, k, vt, kmean):
    B, S, W = k.shape
    NB = S // MOBA_BLOCK
    QB = MOBA_BLOCK
    n_hp = W // LANES
    assert NB % MOBA_UNROLL == 0 and NB <= LANES
    return pl.pallas_call(
        _moba_kernel,
        grid=(B, n_hp),
        in_specs=[
            pl.BlockSpec((1, NB, LANES, QB), lambda b, hp: (b, 0, hp, 0)),
            pl.BlockSpec((1, S, LANES), lambda b, hp: (b, 0, hp)),
            pl.BlockSpec((1, NB, LANES, MOBA_BLOCK), lambda b, hp: (b, 0, hp, 0)),
            pl.BlockSpec((1, NB, LANES), lambda b, hp: (b, 0, hp)),
        ],
        out_specs=pl.BlockSpec((1, S, LANES), lambda b, hp: (b, 0, hp)),
        out_shape=jax.ShapeDtypeStruct((B, S, W), bf16),
        scratch_shapes=[
            pltpu.VMEM((2, 2, MOBA_UNROLL * MOBA_BLOCK, QB), f32),
            pltpu.VMEM((2, (MOBA_UNROLL + 1) * MOBA_BLOCK, QB), bf16),
        ],
        compiler_params=pltpu.CompilerParams(
            dimension_semantics=("parallel", "parallel"),
            vmem_limit_bytes=VMEM_LIMIT),
        name="moba",
    )(qt, k, vt, kmean)


def _split3(a):
    hi = a.astype(bf16)
    r1 = a - hi.astype(f32)
    mid = r1.astype(bf16)
    lo = (r1 - mid.astype(f32)).astype(bf16)
    return hi, mid, lo


def _gla_kernel(gq_ref, gk_ref, gv_ref, gr_ref, la_ref, gn_ref, y_ref, st_ref):
    C = GLA_CHUNK
    H = GLA_HEADS
    KW = GLA_KWIDTH
    DV = GLA_VAL_DIM
    NSUB = C // GLA_SUB
    T = gq_ref.shape[1]

    @pl.when(pl.program_id(1) == 0)
    def _():
        st_ref[...] = jnp.zeros_like(st_ref)

    row = lax.broadcasted_iota(jnp.int32, (C, C), 0)
    col = lax.broadcasted_iota(jnp.int32, (C, C), 1)
    tril = (col <= row).astype(bf16)
    lane_head = lax.broadcasted_iota(jnp.int32, (C, KW), 1) // GLA_KEY_DIM
    rt = lax.broadcasted_iota(jnp.int32, (C, NSUB * C), 0)
    rc = lax.broadcasted_iota(jnp.int32, (C, NSUB * C), 1)
    keep = jnp.logical_and(rc // C == rt // GLA_SUB, rc % C <= rt)
    st_lane_head = lax.broadcasted_iota(jnp.int32, (DV, KW), 1) // GLA_KEY_DIM

    n_chunks = T // C
    chunk_rows = [slice(c * C, (c + 1) * C) for c in range(n_chunks)]

    Gs = []
    for rows in chunk_rows:
        hi, mid, lo = _split3(la_ref[0, rows, :])
        Gs.append(_dot(tril, hi) + _dot(tril, mid) + _dot(tril, lo))

    rs, upds, qz_sts, decays, vs = [], [], [], [], []
    for rows, G in zip(chunk_rows, Gs):
        q = gq_ref[0, rows, :]
        k = gk_ref[0, rows, :]
        v = gv_ref[0, rows, :].astype(bf16)
        g_last = G[C - 1:C, :]
        g_ref_rows = [G[i * GLA_SUB:i * GLA_SUB + 1, :] for i in range(NSUB)]
        g_own = jnp.concatenate(
            [jnp.broadcast_to(g, (GLA_SUB, KW)) for g in g_ref_rows], axis=0)
        q_in = q * jnp.exp(G - g_own)
        k_in = jnp.concatenate(
            [(k * jnp.exp(jnp.minimum(g - G, EXP_CAP))).astype(bf16) for g in g_ref_rows],
            axis=0)
        q_st = q * jnp.exp(G)
        k_st = (k * jnp.exp(g_last - G)).astype(bf16)
        qz_in = jnp.concatenate(
            [jnp.where(lane_head == h, q_in, 0.0).astype(bf16) for h in range(H)], axis=0)
        qz_sts.append(jnp.concatenate(
            [jnp.where(lane_head == h, q_st, 0.0).astype(bf16) for h in range(H)], axis=0))
        rs.append(_dot_nt(qz_in, k_in))
        upds.append(_dot_tn(v, k_st))
        decays.append(jnp.exp(g_last))
        vs.append(v)

    o_intras = []
    for r, v in zip(rs, vs):
        per_head = []
        for h in range(H):
            r_h = jnp.where(keep, r[h * C:(h + 1) * C, :], 0.0).astype(bf16)
            v_rep = jnp.concatenate([v[:, h * DV:(h + 1) * DV]] * NSUB, axis=0)
            per_head.append(_dot(r_h, v_rep))
        o_intras.append(per_head)

    st = st_ref[...]
    o_inters = []
    for qz_st, upd, decay in zip(qz_sts, upds, decays):
        o_inters.append(_dot_nt(qz_st, st.astype(bf16)))
        st = st * decay
        for h in range(H):
            st = st + jnp.where(st_lane_head == h, upd[h * DV:(h + 1) * DV, :], 0.0)
    st_ref[...] = st

    for rows, o_intra, o_inter in zip(chunk_rows, o_intras, o_inters):
        for h in range(H):
            o = o_intra[h] + o_inter[h * C:(h + 1) * C, :]
            o = o * lax.rsqrt(jnp.mean(o * o, axis=-1, keepdims=True) + EPS)
            o = o * gn_ref[:, h * DV:(h + 1) * DV]
            gr = gr_ref[0, rows, h * DV:(h + 1) * DV]
            y = o * (gr * jax.nn.sigmoid(gr))
            y_ref[0, rows, h * DV:(h + 1) * DV] = y.astype(y_ref.dtype)


def _gla_call(gq, gk, gv, gr, la, gn):
    B, S, _ = gq.shape
    T = GLA_TILE
    tok = lambda w: pl.BlockSpec((1, T, w), lambda b, i: (b, i, 0))
    return pl.pallas_call(
        _gla_kernel,
        grid=(B, S // T),
        in_specs=[tok(GLA_KWIDTH), tok(GLA_KWIDTH), tok(GLA_VWIDTH), tok(GLA_VWIDTH),
                  tok(GLA_KWIDTH), pl.BlockSpec((1, GLA_VWIDTH), lambda b, i: (0, 0))],
        out_specs=tok(GLA_VWIDTH),
        out_shape=jax.ShapeDtypeStruct((B, S, GLA_VWIDTH), bf16),
        scratch_shapes=[pltpu.VMEM((GLA_VAL_DIM, GLA_KWIDTH), f32)],
        compiler_params=pltpu.CompilerParams(
            dimension_semantics=("parallel", "arbitrary"), vmem_limit_bytes=VMEM_LIMIT),
        name="gla",
    )(gq, gk, gv, gr, la, gn)


def _ffn_kernel(x_ref, ym_ref, yg_ref, wo_m_ref, wo_g_ref, fg_ref, wup_ref, cw_ref, cb_ref,
                wdn_ref, og_ref, out_ref, u_ref):
    T = x_ref.shape[1]
    PAD = 8

    @pl.when(pl.program_id(1) == 0)
    def _():
        u_ref[0:PAD, :] = jnp.zeros((PAD, u_ref.shape[1]), f32)

    h = x_ref[0] + _dot(ym_ref[0], wo_m_ref[...]) + _dot(yg_ref[0], wo_g_ref[...])
    hn = _rms(h, fg_ref[...]).astype(bf16)
    u_ref[PAD:PAD + T, :] = _dot(hn, wup_ref[...])
    cw = cw_ref[...]
    conv = (cw[0:1] * u_ref[PAD - 2:PAD - 2 + T, :]
            + cw[1:2] * u_ref[PAD - 1:PAD - 1 + T, :]
            + cw[2:3] * u_ref[PAD:PAD + T, :]
            + cb_ref[...])
    u_ref[0:PAD, :] = u_ref[T:T + PAD, :]
    hg = conv[:, :D_FF]
    act = (hg * jax.nn.sigmoid(hg) * conv[:, D_FF:]).astype(bf16)
    y = h + _dot(act, wdn_ref[...])
    out_ref[0] = _rms(y, og_ref[...])


def _ffn_call(x, ym, yg, wo_m, wo_g, fg, wup, cw, cb, wdn, og):
    B, S, D = x.shape
    T = FFN_TILE
    tok = lambda w: pl.BlockSpec((1, T, w), lambda b, i: (b, i, 0))
    const = lambda a: pl.BlockSpec(a.shape, lambda b, i: (0,) * a.ndim, pipeline_mode=pl.Buffered(1))
    return pl.pallas_call(
        _ffn_kernel,
        grid=(B, S // T),
        in_specs=[tok(D), tok(MOBA_WIDTH), tok(GLA_VWIDTH),
                  const(wo_m), const(wo_g), const(fg), const(wup), const(cw), const(cb),
                  const(wdn), const(og)],
        out_specs=tok(D),
        out_shape=jax.ShapeDtypeStruct((B, S, D), x.dtype),
        scratch_shapes=[pltpu.VMEM((T + 8, 2 * D_FF), f32)],
        compiler_params=pltpu.CompilerParams(
            dimension_semantics=("parallel", "arbitrary"), vmem_limit_bytes=VMEM_LIMIT),
        name="out_ffn",
    )(x, ym, yg, wo_m, wo_g, fg, wup, cw, cb, wdn, og)


def _rope_tables(S):
    hd = MOBA_HEAD_DIM
    inv_freq = 1.0 / (ROPE_THETA ** (jnp.arange(0, hd, 2, dtype=f32) / hd))
    ang = jnp.arange(S).astype(f32)[:, None] * inv_freq[None, :]
    cos, sin = jnp.cos(ang), jnp.sin(ang)
    cos_r = jnp.tile(cos, (1, LANES // (hd // 2)))
    sin_r = jnp.tile(jnp.concatenate([-sin, sin], axis=1), (1, LANES // hd))
    return cos.T, sin.T, cos_r, sin_r


def kernel(x, attn_norm_g, w_in, w_gate_up, b_gate, gla_norm_g, w_out, ffn_norm_g, w_ffn_up,
           conv_w, conv_b, w_ffn_down, final_norm_g):
    B, S, D = x.shape
    l = 0
    o_mk = MOBA_WIDTH
    o_mv = 2 * MOBA_WIDTH
    o_gq = 3 * MOBA_WIDTH
    o_gg = o_gq + 2 * GLA_KWIDTH + 2 * GLA_VWIDTH
    w = w_in[l]
    wqt = w[:, :o_mk].T.astype(bf16)
    wk = w[:, o_mk:o_mv].astype(bf16)
    wvt = jnp.concatenate([w[:, o_mv:o_gq], w[:, o_gg:]], axis=1).T.astype(bf16)
    wg = w[:, o_gq:o_gg].astype(bf16)
    wgu = w_gate_up[l]
    wgu_hi = wgu.astype(bf16)
    wgu_lo = (wgu - wgu_hi.astype(f32)).astype(bf16)
    cos_t, sin_t, cos_r, sin_r = _rope_tables(S)

    qt, k, vt, kmean, gq, gk, gv, gr, la = _proj_call(
        x, attn_norm_g[l][None, :], wqt, wk, wvt, wg, wgu_hi, wgu_lo, b_gate[l][None, :],
        cos_t, sin_t, cos_r, sin_r)
    kmean = kmean.reshape(B, S // MOBA_BLOCK, MOBA_WIDTH)

    y_moba = _moba_call(qt, k, vt, kmean)
    y_gla = _gla_call(gq, gk, gv, gr, la, gla_norm_g[l].reshape(1, GLA_VWIDTH))

    wo = w_out[l].astype(bf16)
    return _ffn_call(
        x, y_moba, y_gla, wo[:MOBA_WIDTH], wo[MOBA_WIDTH:], ffn_norm_g[l][None, :],
        w_ffn_up[l].astype(bf16), conv_w[l], conv_b[l][None, :], w_ffn_down[l].astype(bf16),
        final_norm_g[None, :])
```

```python
import functools
import math

import jax
import jax.numpy as jnp
from jax import lax
from jax.experimental import pallas as pl
from jax.experimental.pallas import tpu as pltpu

D_MODEL = 1024
MOBA_HEAD_DIM = 64
MOBA_HEADS = 8
MOBA_WIDTH = 512
MOBA_BLOCK = 256
MOBA_TOPK = 3
MOBA_UNROLL = 8
MOBA_ROWS = 32
GLA_HEADS = 4
GLA_VAL_DIM = 128
GLA_KEY_DIM = 64
GLA_VWIDTH = 512
GLA_KWIDTH = 256
GLA_GATE_RANK = 16
GLA_GATE_TAU = 16.0
GLA_CHUNK = 64
GLA_SUB = 16
D_FF = 2816
CONV_WIDTH = 3
ROPE_THETA = 10000.0
EPS = 1e-6

LANES = 128
LOG2_E = 1.4426950408889634
NEG_BIG = -1e30
EXP_CAP = 80.0

PROJ_TILE = 1024
GLA_TILE = 512
FFN_TILE = 512
VMEM_LIMIT = 56 * 1024 * 1024

f32 = jnp.float32
bf16 = jnp.bfloat16


def _dot(a, b):
    return jnp.dot(a, b, preferred_element_type=f32)


def _dot_nt(a, b):
    return lax.dot_general(a, b, (((1,), (1,)), ((), ())), preferred_element_type=f32)


def _dot_tn(a, b):
    return lax.dot_general(a, b, (((0,), (0,)), ((), ())), preferred_element_type=f32)


def _rms(xf, g):
    return xf * lax.rsqrt(jnp.mean(xf * xf, axis=-1, keepdims=True) + EPS) * g


def _proj_kernel(x_ref, g_ref, wqt_ref, wk_ref, wvt_ref, wg_ref, wgu_hi_ref, wgu_lo_ref, bg_ref,
                 cos_t_ref, sin_t_ref, cos_r_ref, sin_r_ref,
                 qt_ref, k_ref, vt_ref, kmean_ref, gq_ref, gk_ref, gv_ref, gr_ref, la_ref):
    T = x_ref.shape[1]
    xn = _rms(x_ref[0], g_ref[...]).astype(bf16)

    nblk = T // MOBA_BLOCK
    vg = _dot_nt(wvt_ref[...], xn)
    vt = vg[:MOBA_WIDTH].astype(bf16)
    for j in range(nblk):
        vt_ref[0, j] = vt[:, j * MOBA_BLOCK:(j + 1) * MOBA_BLOCK]

    gate_lr = vg[MOBA_WIDTH:]
    lr_hi = gate_lr.astype(bf16)
    lr_lo = (gate_lr - lr_hi.astype(f32)).astype(bf16)
    z = (_dot_tn(lr_hi, wgu_hi_ref[...]) + _dot_tn(lr_lo, wgu_hi_ref[...])
         + _dot_tn(lr_hi, wgu_lo_ref[...]) + bg_ref[...])
    log_sig = jnp.minimum(z, 0.0) - jnp.log1p(jnp.exp(-jnp.abs(z)))
    la_ref[0] = log_sig * (1.0 / GLA_GATE_TAU)

    qt = _dot_nt(wqt_ref[...], xn)
    cos_t = cos_t_ref[...]
    sin_t = sin_t_ref[...]
    half = MOBA_HEAD_DIM // 2
    scale = MOBA_HEAD_DIM ** -0.5 * LOG2_E
    for h in range(MOBA_HEADS):
        r0 = h * MOBA_HEAD_DIM
        t1 = qt[r0:r0 + half]
        t2 = qt[r0 + half:r0 + MOBA_HEAD_DIM]
        lo = ((t1 * cos_t - t2 * sin_t) * scale).astype(bf16)
        hi = ((t2 * cos_t + t1 * sin_t) * scale).astype(bf16)
        for j in range(nblk):
            qt_ref[0, j, r0:r0 + half, :] = lo[:, j * MOBA_BLOCK:(j + 1) * MOBA_BLOCK]
            qt_ref[0, j, r0 + half:r0 + MOBA_HEAD_DIM, :] = hi[:, j * MOBA_BLOCK:(j + 1) * MOBA_BLOCK]

    k = _dot(xn, wk_ref[...])
    cos_r = cos_r_ref[...]
    sin_r = sin_r_ref[...]
    lane = lax.broadcasted_iota(jnp.int32, (T, LANES), 1)
    first_half = (lane % MOBA_HEAD_DIM) < half
    for p in range(MOBA_WIDTH // LANES):
        kp = k[:, p * LANES:(p + 1) * LANES]
        rot = jnp.where(first_half, pltpu.roll(kp, LANES - half, 1), pltpu.roll(kp, half, 1))
        kr = kp * cos_r + rot * sin_r
        k_ref[0, :, p * LANES:(p + 1) * LANES] = kr.astype(bf16)
        for j in range(nblk):
            kmean_ref[0, 0, j:j + 1, p * LANES:(p + 1) * LANES] = jnp.mean(
                kr[j * MOBA_BLOCK:(j + 1) * MOBA_BLOCK], axis=0, keepdims=True)

    pg = _dot(xn, wg_ref[...])
    o_gk = GLA_KWIDTH
    o_gv = o_gk + GLA_KWIDTH
    o_gr = o_gv + GLA_VWIDTH
    gq_ref[0] = pg[:, :o_gk] * (GLA_KEY_DIM ** -0.5)
    gk_ref[0] = pg[:, o_gk:o_gv]
    gv_ref[0] = pg[:, o_gv:o_gr]
    gr_ref[0] = pg[:, o_gr:]


def _proj_call(x, g, wqt, wk, wvt, wg, wgu_hi, wgu_lo, bg, cos_t, sin_t, cos_r, sin_r):
    B, S, D = x.shape
    T = PROJ_TILE
    nb_t = T // MOBA_BLOCK
    const = lambda shape: pl.BlockSpec(shape, lambda b, i: (0,) * len(shape))
    tok = lambda w: pl.BlockSpec((1, T, w), lambda b, i: (b, i, 0))
    out_shape = (
        jax.ShapeDtypeStruct((B, S // MOBA_BLOCK, MOBA_WIDTH, MOBA_BLOCK), bf16),
        jax.ShapeDtypeStruct((B, S, MOBA_WIDTH), bf16),
        jax.ShapeDtypeStruct((B, S // MOBA_BLOCK, MOBA_WIDTH, MOBA_BLOCK), bf16),
        jax.ShapeDtypeStruct((B, S // T, nb_t, MOBA_WIDTH), f32),
        jax.ShapeDtypeStruct((B, S, GLA_KWIDTH), f32),
        jax.ShapeDtypeStruct((B, S, GLA_KWIDTH), f32),
        jax.ShapeDtypeStruct((B, S, GLA_VWIDTH), f32),
        jax.ShapeDtypeStruct((B, S, GLA_VWIDTH), f32),
        jax.ShapeDtypeStruct((B, S, GLA_KWIDTH), f32),
    )
    out_specs = (
        pl.BlockSpec((1, nb_t, MOBA_WIDTH, MOBA_BLOCK), lambda b, i: (b, i, 0, 0)),
        tok(MOBA_WIDTH),
        pl.BlockSpec((1, nb_t, MOBA_WIDTH, MOBA_BLOCK), lambda b, i: (b, i, 0, 0)),
        pl.BlockSpec((1, 1, nb_t, MOBA_WIDTH), lambda b, i: (b, i, 0, 0)),
        tok(GLA_KWIDTH), tok(GLA_KWIDTH), tok(GLA_VWIDTH), tok(GLA_VWIDTH), tok(GLA_KWIDTH),
    )
    in_specs = [
        tok(D),
        const((1, D)),
        const(wqt.shape), const(wk.shape), const(wvt.shape), const(wg.shape),
        const(wgu_hi.shape), const(wgu_lo.shape), const((1, GLA_KWIDTH)),
        pl.BlockSpec((MOBA_HEAD_DIM // 2, T), lambda b, i: (0, i)),
        pl.BlockSpec((MOBA_HEAD_DIM // 2, T), lambda b, i: (0, i)),
        pl.BlockSpec((T, LANES), lambda b, i: (i, 0)),
        pl.BlockSpec((T, LANES), lambda b, i: (i, 0)),
    ]
    return pl.pallas_call(
        _proj_kernel,
        grid=(B, S // T),
        in_specs=in_specs,
        out_specs=out_specs,
        out_shape=out_shape,
        compiler_params=pltpu.CompilerParams(
            dimension_semantics=("parallel", "parallel"), vmem_limit_bytes=VMEM_LIMIT),
        name="in_proj",
    )(x, g, wqt, wk, wvt, wg, wgu_hi, wgu_lo, bg, cos_t, sin_t, cos_r, sin_r)


def _moba_kernel(qt_ref, k_ref, vt_ref, kmean_ref, o_ref, s_ref, p_ref):
    NT = qt_ref.shape[1]
    QB = qt_ref.shape[3]
    NB = kmean_ref.shape[1]
    HD = MOBA_HEAD_DIM
    U = MOBA_UNROLL
    BLK = MOBA_BLOCK
    ROWS = MOBA_ROWS

    km = kmean_ref[0]
    km_hi = km.astype(bf16)
    km_lo = (km - km_hi.astype(f32)).astype(bf16)
    blk = lax.broadcasted_iota(jnp.int32, (NB, QB), 0)
    zeros_half = jnp.zeros((HD, QB), bf16)
    zeros_tail = jnp.zeros((LANES - NB, QB), bf16)
    in_causal = (lax.broadcasted_iota(jnp.int32, (BLK, QB), 0)
                 <= lax.broadcasted_iota(jnp.int32, (BLK, QB), 1))
    lane8 = lax.broadcasted_iota(jnp.int32, (8, LANES), 1)

    def select(qi):
        qzs = []
        for h in range(2):
            q_h = qt_ref[0, qi, h * HD:(h + 1) * HD, :]
            qz = jnp.concatenate([q_h, zeros_half] if h == 0 else [zeros_half, q_h], axis=0)
            gate = _dot(km_hi, qz) + _dot(km_lo, qz)
            gate = jnp.where(blk < qi, gate, -jnp.inf)
            sel = jnp.zeros((NB, QB), jnp.bool_)
            for _ in range(MOBA_TOPK):
                top = jnp.max(gate, axis=0, keepdims=True)
                idx = jnp.min(jnp.where(gate == top, blk, NB), axis=0, keepdims=True)
                pick = blk == idx
                sel = jnp.logical_or(sel, jnp.logical_and(pick, top > -jnp.inf))
                gate = jnp.where(pick, -jnp.inf, gate)
            bias = jnp.where(sel, 0.0, NEG_BIG).astype(bf16)
            qzs.append(jnp.concatenate([qz, bias, zeros_tail], axis=0))
        return tuple(qzs)

    def block_scores(qzs, j, u, slot, tops):
        r0 = pl.multiple_of(j * BLK, BLK)
        onehot = jnp.tile(jnp.where(lane8 == j, 1.0, 0.0), (BLK // 8, 1)).astype(bf16)
        k_aug = jnp.concatenate([k_ref[0, pl.ds(r0, BLK), :], onehot], axis=1)
        out = []
        for h in range(2):
            s = _dot(k_aug, qzs[h])
            s_ref[slot, h, u * BLK:(u + 1) * BLK, :] = s
            top = jnp.max(s, axis=0, keepdims=True)
            out.append(top if tops is None else jnp.maximum(tops[h], top))
        return tuple(out)

    def own_scores(qzs, qi):
        k_own = k_ref[0, pl.ds(pl.multiple_of(qi * BLK, BLK), BLK), :]
        out = []
        for h in range(2):
            s = jnp.where(in_causal, _dot(k_own, qzs[h][:LANES]), NEG_BIG)
            out.append((s, jnp.max(s, axis=0, keepdims=True)))
        return tuple(out)

    def block_softmax(load_rows, u, m_new, part, h):
        for r in range(u * BLK, (u + 1) * BLK, ROWS):
            e = jnp.exp2(load_rows(r) - m_new)
            part = part + e
            p_ref[h, r:r + ROWS, :] = e.astype(bf16)
        return part

    def block_values(j, u, pv, h):
        d = _dot(vt_ref[0, j, h * HD:(h + 1) * HD, :], p_ref[h, u * BLK:(u + 1) * BLK, :])
        return d if pv is None else pv + d

    def group(t, tops, mls, accs, load_rows, prefetch):
        m_new = [jnp.maximum(mls[h][0], tops[h]) for h in range(2)]
        alpha = [jnp.exp2(mls[h][0] - m_new[h]) for h in range(2)]
        part = [jnp.zeros((ROWS, QB), f32)] * 2
        pv = [None, None]
        tops_next = None
        for u in range(U):
            tops_next = prefetch(u, tops_next)
            for h in range(2):
                part[h] = block_softmax(load_rows[h], u, m_new[h], part[h], h)
            for h in range(2):
                pv[h] = block_values(t * U + u, u, pv[h], h)
        mls = tuple((m_new[h], alpha[h] * mls[h][1] + jnp.sum(part[h], axis=0, keepdims=True))
                    for h in range(2))
        accs = tuple(alpha[h] * accs[h] + pv[h] for h in range(2))
        return tops_next, mls, accs

    def slot_rows(slot):
        return [lambda r, h=h: s_ref[slot, h, r:r + ROWS, :] for h in range(2)]

    ml0 = (jnp.full((1, QB), -jnp.inf, f32), jnp.zeros((1, QB), f32))
    acc0 = jnp.zeros((HD, QB), f32)

    def tile(qi, carry):
        qzs, qzs_next, own, tops0, phase = carry
        n_groups = jnp.maximum((qi + U - 1) // U, 1)

        def step(t, c, cur):
            tops, mls, accs = c
            return group(t, tops, mls, accs, slot_rows(cur),
                         lambda u, tn: block_scores(qzs, (t + 1) * U + u, u, 1 - cur, tn))

        def body(t, c):
            return lax.cond((t + phase) % 2 == 0,
                            functools.partial(step, t, cur=0), functools.partial(step, t, cur=1), c)

        state = lax.fori_loop(0, n_groups - 1, body, (tops0, (ml0, ml0), (acc0, acc0)))

        def boundary(c, cur):
            tops, mls, accs = c
            m_new = [jnp.maximum(mls[h][0], jnp.maximum(tops[h], own[h][1])) for h in range(2)]
            alpha = [jnp.exp2(mls[h][0] - m_new[h]) for h in range(2)]
            part = [jnp.zeros((ROWS, QB), f32)] * 2
            pv = [None, None]
            tops_next = None
            load_rows = slot_rows(cur)
            q_next = jnp.minimum(qi + 1, NT - 1)
            for u in range(U + 1):
                if u < U:
                    tops_next = block_scores(qzs_next, u, u, 1 - cur, tops_next)
                else:
                    own_next = own_scores(qzs_next, q_next)
                for h in range(2):
                    if u == 0:
                        for r in range(0, BLK, ROWS):
                            e = jnp.exp2(own[h][0][r:r + ROWS] - m_new[h])
                            part[h] = part[h] + e
                            p_ref[h, U * BLK + r:U * BLK + r + ROWS, :] = e.astype(bf16)
                    else:
                        part[h] = block_softmax(load_rows[h], u - 1, m_new[h], part[h], h)
                for h in range(2):
                    if u == 0:
                        pv[h] = _dot(vt_ref[0, qi, h * HD:(h + 1) * HD, :],
                                     p_ref[h, U * BLK:(U + 1) * BLK, :])
                    else:
                        pv[h] = block_values((n_groups - 1) * U + u - 1, u - 1, pv[h], h)
            qzs_after = select(jnp.minimum(qi + 2, NT - 1))
            outs = []
            for h in range(2):
                l = alpha[h] * mls[h][1] + jnp.sum(part[h], axis=0, keepdims=True)
                outs.append((alpha[h] * accs[h] + pv[h]) / l)
            o_t = jnp.concatenate(outs, axis=0)
            r0 = pl.multiple_of(qi * BLK, BLK)
            o_ref[0, pl.ds(r0, BLK), :] = o_t.T.astype(o_ref.dtype)
            return qzs_after, own_next, tops_next

        last_slot = (n_groups - 1 + phase) % 2
        qzs_after, own_next, tops_next = lax.cond(
            last_slot == 0, functools.partial(boundary, cur=0), functools.partial(boundary, cur=1),
            state)
        return qzs_next, qzs_after, own_next, tops_next, 1 - last_slot

    qzs0 = select(0)
    qzs1 = select(1)
    tops0 = None
    for u in range(U):
        tops0 = block_scores(qzs0, u, u, 0, tops0)
    lax.fori_loop(0, NT, tile, (qzs0, qzs1, own_scores(qzs0, 0), tops0, jnp.int32(0)))


def _moba_call(qt, k, vt, kmean):
    B, S, W = k.shape
    NB = S // MOBA_BLOCK
    QB = MOBA_BLOCK
    n_hp = W // LANES
    assert NB % MOBA_UNROLL == 0 and NB <= LANES
    return pl.pallas_call(
        _moba_kernel,
        grid=(B, n_hp),
        in_specs=[
            pl.BlockSpec((1, NB, LANES, QB), lambda b, hp: (b, 0, hp, 0)),
            pl.BlockSpec((1, S, LANES), lambda b, hp: (b, 0, hp)),
            pl.BlockSpec((1, NB, LANES, MOBA_BLOCK), lambda b, hp: (b, 0, hp, 0)),
            pl.BlockSpec((1, NB, LANES), lambda b, hp: (b, 0, hp)),
        ],
        out_specs=pl.BlockSpec((1, S, LANES), lambda b, hp: (b, 0, hp)),
        out_shape=jax.ShapeDtypeStruct((B, S, W), bf16),
        scratch_shapes=[
            pltpu.VMEM((2, 2, MOBA_UNROLL * MOBA_BLOCK, QB), f32),
            pltpu.VMEM((2, (MOBA_UNROLL + 1) * MOBA_BLOCK, QB), bf16),
        ],
        compiler_params=pltpu.CompilerParams(
            dimension_semantics=("parallel", "parallel"),
            vmem_limit_bytes=VMEM_LIMIT),
        name="moba",
    )(qt, k, vt, kmean)


def _split3(a):
    hi = a.astype(bf16)
    r1 = a - hi.astype(f32)
    mid = r1.astype(bf16)
    lo = (r1 - mid.astype(f32)).astype(bf16)
    return hi, mid, lo


def _gla_kernel(gq_ref, gk_ref, gv_ref, gr_ref, la_ref, gn_ref, y_ref, st_ref):
    C = GLA_CHUNK
    H = GLA_HEADS
    KW = GLA_KWIDTH
    DV = GLA_VAL_DIM
    NSUB = C // GLA_SUB
    T = gq_ref.shape[1]

    @pl.when(pl.program_id(1) == 0)
    def _():
        st_ref[...] = jnp.zeros_like(st_ref)

    row = lax.broadcasted_iota(jnp.int32, (C, C), 0)
    col = lax.broadcasted_iota(jnp.int32, (C, C), 1)
    tril = (col <= row).astype(bf16)
    lane_head = lax.broadcasted_iota(jnp.int32, (C, KW), 1) // GLA_KEY_DIM
    rt = lax.broadcasted_iota(jnp.int32, (C, NSUB * C), 0)
    rc = lax.broadcasted_iota(jnp.int32, (C, NSUB * C), 1)
    keep = jnp.logical_and(rc // C == rt // GLA_SUB, rc % C <= rt)
    st_lane_head = lax.broadcasted_iota(jnp.int32, (DV, KW), 1) // GLA_KEY_DIM

    n_chunks = T // C
    chunk_rows = [slice(c * C, (c + 1) * C) for c in range(n_chunks)]

    Gs = []
    for rows in chunk_rows:
        hi, mid, lo = _split3(la_ref[0, rows, :])
        Gs.append(_dot(tril, hi) + _dot(tril, mid) + _dot(tril, lo))

    rs, upds, qz_sts, decays, vs = [], [], [], [], []
    for rows, G in zip(chunk_rows, Gs):
        q = gq_ref[0, rows, :]
        k = gk_ref[0, rows, :]
        v = gv_ref[0, rows, :].astype(bf16)
        g_last = G[C - 1:C, :]
        g_ref_rows = [G[i * GLA_SUB:i * GLA_SUB + 1, :] for i in range(NSUB)]
        g_own = jnp.concatenate(
            [jnp.broadcast_to(g, (GLA_SUB, KW)) for g in g_ref_rows], axis=0)
        q_in = q * jnp.exp(G - g_own)
        k_in = jnp.concatenate(
            [(k * jnp.exp(jnp.minimum(g - G, EXP_CAP))).astype(bf16) for g in g_ref_rows],
            axis=0)
        q_st = q * jnp.exp(G)
        k_st = (k * jnp.exp(g_last - G)).astype(bf16)
        qz_in = jnp.concatenate(
            [jnp.where(lane_head == h, q_in, 0.0).astype(bf16) for h in range(H)], axis=0)
        qz_sts.append(jnp.concatenate(
            [jnp.where(lane_head == h, q_st, 0.0).astype(bf16) for h in range(H)], axis=0))
        rs.append(_dot_nt(qz_in, k_in))
        upds.append(_dot_tn(v, k_st))
        decays.append(jnp.exp(g_last))
        vs.append(v)

    o_intras = []
    for r, v in zip(rs, vs):
        per_head = []
        for h in range(H):
            r_h = jnp.where(keep, r[h * C:(h + 1) * C, :], 0.0).astype(bf16)
            v_rep = jnp.concatenate([v[:, h * DV:(h + 1) * DV]] * NSUB, axis=0)
            per_head.append(_dot(r_h, v_rep))
        o_intras.append(per_head)

    st = st_ref[...]
    o_inters = []
    for qz_st, upd, decay in zip(qz_sts, upds, decays):
        o_inters.append(_dot_nt(qz_st, st.astype(bf16)))
        st = st * decay
        for h in range(H):
            st = st + jnp.where(st_lane_head == h, upd[h * DV:(h + 1) * DV, :], 0.0)
    st_ref[...] = st

    for rows, o_intra, o_inter in zip(chunk_rows, o_intras, o_inters):
        for h in range(H):
            o = o_intra[h] + o_inter[h * C:(h + 1) * C, :]
            o = o * lax.rsqrt(jnp.mean(o * o, axis=-1, keepdims=True) + EPS)
            o = o * gn_ref[:, h * DV:(h + 1) * DV]
            gr = gr_ref[0, rows, h * DV:(h + 1) * DV]
            y = o * (gr * jax.nn.sigmoid(gr))
            y_ref[0, rows, h * DV:(h + 1) * DV] = y.astype(y_ref.dtype)


def _gla_call(gq, gk, gv, gr, la, gn):
    B, S, _ = gq.shape
    T = GLA_TILE
    tok = lambda w: pl.BlockSpec((1, T, w), lambda b, i: (b, i, 0))
    return pl.pallas_call(
        _gla_kernel,
        grid=(B, S // T),
        in_specs=[tok(GLA_KWIDTH), tok(GLA_KWIDTH), tok(GLA_VWIDTH), tok(GLA_VWIDTH),
                  tok(GLA_KWIDTH), pl.BlockSpec((1, GLA_VWIDTH), lambda b, i: (0, 0))],
        out_specs=tok(GLA_VWIDTH),
        out_shape=jax.ShapeDtypeStruct((B, S, GLA_VWIDTH), bf16),
        scratch_shapes=[pltpu.VMEM((GLA_VAL_DIM, GLA_KWIDTH), f32)],
        compiler_params=pltpu.CompilerParams(
            dimension_semantics=("parallel", "arbitrary"), vmem_limit_bytes=VMEM_LIMIT),
        name="gla",
    )(gq, gk, gv, gr, la, gn)


def _ffn_kernel(x_ref, ym_ref, yg_ref, wo_m_ref, wo_g_ref, fg_ref, wup_ref, cw_ref, cb_ref,
                wdn_ref, og_ref, out_ref, u_ref):
    T = x_ref.shape[1]
    PAD = 8

    @pl.when(pl.program_id(1) == 0)
    def _():
        u_ref[0:PAD, :] = jnp.zeros((PAD, u_ref.shape[1]), f32)

    h = x_ref[0] + _dot(ym_ref[0], wo_m_ref[...]) + _dot(yg_ref[0], wo_g_ref[...])
    hn = _rms(h, fg_ref[...]).astype(bf16)
    u_ref[PAD:PAD + T, :] = _dot(hn, wup_ref[...])
    cw = cw_ref[...]
    conv = (cw[0:1] * u_ref[PAD - 2:PAD - 2 + T, :]
            + cw[1:2] * u_ref[PAD - 1:PAD - 1 + T, :]
            + cw[2:3] * u_ref[PAD:PAD + T, :]
            + cb_ref[...])
    u_ref[0:PAD, :] = u_ref[T:T + PAD, :]
    hg = conv[:, :D_FF]
    act = (hg * jax.nn.sigmoid(hg) * conv[:, D_FF:]).astype(bf16)
    y = h + _dot(act, wdn_ref[...])
    out_ref[0] = _rms(y, og_ref[...])


def _ffn_call(x, ym, yg, wo_m, wo_g, fg, wup, cw, cb, wdn, og):
    B, S, D = x.shape
    T = FFN_TILE
    tok = lambda w: pl.BlockSpec((1, T, w), lambda b, i: (b, i, 0))
    const = lambda a: pl.BlockSpec(a.shape, lambda b, i: (0,) * a.ndim, pipeline_mode=pl.Buffered(1))
    return pl.pallas_call(
        _ffn_kernel,
        grid=(B, S // T),
        in_specs=[tok(D), tok(MOBA_WIDTH), tok(GLA_VWIDTH),
                  const(wo_m), const(wo_g), const(fg), const(wup), const(cw), const(cb),
                  const(wdn), const(og)],
        out_specs=tok(D),
        out_shape=jax.ShapeDtypeStruct((B, S, D), x.dtype),
        scratch_shapes=[pltpu.VMEM((T + 8, 2 * D_FF), f32)],
        compiler_params=pltpu.CompilerParams(
            dimension_semantics=("parallel", "arbitrary"), vmem_limit_bytes=VMEM_LIMIT),
        name="out_ffn",
    )(x, ym, yg, wo_m, wo_g, fg, wup, cw, cb, wdn, og)


def _rope_tables(S):
    hd = MOBA_HEAD_DIM
    inv_freq = 1.0 / (ROPE_THETA ** (jnp.arange(0, hd, 2, dtype=f32) / hd))
    ang = jnp.arange(S).astype(f32)[:, None] * inv_freq[None, :]
    cos, sin = jnp.cos(ang), jnp.sin(ang)
    cos_r = jnp.tile(cos, (1, LANES // (hd // 2)))
    sin_r = jnp.tile(jnp.concatenate([-sin, sin], axis=1), (1, LANES // hd))
    return cos.T, sin.T, cos_r, sin_r


def kernel(x, attn_norm_g, w_in, w_gate_up, b_gate, gla_norm_g, w_out, ffn_norm_g, w_ffn_up,
           conv_w, conv_b, w_ffn_down, final_norm_g):
    B, S, D = x.shape
    l = 0
    o_mk = MOBA_WIDTH
    o_mv = 2 * MOBA_WIDTH
    o_gq = 3 * MOBA_WIDTH
    o_gg = o_gq + 2 * GLA_KWIDTH + 2 * GLA_VWIDTH
    w = w_in[l]
    wqt = w[:, :o_mk].T.astype(bf16)
    wk = w[:, o_mk:o_mv].astype(bf16)
    wvt = jnp.concatenate([w[:, o_mv:o_gq], w[:, o_gg:]], axis=1).T.astype(bf16)
    wg = w[:, o_gq:o_gg].astype(bf16)
    wgu = w_gate_up[l]
    wgu_hi = wgu.astype(bf16)
    wgu_lo = (wgu - wgu_hi.astype(f32)).astype(bf16)
    cos_t, sin_t, cos_r, sin_r = _rope_tables(S)

    qt, k, vt, kmean, gq, gk, gv, gr, la = _proj_call(
        x, attn_norm_g[l][None, :], wqt, wk, wvt, wg, wgu_hi, wgu_lo, b_gate[l][None, :],
        cos_t, sin_t, cos_r, sin_r)
    kmean = kmean.reshape(B, S // MOBA_BLOCK, MOBA_WIDTH)

    y_moba = _moba_call(qt, k, vt, kmean)
    y_gla = _gla_call(gq, gk, gv, gr, la, gla_norm_g[l].reshape(1, GLA_VWIDTH))

    wo = w_out[l].astype(bf16)
    return _ffn_call(
        x, y_moba, y_gla, wo[:MOBA_WIDTH], wo[MOBA_WIDTH:], ffn_norm_g[l][None, :],
        w_ffn_up[l].astype(bf16), conv_w[l], conv_b[l][None, :], w_ffn_down[l].astype(bf16),
        final_norm_g[None, :])
```

```python
import functools
import math

import jax
import jax.numpy as jnp
from jax import lax
from jax.experimental import pallas as pl
from jax.experimental.pallas import tpu as pltpu

D_MODEL = 1024
MOBA_HEAD_DIM = 64
MOBA_HEADS = 8
MOBA_WIDTH = 512
MOBA_BLOCK = 256
MOBA_TOPK = 3
MOBA_UNROLL = 8
MOBA_ROWS = 32
GLA_HEADS = 4
GLA_VAL_DIM = 128
GLA_KEY_DIM = 64
GLA_VWIDTH = 512
GLA_KWIDTH = 256
GLA_GATE_RANK = 16
GLA_GATE_TAU = 16.0
GLA_CHUNK = 64
GLA_SUB = 16
D_FF = 2816
CONV_WIDTH = 3
ROPE_THETA = 10000.0
EPS = 1e-6

LANES = 128
LOG2_E = 1.4426950408889634
NEG_BIG = -1e30
EXP_CAP = 80.0

PROJ_TILE = 1024
GLA_TILE = 1024
FFN_TILE = 512
VMEM_LIMIT = 56 * 1024 * 1024

f32 = jnp.float32
bf16 = jnp.bfloat16


def _dot(a, b):
    return jnp.dot(a, b, preferred_element_type=f32)


def _dot_nt(a, b):
    return lax.dot_general(a, b, (((1,), (1,)), ((), ())), preferred_element_type=f32)


def _dot_tn(a, b):
    return lax.dot_general(a, b, (((0,), (0,)), ((), ())), preferred_element_type=f32)


def _rms(xf, g):
    return xf * lax.rsqrt(jnp.mean(xf * xf, axis=-1, keepdims=True) + EPS) * g


def _proj_kernel(x_ref, g_ref, wqt_ref, wk_ref, wvt_ref, wg_ref, wgu_hi_ref, wgu_lo_ref, bg_ref,
                 cos_t_ref, sin_t_ref, cos_r_ref, sin_r_ref,
                 qt_ref, k_ref, vt_ref, kmean_ref, gq_ref, gk_ref, gv_ref, gr_ref, la_ref):
    T = x_ref.shape[1]
    xn = _rms(x_ref[0], g_ref[...]).astype(bf16)

    nblk = T // MOBA_BLOCK
    vg = _dot_nt(wvt_ref[...], xn)
    vt = vg[:MOBA_WIDTH].astype(bf16)
    for j in range(nblk):
        vt_ref[0, j] = vt[:, j * MOBA_BLOCK:(j + 1) * MOBA_BLOCK]

    gate_lr = vg[MOBA_WIDTH:]
    lr_hi = gate_lr.astype(bf16)
    lr_lo = (gate_lr - lr_hi.astype(f32)).astype(bf16)
    z = (_dot_tn(lr_hi, wgu_hi_ref[...]) + _dot_tn(lr_lo, wgu_hi_ref[...])
         + _dot_tn(lr_hi, wgu_lo_ref[...]) + bg_ref[...])
    log_sig = jnp.minimum(z, 0.0) - jnp.log1p(jnp.exp(-jnp.abs(z)))
    la_ref[0] = log_sig * (1.0 / GLA_GATE_TAU)

    qt = _dot_nt(wqt_ref[...], xn)
    cos_t = cos_t_ref[...]
    sin_t = sin_t_ref[...]
    half = MOBA_HEAD_DIM // 2
    scale = MOBA_HEAD_DIM ** -0.5 * LOG2_E
    for h in range(MOBA_HEADS):
        r0 = h * MOBA_HEAD_DIM
        t1 = qt[r0:r0 + half]
        t2 = qt[r0 + half:r0 + MOBA_HEAD_DIM]
        lo = ((t1 * cos_t - t2 * sin_t) * scale).astype(bf16)
        hi = ((t2 * cos_t + t1 * sin_t) * scale).astype(bf16)
        for j in range(nblk):
            qt_ref[0, j, r0:r0 + half, :] = lo[:, j * MOBA_BLOCK:(j + 1) * MOBA_BLOCK]
            qt_ref[0, j, r0 + half:r0 + MOBA_HEAD_DIM, :] = hi[:, j * MOBA_BLOCK:(j + 1) * MOBA_BLOCK]

    k = _dot(xn, wk_ref[...])
    cos_r = cos_r_ref[...]
    sin_r = sin_r_ref[...]
    lane = lax.broadcasted_iota(jnp.int32, (T, LANES), 1)
    first_half = (lane % MOBA_HEAD_DIM) < half
    for p in range(MOBA_WIDTH // LANES):
        kp = k[:, p * LANES:(p + 1) * LANES]
        rot = jnp.where(first_half, pltpu.roll(kp, LANES - half, 1), pltpu.roll(kp, half, 1))
        kr = kp * cos_r + rot * sin_r
        k_ref[0, :, p * LANES:(p + 1) * LANES] = kr.astype(bf16)
        for j in range(nblk):
            kmean_ref[0, 0, j:j + 1, p * LANES:(p + 1) * LANES] = jnp.mean(
                kr[j * MOBA_BLOCK:(j + 1) * MOBA_BLOCK], axis=0, keepdims=True)

    pg = _dot(xn, wg_ref[...])
    o_gk = GLA_KWIDTH
    o_gv = o_gk + GLA_KWIDTH
    o_gr = o_gv + GLA_VWIDTH
    gq_ref[0] = pg[:, :o_gk] * (GLA_KEY_DIM ** -0.5)
    gk_ref[0] = pg[:, o_gk:o_gv]
    gv_ref[0] = pg[:, o_gv:o_gr]
    gr_ref[0] = pg[:, o_gr:]


def _proj_call(x, g, wqt, wk, wvt, wg, wgu_hi, wgu_lo, bg, cos_t, sin_t, cos_r, sin_r):
    B, S, D = x.shape
    T = PROJ_TILE
    nb_t = T // MOBA_BLOCK
    const = lambda shape: pl.BlockSpec(shape, lambda b, i: (0,) * len(shape))
    tok = lambda w: pl.BlockSpec((1, T, w), lambda b, i: (b, i, 0))
    out_shape = (
        jax.ShapeDtypeStruct((B, S // MOBA_BLOCK, MOBA_WIDTH, MOBA_BLOCK), bf16),
        jax.ShapeDtypeStruct((B, S, MOBA_WIDTH), bf16),
        jax.ShapeDtypeStruct((B, S // MOBA_BLOCK, MOBA_WIDTH, MOBA_BLOCK), bf16),
        jax.ShapeDtypeStruct((B, S // T, nb_t, MOBA_WIDTH), f32),
        jax.ShapeDtypeStruct((B, S, GLA_KWIDTH), f32),
        jax.ShapeDtypeStruct((B, S, GLA_KWIDTH), f32),
        jax.ShapeDtypeStruct((B, S, GLA_VWIDTH), f32),
        jax.ShapeDtypeStruct((B, S, GLA_VWIDTH), f32),
        jax.ShapeDtypeStruct((B, S, GLA_KWIDTH), f32),
    )
    out_specs = (
        pl.BlockSpec((1, nb_t, MOBA_WIDTH, MOBA_BLOCK), lambda b, i: (b, i, 0, 0)),
        tok(MOBA_WIDTH),
        pl.BlockSpec((1, nb_t, MOBA_WIDTH, MOBA_BLOCK), lambda b, i: (b, i, 0, 0)),
        pl.BlockSpec((1, 1, nb_t, MOBA_WIDTH), lambda b, i: (b, i, 0, 0)),
        tok(GLA_KWIDTH), tok(GLA_KWIDTH), tok(GLA_VWIDTH), tok(GLA_VWIDTH), tok(GLA_KWIDTH),
    )
    in_specs = [
        tok(D),
        const((1, D)),
        const(wqt.shape), const(wk.shape), const(wvt.shape), const(wg.shape),
        const(wgu_hi.shape), const(wgu_lo.shape), const((1, GLA_KWIDTH)),
        pl.BlockSpec((MOBA_HEAD_DIM // 2, T), lambda b, i: (0, i)),
        pl.BlockSpec((MOBA_HEAD_DIM // 2, T), lambda b, i: (0, i)),
        pl.BlockSpec((T, LANES), lambda b, i: (i, 0)),
        pl.BlockSpec((T, LANES), lambda b, i: (i, 0)),
    ]
    return pl.pallas_call(
        _proj_kernel,
        grid=(B, S // T),
        in_specs=in_specs,
        out_specs=out_specs,
        out_shape=out_shape,
        compiler_params=pltpu.CompilerParams(
            dimension_semantics=("parallel", "parallel"), vmem_limit_bytes=VMEM_LIMIT),
        name="in_proj",
    )(x, g, wqt, wk, wvt, wg, wgu_hi, wgu_lo, bg, cos_t, sin_t, cos_r, sin_r)


def _moba_kernel(qt_ref, k_ref, vt_ref, kmean_ref, o_ref, s_ref, p_ref):
    NT = qt_ref.shape[1]
    QB = qt_ref.shape[3]
    NB = kmean_ref.shape[1]
    HD = MOBA_HEAD_DIM
    U = MOBA_UNROLL
    BLK = MOBA_BLOCK
    ROWS = MOBA_ROWS

    km = kmean_ref[0]
    km_hi = km.astype(bf16)
    km_lo = (km - km_hi.astype(f32)).astype(bf16)
    blk = lax.broadcasted_iota(jnp.int32, (NB, QB), 0)
    zeros_half = jnp.zeros((HD, QB), bf16)
    zeros_tail = jnp.zeros((LANES - NB, QB), bf16)
    in_causal = (lax.broadcasted_iota(jnp.int32, (BLK, QB), 0)
                 <= lax.broadcasted_iota(jnp.int32, (BLK, QB), 1))
    lane8 = lax.broadcasted_iota(jnp.int32, (8, LANES), 1)

    def select(qi):
        qzs = []
        for h in range(2):
            q_h = qt_ref[0, qi, h * HD:(h + 1) * HD, :]
            qz = jnp.concatenate([q_h, zeros_half] if h == 0 else [zeros_half, q_h], axis=0)
            gate = _dot(km_hi, qz) + _dot(km_lo, qz)
            gate = jnp.where(blk < qi, gate, -jnp.inf)
            sel = jnp.zeros((NB, QB), jnp.bool_)
            for _ in range(MOBA_TOPK):
                top = jnp.max(gate, axis=0, keepdims=True)
                idx = jnp.min(jnp.where(gate == top, blk, NB), axis=0, keepdims=True)
                pick = blk == idx
                sel = jnp.logical_or(sel, jnp.logical_and(pick, top > -jnp.inf))
                gate = jnp.where(pick, -jnp.inf, gate)
            bias = jnp.where(sel, 0.0, NEG_BIG).astype(bf16)
            qzs.append(jnp.concatenate([qz, bias, zeros_tail], axis=0))
        return tuple(qzs)

    def block_scores(qzs, j, u, slot, tops):
        r0 = pl.multiple_of(j * BLK, BLK)
        onehot = jnp.tile(jnp.where(lane8 == j, 1.0, 0.0), (BLK // 8, 1)).astype(bf16)
        k_aug = jnp.concatenate([k_ref[0, pl.ds(r0, BLK), :], onehot], axis=1)
        out = []
        for h in range(2):
            s = _dot(k_aug, qzs[h])
            s_ref[slot, h, u * BLK:(u + 1) * BLK, :] = s
            top = jnp.max(s, axis=0, keepdims=True)
            out.append(top if tops is None else jnp.maximum(tops[h], top))
        return tuple(out)

    def own_scores(qzs, qi):
        k_own = k_ref[0, pl.ds(pl.multiple_of(qi * BLK, BLK), BLK), :]
        out = []
        for h in range(2):
            s = jnp.where(in_causal, _dot(k_own, qzs[h][:LANES]), NEG_BIG)
            out.append((s, jnp.max(s, axis=0, keepdims=True)))
        return tuple(out)

    def block_softmax(load_rows, u, m_new, part, h):
        for r in range(u * BLK, (u + 1) * BLK, ROWS):
            e = jnp.exp2(load_rows(r) - m_new)
            part = part + e
            p_ref[h, r:r + ROWS, :] = e.astype(bf16)
        return part

    def block_values(j, u, pv, h):
        d = _dot(vt_ref[0, j, h * HD:(h + 1) * HD, :], p_ref[h, u * BLK:(u + 1) * BLK, :])
        return d if pv is None else pv + d

    def group(t, tops, mls, accs, load_rows, prefetch):
        m_new = [jnp.maximum(mls[h][0], tops[h]) for h in range(2)]
        alpha = [jnp.exp2(mls[h][0] - m_new[h]) for h in range(2)]
        part = [jnp.zeros((ROWS, QB), f32)] * 2
        pv = [None, None]
        tops_next = None
        for u in range(U):
            tops_next = prefetch(u, tops_next)
            for h in range(2):
                part[h] = block_softmax(load_rows[h], u, m_new[h], part[h], h)
            for h in range(2):
                pv[h] = block_values(t * U + u, u, pv[h], h)
        mls = tuple((m_new[h], alpha[h] * mls[h][1] + jnp.sum(part[h], axis=0, keepdims=True))
                    for h in range(2))
        accs = tuple(alpha[h] * accs[h] + pv[h] for h in range(2))
        return tops_next, mls, accs

    def slot_rows(slot):
        return [lambda r, h=h: s_ref[slot, h, r:r + ROWS, :] for h in range(2)]

    ml0 = (jnp.full((1, QB), -jnp.inf, f32), jnp.zeros((1, QB), f32))
    acc0 = jnp.zeros((HD, QB), f32)

    def tile(qi, carry):
        qzs, qzs_next, own, tops0, phase = carry
        n_groups = jnp.maximum((qi + U - 1) // U, 1)

        def step(t, c, cur):
            tops, mls, accs = c
            return group(t, tops, mls, accs, slot_rows(cur),
                         lambda u, tn: block_scores(qzs, (t + 1) * U + u, u, 1 - cur, tn))

        def body(t, c):
            return lax.cond((t + phase) % 2 == 0,
                            functools.partial(step, t, cur=0), functools.partial(step, t, cur=1), c)

        state = lax.fori_loop(0, n_groups - 1, body, (tops0, (ml0, ml0), (acc0, acc0)))

        def boundary(c, cur):
            tops, mls, accs = c
            m_new = [jnp.maximum(mls[h][0], jnp.maximum(tops[h], own[h][1])) for h in range(2)]
            alpha = [jnp.exp2(mls[h][0] - m_new[h]) for h in range(2)]
            part = [jnp.zeros((ROWS, QB), f32)] * 2
            pv = [None, None]
            tops_next = None
            load_rows = slot_rows(cur)
            q_next = jnp.minimum(qi + 1, NT - 1)
            for u in range(U + 1):
                if u < U:
                    tops_next = block_scores(qzs_next, u, u, 1 - cur, tops_next)
                else:
                    own_next = own_scores(qzs_next, q_next)
                for h in range(2):
                    if u == 0:
                        for r in range(0, BLK, ROWS):
                            e = jnp.exp2(own[h][0][r:r + ROWS] - m_new[h])
                            part[h] = part[h] + e
                            p_ref[h, U * BLK + r:U * BLK + r + ROWS, :] = e.astype(bf16)
                    else:
                        part[h] = block_softmax(load_rows[h], u - 1, m_new[h], part[h], h)
                for h in range(2):
                    if u == 0:
                        pv[h] = _dot(vt_ref[0, qi, h * HD:(h + 1) * HD, :],
                                     p_ref[h, U * BLK:(U + 1) * BLK, :])
                    else:
                        pv[h] = block_values((n_groups - 1) * U + u - 1, u - 1, pv[h], h)
            qzs_after = select(jnp.minimum(qi + 2, NT - 1))
            outs = []
            for h in range(2):
                l = alpha[h] * mls[h][1] + jnp.sum(part[h], axis=0, keepdims=True)
                outs.append((alpha[h] * accs[h] + pv[h]) / l)
            o_t = jnp.concatenate(outs, axis=0)
            r0 = pl.multiple_of(qi * BLK, BLK)
            o_ref[0, pl.ds(r0, BLK), :] = o_t.T.astype(o_ref.dtype)
            return qzs_after, own_next, tops_next

        last_slot = (n_groups - 1 + phase) % 2
        qzs_after, own_next, tops_next = lax.cond(
            last_slot == 0, functools.partial(boundary, cur=0), functools.partial(boundary, cur=1),
            state)
        return qzs_next, qzs_after, own_next, tops_next, 1 - last_slot

    qzs0 = select(0)
    qzs1 = select(1)
    tops0 = None
    for u in range(U):
        tops0 = block_scores(qzs0, u, u, 0, tops0)
    lax.fori_loop(0, NT, tile, (qzs0, qzs1, own_scores(qzs0, 0), tops0, jnp.int32(0)))


def _moba_call(qt, k, vt, kmean):
    B, S, W = k.shape
    NB = S // MOBA_BLOCK
    QB = MOBA_BLOCK
    n_hp = W // LANES
    assert NB % MOBA_UNROLL == 0 and NB <= LANES
    return pl.pallas_call(
        _moba_kernel,
        grid=(B, n_hp),
        in_specs=[
            pl.BlockSpec((1, NB, LANES, QB), lambda b, hp: (b, 0, hp, 0)),
            pl.BlockSpec((1, S, LANES), lambda b, hp: (b, 0, hp)),
            pl.BlockSpec((1, NB, LANES, MOBA_BLOCK), lambda b, hp: (b, 0, hp, 0)),
            pl.BlockSpec((1, NB, LANES), lambda b, hp: (b, 0, hp)),
        ],
        out_specs=pl.BlockSpec((1, S, LANES), lambda b, hp: (b, 0, hp)),
        out_shape=jax.ShapeDtypeStruct((B, S, W), bf16),
        scratch_shapes=[
            pltpu.VMEM((2, 2, MOBA_UNROLL * MOBA_BLOCK, QB), f32),
            pltpu.VMEM((2, (MOBA_UNROLL + 1) * MOBA_BLOCK, QB), bf16),
        ],
        compiler_params=pltpu.CompilerParams(
            dimension_semantics=("parallel", "parallel"),
            vmem_limit_bytes=VMEM_LIMIT),
        name="moba",
    )(qt, k, vt, kmean)


def _split3(a):
    hi = a.astype(bf16)
    r1 = a - hi.astype(f32)
    mid = r1.astype(bf16)
    lo = (r1 - mid.astype(f32)).astype(bf16)
    return hi, mid, lo


def _gla_kernel(gq_ref, gk_ref, gv_ref, gr_ref, la_ref, gn_ref, y_ref, st_ref):
    C = GLA_CHUNK
    H = GLA_HEADS
    KW = GLA_KWIDTH
    DV = GLA_VAL_DIM
    NSUB = C // GLA_SUB
    T = gq_ref.shape[1]

    @pl.when(pl.program_id(1) == 0)
    def _():
        st_ref[...] = jnp.zeros_like(st_ref)

    row = lax.broadcasted_iota(jnp.int32, (C, C), 0)
    col = lax.broadcasted_iota(jnp.int32, (C, C), 1)
    tril = (col <= row).astype(bf16)
    lane_head = lax.broadcasted_iota(jnp.int32, (C, KW), 1) // GLA_KEY_DIM
    rt = lax.broadcasted_iota(jnp.int32, (C, NSUB * C), 0)
    rc = lax.broadcasted_iota(jnp.int32, (C, NSUB * C), 1)
    keep = jnp.logical_and(rc // C == rt // GLA_SUB, rc % C <= rt)
    st_lane_head = lax.broadcasted_iota(jnp.int32, (DV, KW), 1) // GLA_KEY_DIM

    n_chunks = T // C
    chunk_rows = [slice(c * C, (c + 1) * C) for c in range(n_chunks)]

    Gs = []
    for rows in chunk_rows:
        hi, mid, lo = _split3(la_ref[0, rows, :])
        Gs.append(_dot(tril, hi) + _dot(tril, mid) + _dot(tril, lo))

    rs, upds, qz_sts, decays, vs = [], [], [], [], []
    for rows, G in zip(chunk_rows, Gs):
        q = gq_ref[0, rows, :]
        k = gk_ref[0, rows, :]
        v = gv_ref[0, rows, :].astype(bf16)
        g_last = G[C - 1:C, :]
        g_ref_rows = [G[i * GLA_SUB:i * GLA_SUB + 1, :] for i in range(NSUB)]
        g_own = jnp.concatenate(
            [jnp.broadcast_to(g, (GLA_SUB, KW)) for g in g_ref_rows], axis=0)
        q_in = q * jnp.exp(G - g_own)
        k_in = jnp.concatenate(
            [(k * jnp.exp(jnp.minimum(g - G, EXP_CAP))).astype(bf16) for g in g_ref_rows],
            axis=0)
        q_st = q * jnp.exp(G)
        k_st = (k * jnp.exp(g_last - G)).astype(bf16)
        qz_in = jnp.concatenate(
            [jnp.where(lane_head == h, q_in, 0.0).astype(bf16) for h in range(H)], axis=0)
        qz_sts.append(jnp.concatenate(
            [jnp.where(lane_head == h, q_st, 0.0).astype(bf16) for h in range(H)], axis=0))
        rs.append(_dot_nt(qz_in, k_in))
        upds.append(_dot_tn(v, k_st))
        decays.append(jnp.exp(g_last))
        vs.append(v)

    o_intras = []
    for r, v in zip(rs, vs):
        per_head = []
        for h in range(H):
            r_h = jnp.where(keep, r[h * C:(h + 1) * C, :], 0.0).astype(bf16)
            v_rep = jnp.concatenate([v[:, h * DV:(h + 1) * DV]] * NSUB, axis=0)
            per_head.append(_dot(r_h, v_rep))
        o_intras.append(per_head)

    st = st_ref[...]
    o_inters = []
    for qz_st, upd, decay in zip(qz_sts, upds, decays):
        o_inters.append(_dot_nt(qz_st, st.astype(bf16)))
        st = st * decay
        for h in range(H):
            st = st + jnp.where(st_lane_head == h, upd[h * DV:(h + 1) * DV, :], 0.0)
    st_ref[...] = st

    for rows, o_intra, o_inter in zip(chunk_rows, o_intras, o_inters):
        for h in range(H):
            o = o_intra[h] + o_inter[h * C:(h + 1) * C, :]
            o = o * lax.rsqrt(jnp.mean(o * o, axis=-1, keepdims=True) + EPS)
            o = o * gn_ref[:, h * DV:(h + 1) * DV]
            gr = gr_ref[0, rows, h * DV:(h + 1) * DV]
            y = o * (gr * jax.nn.sigmoid(gr))
            y_ref[0, rows, h * DV:(h + 1) * DV] = y.astype(y_ref.dtype)


def _gla_call(gq, gk, gv, gr, la, gn):
    B, S, _ = gq.shape
    T = GLA_TILE
    tok = lambda w: pl.BlockSpec((1, T, w), lambda b, i: (b, i, 0))
    return pl.pallas_call(
        _gla_kernel,
        grid=(B, S // T),
        in_specs=[tok(GLA_KWIDTH), tok(GLA_KWIDTH), tok(GLA_VWIDTH), tok(GLA_VWIDTH),
                  tok(GLA_KWIDTH), pl.BlockSpec((1, GLA_VWIDTH), lambda b, i: (0, 0))],
        out_specs=tok(GLA_VWIDTH),
        out_shape=jax.ShapeDtypeStruct((B, S, GLA_VWIDTH), bf16),
        scratch_shapes=[pltpu.VMEM((GLA_VAL_DIM, GLA_KWIDTH), f32)],
        compiler_params=pltpu.CompilerParams(
            dimension_semantics=("parallel", "arbitrary"), vmem_limit_bytes=VMEM_LIMIT),
        name="gla",
    )(gq, gk, gv, gr, la, gn)


def _ffn_kernel(x_ref, ym_ref, yg_ref, wo_m_ref, wo_g_ref, fg_ref, wup_ref, cw_ref, cb_ref,
                wdn_ref, og_ref, out_ref, u_ref):
    T = x_ref.shape[1]
    PAD = 8

    @pl.when(pl.program_id(1) == 0)
    def _():
        u_ref[0:PAD, :] = jnp.zeros((PAD, u_ref.shape[1]), f32)

    h = x_ref[0] + _dot(ym_ref[0], wo_m_ref[...]) + _dot(yg_ref[0], wo_g_ref[...])
    hn = _rms(h, fg_ref[...]).astype(bf16)
    u_ref[PAD:PAD + T, :] = _dot(hn, wup_ref[...])
    cw = cw_ref[...]
    conv = (cw[0:1] * u_ref[PAD - 2:PAD - 2 + T, :]
            + cw[1:2] * u_ref[PAD - 1:PAD - 1 + T, :]
            + cw[2:3] * u_ref[PAD:PAD + T, :]
            + cb_ref[...])
    u_ref[0:PAD, :] = u_ref[T:T + PAD, :]
    hg = conv[:, :D_FF]
    act = (hg * jax.nn.sigmoid(hg) * conv[:, D_FF:]).astype(bf16)
    y = h + _dot(act, wdn_ref[...])
    out_ref[0] = _rms(y, og_ref[...])


def _ffn_call(x, ym, yg, wo_m, wo_g, fg, wup, cw, cb, wdn, og):
    B, S, D = x.shape
    T = FFN_TILE
    tok = lambda w: pl.BlockSpec((1, T, w), lambda b, i: (b, i, 0))
    const = lambda a: pl.BlockSpec(a.shape, lambda b, i: (0,) * a.ndim, pipeline_mode=pl.Buffered(1))
    return pl.pallas_call(
        _ffn_kernel,
        grid=(B, S // T),
        in_specs=[tok(D), tok(MOBA_WIDTH), tok(GLA_VWIDTH),
                  const(wo_m), const(wo_g), const(fg), const(wup), const(cw), const(cb),
                  const(wdn), const(og)],
        out_specs=tok(D),
        out_shape=jax.ShapeDtypeStruct((B, S, D), x.dtype),
        scratch_shapes=[pltpu.VMEM((T + 8, 2 * D_FF), f32)],
        compiler_params=pltpu.CompilerParams(
            dimension_semantics=("parallel", "arbitrary"), vmem_limit_bytes=VMEM_LIMIT),
        name="out_ffn",
    )(x, ym, yg, wo_m, wo_g, fg, wup, cw, cb, wdn, og)


def _rope_tables(S):
    hd = MOBA_HEAD_DIM
    inv_freq = 1.0 / (ROPE_THETA ** (jnp.arange(0, hd, 2, dtype=f32) / hd))
    ang = jnp.arange(S).astype(f32)[:, None] * inv_freq[None, :]
    cos, sin = jnp.cos(ang), jnp.sin(ang)
    cos_r = jnp.tile(cos, (1, LANES // (hd // 2)))
    sin_r = jnp.tile(jnp.concatenate([-sin, sin], axis=1), (1, LANES // hd))
    return cos.T, sin.T, cos_r, sin_r


def kernel(x, attn_norm_g, w_in, w_gate_up, b_gate, gla_norm_g, w_out, ffn_norm_g, w_ffn_up,
           conv_w, conv_b, w_ffn_down, final_norm_g):
    B, S, D = x.shape
    l = 0
    o_mk = MOBA_WIDTH
    o_mv = 2 * MOBA_WIDTH
    o_gq = 3 * MOBA_WIDTH
    o_gg = o_gq + 2 * GLA_KWIDTH + 2 * GLA_VWIDTH
    w = w_in[l]
    wqt = w[:, :o_mk].T.astype(bf16)
    wk = w[:, o_mk:o_mv].astype(bf16)
    wvt = jnp.concatenate([w[:, o_mv:o_gq], w[:, o_gg:]], axis=1).T.astype(bf16)
    wg = w[:, o_gq:o_gg].astype(bf16)
    wgu = w_gate_up[l]
    wgu_hi = wgu.astype(bf16)
    wgu_lo = (wgu - wgu_hi.astype(f32)).astype(bf16)
    cos_t, sin_t, cos_r, sin_r = _rope_tables(S)

    qt, k, vt, kmean, gq, gk, gv, gr, la = _proj_call(
        x, attn_norm_g[l][None, :], wqt, wk, wvt, wg, wgu_hi, wgu_lo, b_gate[l][None, :],
        cos_t, sin_t, cos_r, sin_r)
    kmean = kmean.reshape(B, S // MOBA_BLOCK, MOBA_WIDTH)

    y_moba = _moba_call(qt, k, vt, kmean)
    y_gla = _gla_call(gq, gk, gv, gr, la, gla_norm_g[l].reshape(1, GLA_VWIDTH))

    wo = w_out[l].astype(bf16)
    return _ffn_call(
        x, y_moba, y_gla, wo[:MOBA_WIDTH], wo[MOBA_WIDTH:], ffn_norm_g[l][None, :],
        w_ffn_up[l].astype(bf16), conv_w[l], conv_b[l][None, :], w_ffn_down[l].astype(bf16),
        final_norm_g[None, :])
```

```python
import functools

import jax
import jax.numpy as jnp
from jax import lax
from jax.experimental import pallas as pl
from jax.experimental.pallas import tpu as pltpu

D_MODEL = 1024
MOBA_HEAD_DIM = 64
MOBA_HEADS = 8
MOBA_WIDTH = 512
MOBA_BLOCK = 256
MOBA_TOPK = 3
MOBA_UNROLL = 8
MOBA_ROWS = 32
GLA_HEADS = 4
GLA_VAL_DIM = 128
GLA_KEY_DIM = 64
GLA_VWIDTH = 512
GLA_KWIDTH = 256
GLA_GATE_RANK = 16
GLA_GATE_TAU = 16.0
GLA_CHUNK = 64
GLA_SUB = 16
D_FF = 2816
CONV_WIDTH = 3
ROPE_THETA = 10000.0
EPS = 1e-6

LANES = 128
LOG2_E = 1.4426950408889634
NEG_BIG = -1e30
EXP_CAP = 80.0

PROJ_TILE = 1024
GLA_TILE = 1024
FFN_TILE = 512
V7X_VMEM_BYTES = 64 * 1024 * 1024
VMEM_LIMIT = V7X_VMEM_BYTES - 8 * 1024 * 1024

f32 = jnp.float32
bf16 = jnp.bfloat16


def _dot(a, b):
    return jnp.dot(a, b, preferred_element_type=f32)


def _dot_nt(a, b):
    return lax.dot_general(a, b, (((1,), (1,)), ((), ())), preferred_element_type=f32)


def _dot_tn(a, b):
    return lax.dot_general(a, b, (((0,), (0,)), ((), ())), preferred_element_type=f32)


def _rms(xf, g):
    return xf * lax.rsqrt(jnp.mean(xf * xf, axis=-1, keepdims=True) + EPS) * g


def _proj_kernel(x_ref, g_ref, wqt_ref, wk_ref, wvt_ref, wg_ref, wgu_hi_ref, wgu_lo_ref, bg_ref,
                 cos_t_ref, sin_t_ref, cos_r_ref, sin_r_ref,
                 qt_ref, k_ref, vt_ref, kmean_ref, gq_ref, gk_ref, gv_ref, gr_ref, la_ref):
    T = x_ref.shape[1]
    xn = _rms(x_ref[0], g_ref[...]).astype(bf16)

    nblk = T // MOBA_BLOCK
    vg = _dot_nt(wvt_ref[...], xn)
    vt = vg[:MOBA_WIDTH].astype(bf16)
    for j in range(nblk):
        vt_ref[0, j] = vt[:, j * MOBA_BLOCK:(j + 1) * MOBA_BLOCK]

    gate_lr = vg[MOBA_WIDTH:]
    lr_hi = gate_lr.astype(bf16)
    lr_lo = (gate_lr - lr_hi.astype(f32)).astype(bf16)
    z = (_dot_tn(lr_hi, wgu_hi_ref[...]) + _dot_tn(lr_lo, wgu_hi_ref[...])
         + _dot_tn(lr_hi, wgu_lo_ref[...]) + bg_ref[...])
    log_sig = jnp.minimum(z, 0.0) - jnp.log1p(jnp.exp(-jnp.abs(z)))
    la_ref[0] = log_sig * (1.0 / GLA_GATE_TAU)

    qt = _dot_nt(wqt_ref[...], xn)
    cos_t = cos_t_ref[...]
    sin_t = sin_t_ref[...]
    half = MOBA_HEAD_DIM // 2
    scale = MOBA_HEAD_DIM ** -0.5 * LOG2_E
    for h in range(MOBA_HEADS):
        r0 = h * MOBA_HEAD_DIM
        t1 = qt[r0:r0 + half]
        t2 = qt[r0 + half:r0 + MOBA_HEAD_DIM]
        lo = ((t1 * cos_t - t2 * sin_t) * scale).astype(bf16)
        hi = ((t2 * cos_t + t1 * sin_t) * scale).astype(bf16)
        for j in range(nblk):
            qt_ref[0, j, r0:r0 + half, :] = lo[:, j * MOBA_BLOCK:(j + 1) * MOBA_BLOCK]
            qt_ref[0, j, r0 + half:r0 + MOBA_HEAD_DIM, :] = hi[:, j * MOBA_BLOCK:(j + 1) * MOBA_BLOCK]

    k = _dot(xn, wk_ref[...])
    cos_r = cos_r_ref[...]
    sin_r = sin_r_ref[...]
    lane = lax.broadcasted_iota(jnp.int32, (T, LANES), 1)
    first_half = (lane % MOBA_HEAD_DIM) < half
    for p in range(MOBA_WIDTH // LANES):
        kp = k[:, p * LANES:(p + 1) * LANES]
        rot = jnp.where(first_half, pltpu.roll(kp, LANES - half, 1), pltpu.roll(kp, half, 1))
        kr = kp * cos_r + rot * sin_r
        k_ref[0, :, p * LANES:(p + 1) * LANES] = kr.astype(bf16)
        for j in range(nblk):
            kmean_ref[0, 0, j:j + 1, p * LANES:(p + 1) * LANES] = jnp.mean(
                kr[j * MOBA_BLOCK:(j + 1) * MOBA_BLOCK], axis=0, keepdims=True)

    pg = _dot(xn, wg_ref[...])
    o_gk = GLA_KWIDTH
    o_gv = o_gk + GLA_KWIDTH
    o_gr = o_gv + GLA_VWIDTH
    gq_ref[0] = pg[:, :o_gk] * (GLA_KEY_DIM ** -0.5)
    gk_ref[0] = pg[:, o_gk:o_gv]
    gv_ref[0] = pg[:, o_gv:o_gr]
    gr_ref[0] = pg[:, o_gr:]


def _proj_call(x, g, wqt, wk, wvt, wg, wgu_hi, wgu_lo, bg, cos_t, sin_t, cos_r, sin_r):
    B, S, D = x.shape
    T = PROJ_TILE
    nb_t = T // MOBA_BLOCK
    const = lambda shape: pl.BlockSpec(shape, lambda b, i: (0,) * len(shape))
    tok = lambda w: pl.BlockSpec((1, T, w), lambda b, i: (b, i, 0))
    out_shape = (
        jax.ShapeDtypeStruct((B, S // MOBA_BLOCK, MOBA_WIDTH, MOBA_BLOCK), bf16),
        jax.ShapeDtypeStruct((B, S, MOBA_WIDTH), bf16),
        jax.ShapeDtypeStruct((B, S // MOBA_BLOCK, MOBA_WIDTH, MOBA_BLOCK), bf16),
        jax.ShapeDtypeStruct((B, S // T, nb_t, MOBA_WIDTH), f32),
        jax.ShapeDtypeStruct((B, S, GLA_KWIDTH), f32),
        jax.ShapeDtypeStruct((B, S, GLA_KWIDTH), f32),
        jax.ShapeDtypeStruct((B, S, GLA_VWIDTH), f32),
        jax.ShapeDtypeStruct((B, S, GLA_VWIDTH), f32),
        jax.ShapeDtypeStruct((B, S, GLA_KWIDTH), f32),
    )
    out_specs = (
        pl.BlockSpec((1, nb_t, MOBA_WIDTH, MOBA_BLOCK), lambda b, i: (b, i, 0, 0)),
        tok(MOBA_WIDTH),
        pl.BlockSpec((1, nb_t, MOBA_WIDTH, MOBA_BLOCK), lambda b, i: (b, i, 0, 0)),
        pl.BlockSpec((1, 1, nb_t, MOBA_WIDTH), lambda b, i: (b, i, 0, 0)),
        tok(GLA_KWIDTH), tok(GLA_KWIDTH), tok(GLA_VWIDTH), tok(GLA_VWIDTH), tok(GLA_KWIDTH),
    )
    in_specs = [
        tok(D),
        const((1, D)),
        const(wqt.shape), const(wk.shape), const(wvt.shape), const(wg.shape),
        const(wgu_hi.shape), const(wgu_lo.shape), const((1, GLA_KWIDTH)),
        pl.BlockSpec((MOBA_HEAD_DIM // 2, T), lambda b, i: (0, i)),
        pl.BlockSpec((MOBA_HEAD_DIM // 2, T), lambda b, i: (0, i)),
        pl.BlockSpec((T, LANES), lambda b, i: (i, 0)),
        pl.BlockSpec((T, LANES), lambda b, i: (i, 0)),
    ]
    return pl.pallas_call(
        _proj_kernel,
        grid=(B, S // T),
        in_specs=in_specs,
        out_specs=out_specs,
        out_shape=out_shape,
        compiler_params=pltpu.CompilerParams(
            dimension_semantics=("parallel", "parallel"), vmem_limit_bytes=VMEM_LIMIT),
        name="in_proj",
    )(x, g, wqt, wk, wvt, wg, wgu_hi, wgu_lo, bg, cos_t, sin_t, cos_r, sin_r)


def _moba_kernel(qt_ref, k_ref, vt_ref, kmean_ref, o_ref, s_ref, p_ref):
    NT = qt_ref.shape[1]
    QB = qt_ref.shape[3]
    NB = kmean_ref.shape[1]
    HD = MOBA_HEAD_DIM
    U = MOBA_UNROLL
    BLK = MOBA_BLOCK
    ROWS = MOBA_ROWS

    km = kmean_ref[0]
    km_hi = km.astype(bf16)
    km_lo = (km - km_hi.astype(f32)).astype(bf16)
    blk = lax.broadcasted_iota(jnp.int32, (NB, QB), 0)
    zeros_half = jnp.zeros((HD, QB), bf16)
    zeros_tail = jnp.zeros((LANES - NB, QB), bf16)
    in_causal = (lax.broadcasted_iota(jnp.int32, (BLK, QB), 0)
                 <= lax.broadcasted_iota(jnp.int32, (BLK, QB), 1))
    lane8 = lax.broadcasted_iota(jnp.int32, (8, LANES), 1)

    def select(qi):
        qzs = []
        for h in range(2):
            q_h = qt_ref[0, qi, h * HD:(h + 1) * HD, :]
            qz = jnp.concatenate([q_h, zeros_half] if h == 0 else [zeros_half, q_h], axis=0)
            gate = _dot(km_hi, qz) + _dot(km_lo, qz)
            gate = jnp.where(blk < qi, gate, -jnp.inf)
            sel = jnp.zeros((NB, QB), jnp.bool_)
            for _ in range(MOBA_TOPK):
                top = jnp.max(gate, axis=0, keepdims=True)
                idx = jnp.min(jnp.where(gate == top, blk, NB), axis=0, keepdims=True)
                pick = blk == idx
                sel = jnp.logical_or(sel, jnp.logical_and(pick, top > -jnp.inf))
                gate = jnp.where(pick, -jnp.inf, gate)
            bias = jnp.where(sel, 0.0, NEG_BIG).astype(bf16)
            qzs.append(jnp.concatenate([qz, bias, zeros_tail], axis=0))
        return tuple(qzs)

    def block_scores(qzs, j, u, slot, tops):
        r0 = pl.multiple_of(j * BLK, BLK)
        onehot = jnp.tile(jnp.where(lane8 == j, 1.0, 0.0), (BLK // 8, 1)).astype(bf16)
        k_aug = jnp.concatenate([k_ref[0, pl.ds(r0, BLK), :], onehot], axis=1)
        out = []
        for h in range(2):
            s = _dot(k_aug, qzs[h])
            s_ref[slot, h, u * BLK:(u + 1) * BLK, :] = s
            top = jnp.max(s, axis=0, keepdims=True)
            out.append(top if tops is None else jnp.maximum(tops[h], top))
        return tuple(out)

    def own_scores(qzs, qi):
        k_own = k_ref[0, pl.ds(pl.multiple_of(qi * BLK, BLK), BLK), :]
        out = []
        for h in range(2):
            s = jnp.where(in_causal, _dot(k_own, qzs[h][:LANES]), NEG_BIG)
            out.append((s, jnp.max(s, axis=0, keepdims=True)))
        return tuple(out)

    def block_softmax(load_rows, u, m_new, part, h):
        for r in range(u * BLK, (u + 1) * BLK, ROWS):
            e = jnp.exp2(load_rows(r) - m_new)
            part = part + e
            p_ref[h, r:r + ROWS, :] = e.astype(bf16)
        return part

    def block_values(j, u, pv, h):
        d = _dot(vt_ref[0, j, h * HD:(h + 1) * HD, :], p_ref[h, u * BLK:(u + 1) * BLK, :])
        return d if pv is None else pv + d

    def group(t, tops, mls, accs, load_rows, prefetch):
        m_new = [jnp.maximum(mls[h][0], tops[h]) for h in range(2)]
        alpha = [jnp.exp2(mls[h][0] - m_new[h]) for h in range(2)]
        part = [jnp.zeros((ROWS, QB), f32)] * 2
        pv = [None, None]
        tops_next = None
        for u in range(U):
            tops_next = prefetch(u, tops_next)
            for h in range(2):
                part[h] = block_softmax(load_rows[h], u, m_new[h], part[h], h)
            for h in range(2):
                pv[h] = block_values(t * U + u, u, pv[h], h)
        mls = tuple((m_new[h], alpha[h] * mls[h][1] + jnp.sum(part[h], axis=0, keepdims=True))
                    for h in range(2))
        accs = tuple(alpha[h] * accs[h] + pv[h] for h in range(2))
        return tops_next, mls, accs

    def slot_rows(slot):
        return [lambda r, h=h: s_ref[slot, h, r:r + ROWS, :] for h in range(2)]

    ml0 = (jnp.full((1, QB), -jnp.inf, f32), jnp.zeros((1, QB), f32))
    acc0 = jnp.zeros((HD, QB), f32)

    def tile(qi, carry):
        qzs, qzs_next, own, tops0, phase = carry
        n_groups = jnp.maximum((qi + U - 1) // U, 1)

        def step(t, c, cur):
            tops, mls, accs = c
            return group(t, tops, mls, accs, slot_rows(cur),
                         lambda u, tn: block_scores(qzs, (t + 1) * U + u, u, 1 - cur, tn))

        def body(t, c):
            return lax.cond((t + phase) % 2 == 0,
                            functools.partial(step, t, cur=0), functools.partial(step, t, cur=1), c)

        state = lax.fori_loop(0, n_groups - 1, body, (tops0, (ml0, ml0), (acc0, acc0)))

        def boundary(c, cur):
            tops, mls, accs = c
            m_new = [jnp.maximum(mls[h][0], jnp.maximum(tops[h], own[h][1])) for h in range(2)]
            alpha = [jnp.exp2(mls[h][0] - m_new[h]) for h in range(2)]
            part = [jnp.zeros((ROWS, QB), f32)] * 2
            pv = [None, None]
            tops_next = None
            load_rows = slot_rows(cur)
            q_next = jnp.minimum(qi + 1, NT - 1)
            for u in range(U + 1):
                if u < U:
                    tops_next = block_scores(qzs_next, u, u, 1 - cur, tops_next)
                else:
                    own_next = own_scores(qzs_next, q_next)
                for h in range(2):
                    if u == 0:
                        for r in range(0, BLK, ROWS):
                            e = jnp.exp2(own[h][0][r:r + ROWS] - m_new[h])
                            part[h] = part[h] + e
                            p_ref[h, U * BLK + r:U * BLK + r + ROWS, :] = e.astype(bf16)
                    else:
                        part[h] = block_softmax(load_rows[h], u - 1, m_new[h], part[h], h)
                for h in range(2):
                    if u == 0:
                        pv[h] = _dot(vt_ref[0, qi, h * HD:(h + 1) * HD, :],
                                     p_ref[h, U * BLK:(U + 1) * BLK, :])
                    else:
                        pv[h] = block_values((n_groups - 1) * U + u - 1, u - 1, pv[h], h)
            qzs_after = select(jnp.minimum(qi + 2, NT - 1))
            outs = []
            for h in range(2):
                l = alpha[h] * mls[h][1] + jnp.sum(part[h], axis=0, keepdims=True)
                outs.append((alpha[h] * accs[h] + pv[h]) / l)
            o_t = jnp.concatenate(outs, axis=0)
            r0 = pl.multiple_of(qi * BLK, BLK)
            o_ref[0, pl.ds(r0, BLK), :] = o_t.T.astype(o_ref.dtype)
            return qzs_after, own_next, tops_next

        last_slot = (n_groups - 1 + phase) % 2
        qzs_after, own_next, tops_next = lax.cond(
            last_slot == 0, functools.partial(boundary, cur=0), functools.partial(boundary, cur=1),
            state)
        return qzs_next, qzs_after, own_next, tops_next, 1 - last_slot

    qzs0 = select(0)
    qzs1 = select(1)
    tops0 = None
    for u in range(U):
        tops0 = block_scores(qzs0, u, u, 0, tops0)
    lax.fori_loop(0, NT, tile, (qzs0, qzs1, own_scores(qzs0, 0), tops0, jnp.int32(0)))


def _moba_call(qt, k, vt, kmean):
    B, S, W = k.shape
    NB = S // MOBA_BLOCK
    QB = MOBA_BLOCK
    n_hp = W // LANES
    assert NB % MOBA_UNROLL == 0 and NB <= LANES
    return pl.pallas_call(
        _moba_kernel,
        grid=(B, n_hp),
        in_specs=[
            pl.BlockSpec((1, NB, LANES, QB), lambda b, hp: (b, 0, hp, 0)),
            pl.BlockSpec((1, S, LANES), lambda b, hp: (b, 0, hp)),
            pl.BlockSpec((1, NB, LANES, MOBA_BLOCK), lambda b, hp: (b, 0, hp, 0)),
            pl.BlockSpec((1, NB, LANES), lambda b, hp: (b, 0, hp)),
        ],
        out_specs=pl.BlockSpec((1, S, LANES), lambda b, hp: (b, 0, hp)),
        out_shape=jax.ShapeDtypeStruct((B, S, W), bf16),
        scratch_shapes=[
            pltpu.VMEM((2, 2, MOBA_UNROLL * MOBA_BLOCK, QB), f32),
            pltpu.VMEM((2, (MOBA_UNROLL + 1) * MOBA_BLOCK, QB), bf16),
        ],
        compiler_params=pltpu.CompilerParams(
            dimension_semantics=("parallel", "parallel"),
            vmem_limit_bytes=VMEM_LIMIT),
        name="moba",
    )(qt, k, vt, kmean)


def _split3(a):
    hi = a.astype(bf16)
    r1 = a - hi.astype(f32)
    mid = r1.astype(bf16)
    lo = (r1 - mid.astype(f32)).astype(bf16)
    return hi, mid, lo


def _gla_kernel(gq_ref, gk_ref, gv_ref, gr_ref, la_ref, gn_ref, y_ref, st_ref):
    C = GLA_CHUNK
    H = GLA_HEADS
    KW = GLA_KWIDTH
    DV = GLA_VAL_DIM
    NSUB = C // GLA_SUB
    T = gq_ref.shape[1]

    @pl.when(pl.program_id(1) == 0)
    def _():
        st_ref[...] = jnp.zeros_like(st_ref)

    row = lax.broadcasted_iota(jnp.int32, (C, C), 0)
    col = lax.broadcasted_iota(jnp.int32, (C, C), 1)
    tril = (col <= row).astype(bf16)
    lane_head = lax.broadcasted_iota(jnp.int32, (C, KW), 1) // GLA_KEY_DIM
    rt = lax.broadcasted_iota(jnp.int32, (C, NSUB * C), 0)
    rc = lax.broadcasted_iota(jnp.int32, (C, NSUB * C), 1)
    keep = jnp.logical_and(rc // C == rt // GLA_SUB, rc % C <= rt)
    st_lane_head = lax.broadcasted_iota(jnp.int32, (DV, KW), 1) // GLA_KEY_DIM

    n_chunks = T // C
    chunk_rows = [slice(c * C, (c + 1) * C) for c in range(n_chunks)]

    Gs = []
    for rows in chunk_rows:
        hi, mid, lo = _split3(la_ref[0, rows, :])
        Gs.append(_dot(tril, hi) + _dot(tril, mid) + _dot(tril, lo))

    rs, upds, qz_sts, decays, vs = [], [], [], [], []
    for rows, G in zip(chunk_rows, Gs):
        q = gq_ref[0, rows, :]
        k = gk_ref[0, rows, :]
        v = gv_ref[0, rows, :].astype(bf16)
        g_last = G[C - 1:C, :]
        g_ref_rows = [G[i * GLA_SUB:i * GLA_SUB + 1, :] for i in range(NSUB)]
        g_own = jnp.concatenate(
            [jnp.broadcast_to(g, (GLA_SUB, KW)) for g in g_ref_rows], axis=0)
        q_in = q * jnp.exp(G - g_own)
        k_in = jnp.concatenate(
            [(k * jnp.exp(jnp.minimum(g - G, EXP_CAP))).astype(bf16) for g in g_ref_rows],
            axis=0)
        q_st = q * jnp.exp(G)
        k_st = (k * jnp.exp(g_last - G)).astype(bf16)
        qz_in = jnp.concatenate(
            [jnp.where(lane_head == h, q_in, 0.0).astype(bf16) for h in range(H)], axis=0)
        qz_sts.append(jnp.concatenate(
            [jnp.where(lane_head == h, q_st, 0.0).astype(bf16) for h in range(H)], axis=0))
        rs.append(_dot_nt(qz_in, k_in))
        upds.append(_dot_tn(v, k_st))
        decays.append(jnp.exp(g_last))
        vs.append(v)

    o_intras = []
    for r, v in zip(rs, vs):
        per_head = []
        for h in range(H):
            r_h = jnp.where(keep, r[h * C:(h + 1) * C, :], 0.0).astype(bf16)
            v_rep = jnp.concatenate([v[:, h * DV:(h + 1) * DV]] * NSUB, axis=0)
            per_head.append(_dot(r_h, v_rep))
        o_intras.append(per_head)

    st = st_ref[...]
    o_inters = []
    for qz_st, upd, decay in zip(qz_sts, upds, decays):
        o_inters.append(_dot_nt(qz_st, st.astype(bf16)))
        st = st * decay
        for h in range(H):
            st = st + jnp.where(st_lane_head == h, upd[h * DV:(h + 1) * DV, :], 0.0)
    st_ref[...] = st

    for rows, o_intra, o_inter in zip(chunk_rows, o_intras, o_inters):
        for h in range(H):
            o = o_intra[h] + o_inter[h * C:(h + 1) * C, :]
            o = o * lax.rsqrt(jnp.mean(o * o, axis=-1, keepdims=True) + EPS)
            o = o * gn_ref[:, h * DV:(h + 1) * DV]
            gr = gr_ref[0, rows, h * DV:(h + 1) * DV]
            y = o * (gr * jax.nn.sigmoid(gr))
            y_ref[0, rows, h * DV:(h + 1) * DV] = y.astype(y_ref.dtype)


def _gla_call(gq, gk, gv, gr, la, gn):
    B, S, _ = gq.shape
    T = GLA_TILE
    tok = lambda w: pl.BlockSpec((1, T, w), lambda b, i: (b, i, 0))
    return pl.pallas_call(
        _gla_kernel,
        grid=(B, S // T),
        in_specs=[tok(GLA_KWIDTH), tok(GLA_KWIDTH), tok(GLA_VWIDTH), tok(GLA_VWIDTH),
                  tok(GLA_KWIDTH), pl.BlockSpec((1, GLA_VWIDTH), lambda b, i: (0, 0))],
        out_specs=tok(GLA_VWIDTH),
        out_shape=jax.ShapeDtypeStruct((B, S, GLA_VWIDTH), bf16),
        scratch_shapes=[pltpu.VMEM((GLA_VAL_DIM, GLA_KWIDTH), f32)],
        compiler_params=pltpu.CompilerParams(
            dimension_semantics=("parallel", "arbitrary"), vmem_limit_bytes=VMEM_LIMIT),
        name="gla",
    )(gq, gk, gv, gr, la, gn)


def _ffn_kernel(x_ref, ym_ref, yg_ref, wo_m_ref, wo_g_ref, fg_ref, wup_ref, cw_ref, cb_ref,
                wdn_ref, og_ref, out_ref, u_ref):
    T = x_ref.shape[1]
    PAD = 8

    @pl.when(pl.program_id(1) == 0)
    def _():
        u_ref[0:PAD, :] = jnp.zeros((PAD, u_ref.shape[1]), f32)

    h = x_ref[0] + _dot(ym_ref[0], wo_m_ref[...]) + _dot(yg_ref[0], wo_g_ref[...])
    hn = _rms(h, fg_ref[...]).astype(bf16)
    u_ref[PAD:PAD + T, :] = _dot(hn, wup_ref[...])
    cw = cw_ref[...]
    conv = (cw[0:1] * u_ref[PAD - 2:PAD - 2 + T, :]
            + cw[1:2] * u_ref[PAD - 1:PAD - 1 + T, :]
            + cw[2:3] * u_ref[PAD:PAD + T, :]
            + cb_ref[...])
    u_ref[0:PAD, :] = u_ref[T:T + PAD, :]
    hg = conv[:, :D_FF]
    act = (hg * jax.nn.sigmoid(hg) * conv[:, D_FF:]).astype(bf16)
    y = h + _dot(act, wdn_ref[...])
    out_ref[0] = _rms(y, og_ref[...])


def _ffn_call(x, ym, yg, wo_m, wo_g, fg, wup, cw, cb, wdn, og):
    B, S, D = x.shape
    T = FFN_TILE
    tok = lambda w: pl.BlockSpec((1, T, w), lambda b, i: (b, i, 0))
    const = lambda a: pl.BlockSpec(a.shape, lambda b, i: (0,) * a.ndim, pipeline_mode=pl.Buffered(1))
    return pl.pallas_call(
        _ffn_kernel,
        grid=(B, S // T),
        in_specs=[tok(D), tok(MOBA_WIDTH), tok(GLA_VWIDTH),
                  const(wo_m), const(wo_g), const(fg), const(wup), const(cw), const(cb),
                  const(wdn), const(og)],
        out_specs=tok(D),
        out_shape=jax.ShapeDtypeStruct((B, S, D), x.dtype),
        scratch_shapes=[pltpu.VMEM((T + 8, 2 * D_FF), f32)],
        compiler_params=pltpu.CompilerParams(
            dimension_semantics=("parallel", "arbitrary"), vmem_limit_bytes=VMEM_LIMIT),
        name="out_ffn",
    )(x, ym, yg, wo_m, wo_g, fg, wup, cw, cb, wdn, og)


def _rope_tables(S):
    hd = MOBA_HEAD_DIM
    inv_freq = 1.0 / (ROPE_THETA ** (jnp.arange(0, hd, 2, dtype=f32) / hd))
    ang = jnp.arange(S).astype(f32)[:, None] * inv_freq[None, :]
    cos, sin = jnp.cos(ang), jnp.sin(ang)
    cos_r = jnp.tile(cos, (1, LANES // (hd // 2)))
    sin_r = jnp.tile(jnp.concatenate([-sin, sin], axis=1), (1, LANES // hd))
    return cos.T, sin.T, cos_r, sin_r


def kernel(x, attn_norm_g, w_in, w_gate_up, b_gate, gla_norm_g, w_out, ffn_norm_g, w_ffn_up,
           conv_w, conv_b, w_ffn_down, final_norm_g):
    B, S, D = x.shape
    assert w_in.shape[0] == 1, "single-layer block"
    assert D == D_MODEL and conv_w.shape[1] == CONV_WIDTH and w_ffn_down.shape[1] == D_FF
    assert S % PROJ_TILE == 0 and S % GLA_TILE == 0 and S % FFN_TILE == 0
    l = 0
    o_mk = MOBA_WIDTH
    o_mv = 2 * MOBA_WIDTH
    o_gq = 3 * MOBA_WIDTH
    o_gg = o_gq + 2 * GLA_KWIDTH + 2 * GLA_VWIDTH
    w = w_in[l]
    wqt = w[:, :o_mk].T.astype(bf16)
    wk = w[:, o_mk:o_mv].astype(bf16)
    wvt = jnp.concatenate([w[:, o_mv:o_gq], w[:, o_gg:]], axis=1).T.astype(bf16)
    wg = w[:, o_gq:o_gg].astype(bf16)
    wgu = w_gate_up[l]
    wgu_hi = wgu.astype(bf16)
    wgu_lo = (wgu - wgu_hi.astype(f32)).astype(bf16)
    cos_t, sin_t, cos_r, sin_r = _rope_tables(S)

    qt, k, vt, kmean, gq, gk, gv, gr, la = _proj_call(
        x, attn_norm_g[l][None, :], wqt, wk, wvt, wg, wgu_hi, wgu_lo, b_gate[l][None, :],
        cos_t, sin_t, cos_r, sin_r)
    kmean = kmean.reshape(B, S // MOBA_BLOCK, MOBA_WIDTH)

    y_moba = _moba_call(qt, k, vt, kmean)
    y_gla = _gla_call(gq, gk, gv, gr, la, gla_norm_g[l].reshape(1, GLA_VWIDTH))

    wo = w_out[l].astype(bf16)
    return _ffn_call(
        x, y_moba, y_gla, wo[:MOBA_WIDTH], wo[MOBA_WIDTH:], ffn_norm_g[l][None, :],
        w_ffn_up[l].astype(bf16), conv_w[l], conv_b[l][None, :], w_ffn_down[l].astype(bf16),
        final_norm_g[None, :])
```

```python
import functools

import jax
import jax.numpy as jnp
from jax import lax
from jax.experimental import pallas as pl
from jax.experimental.pallas import tpu as pltpu

D_MODEL = 1024
MOBA_HEAD_DIM = 64
MOBA_HEADS = 8
MOBA_WIDTH = 512
MOBA_BLOCK = 256
MOBA_TOPK = 3
MOBA_UNROLL = 8
MOBA_ROWS = 32
GLA_HEADS = 4
GLA_VAL_DIM = 128
GLA_KEY_DIM = 64
GLA_VWIDTH = 512
GLA_KWIDTH = 256
GLA_GATE_RANK = 16
GLA_GATE_TAU = 16.0
GLA_CHUNK = 64
GLA_SUB = 16
D_FF = 2816
CONV_WIDTH = 3
ROPE_THETA = 10000.0
EPS = 1e-6

LANES = 128
LOG2_E = 1.4426950408889634
NEG_BIG = -1e30
EXP_CAP = 80.0

PROJ_TILE = 1024
GLA_TILE = 1024
FFN_TILE = 512
V7X_VMEM_BYTES = 64 * 1024 * 1024
VMEM_LIMIT = V7X_VMEM_BYTES - 8 * 1024 * 1024

f32 = jnp.float32
bf16 = jnp.bfloat16


def _dot(a, b):
    return jnp.dot(a, b, preferred_element_type=f32)


def _dot_nt(a, b):
    return lax.dot_general(a, b, (((1,), (1,)), ((), ())), preferred_element_type=f32)


def _dot_tn(a, b):
    return lax.dot_general(a, b, (((0,), (0,)), ((), ())), preferred_element_type=f32)


def _rms(xf, g):
    return xf * lax.rsqrt(jnp.mean(xf * xf, axis=-1, keepdims=True) + EPS) * g


def _proj_kernel(x_ref, g_ref, wqt_ref, wk_ref, wvt_ref, wg_ref, wgu_hi_ref, wgu_lo_ref, bg_ref,
                 cos_t_ref, sin_t_ref, cos_r_ref, sin_r_ref,
                 qt_ref, k_ref, vt_ref, kmean_ref, gq_ref, gk_ref, gv_ref, gr_ref, la_ref):
    T = x_ref.shape[1]
    xn = _rms(x_ref[0], g_ref[...]).astype(bf16)

    nblk = T // MOBA_BLOCK
    vg = _dot_nt(wvt_ref[...], xn)
    vt = vg[:MOBA_WIDTH].astype(bf16)
    for j in range(nblk):
        vt_ref[0, j] = vt[:, j * MOBA_BLOCK:(j + 1) * MOBA_BLOCK]

    gate_lr = vg[MOBA_WIDTH:]
    lr_hi = gate_lr.astype(bf16)
    lr_lo = (gate_lr - lr_hi.astype(f32)).astype(bf16)
    z = (_dot_tn(lr_hi, wgu_hi_ref[...]) + _dot_tn(lr_lo, wgu_hi_ref[...])
         + _dot_tn(lr_hi, wgu_lo_ref[...]) + bg_ref[...])
    log_sig = jnp.minimum(z, 0.0) - jnp.log1p(jnp.exp(-jnp.abs(z)))
    la_ref[0] = log_sig * (1.0 / GLA_GATE_TAU)

    qt = _dot_nt(wqt_ref[...], xn)
    cos_t = cos_t_ref[...]
    sin_t = sin_t_ref[...]
    half = MOBA_HEAD_DIM // 2
    scale = MOBA_HEAD_DIM ** -0.5 * LOG2_E
    for h in range(MOBA_HEADS):
        r0 = h * MOBA_HEAD_DIM
        t1 = qt[r0:r0 + half]
        t2 = qt[r0 + half:r0 + MOBA_HEAD_DIM]
        lo = ((t1 * cos_t - t2 * sin_t) * scale).astype(bf16)
        hi = ((t2 * cos_t + t1 * sin_t) * scale).astype(bf16)
        for j in range(nblk):
            qt_ref[0, j, r0:r0 + half, :] = lo[:, j * MOBA_BLOCK:(j + 1) * MOBA_BLOCK]
            qt_ref[0, j, r0 + half:r0 + MOBA_HEAD_DIM, :] = hi[:, j * MOBA_BLOCK:(j + 1) * MOBA_BLOCK]

    k = _dot(xn, wk_ref[...])
    cos_r = cos_r_ref[...]
    sin_r = sin_r_ref[...]
    lane = lax.broadcasted_iota(jnp.int32, (T, LANES), 1)
    first_half = (lane % MOBA_HEAD_DIM) < half
    for p in range(MOBA_WIDTH // LANES):
        kp = k[:, p * LANES:(p + 1) * LANES]
        rot = jnp.where(first_half, pltpu.roll(kp, LANES - half, 1), pltpu.roll(kp, half, 1))
        kr = kp * cos_r + rot * sin_r
        k_ref[0, :, p * LANES:(p + 1) * LANES] = kr.astype(bf16)
        for j in range(nblk):
            kmean_ref[0, 0, j:j + 1, p * LANES:(p + 1) * LANES] = jnp.mean(
                kr[j * MOBA_BLOCK:(j + 1) * MOBA_BLOCK], axis=0, keepdims=True)

    pg = _dot(xn, wg_ref[...])
    o_gk = GLA_KWIDTH
    o_gv = o_gk + GLA_KWIDTH
    o_gr = o_gv + GLA_VWIDTH
    gq_ref[0] = pg[:, :o_gk] * (GLA_KEY_DIM ** -0.5)
    gk_ref[0] = pg[:, o_gk:o_gv]
    gv_ref[0] = pg[:, o_gv:o_gr]
    gr_ref[0] = pg[:, o_gr:]


def _proj_call(x, g, wqt, wk, wvt, wg, wgu_hi, wgu_lo, bg, cos_t, sin_t, cos_r, sin_r):
    B, S, D = x.shape
    T = PROJ_TILE
    nb_t = T // MOBA_BLOCK
    const = lambda shape: pl.BlockSpec(shape, lambda b, i: (0,) * len(shape))
    tok = lambda w: pl.BlockSpec((1, T, w), lambda b, i: (b, i, 0))
    out_shape = (
        jax.ShapeDtypeStruct((B, S // MOBA_BLOCK, MOBA_WIDTH, MOBA_BLOCK), bf16),
        jax.ShapeDtypeStruct((B, S, MOBA_WIDTH), bf16),
        jax.ShapeDtypeStruct((B, S // MOBA_BLOCK, MOBA_WIDTH, MOBA_BLOCK), bf16),
        jax.ShapeDtypeStruct((B, S // T, nb_t, MOBA_WIDTH), f32),
        jax.ShapeDtypeStruct((B, S, GLA_KWIDTH), f32),
        jax.ShapeDtypeStruct((B, S, GLA_KWIDTH), f32),
        jax.ShapeDtypeStruct((B, S, GLA_VWIDTH), f32),
        jax.ShapeDtypeStruct((B, S, GLA_VWIDTH), f32),
        jax.ShapeDtypeStruct((B, S, GLA_KWIDTH), f32),
    )
    out_specs = (
        pl.BlockSpec((1, nb_t, MOBA_WIDTH, MOBA_BLOCK), lambda b, i: (b, i, 0, 0)),
        tok(MOBA_WIDTH),
        pl.BlockSpec((1, nb_t, MOBA_WIDTH, MOBA_BLOCK), lambda b, i: (b, i, 0, 0)),
        pl.BlockSpec((1, 1, nb_t, MOBA_WIDTH), lambda b, i: (b, i, 0, 0)),
        tok(GLA_KWIDTH), tok(GLA_KWIDTH), tok(GLA_VWIDTH), tok(GLA_VWIDTH), tok(GLA_KWIDTH),
    )
    in_specs = [
        tok(D),
        const((1, D)),
        const(wqt.shape), const(wk.shape), const(wvt.shape), const(wg.shape),
        const(wgu_hi.shape), const(wgu_lo.shape), const((1, GLA_KWIDTH)),
        pl.BlockSpec((MOBA_HEAD_DIM // 2, T), lambda b, i: (0, i)),
        pl.BlockSpec((MOBA_HEAD_DIM // 2, T), lambda b, i: (0, i)),
        pl.BlockSpec((T, LANES), lambda b, i: (i, 0)),
        pl.BlockSpec((T, LANES), lambda b, i: (i, 0)),
    ]
    return pl.pallas_call(
        _proj_kernel,
        grid=(B, S // T),
        in_specs=in_specs,
        out_specs=out_specs,
        out_shape=out_shape,
        compiler_params=pltpu.CompilerParams(
            dimension_semantics=("parallel", "parallel"), vmem_limit_bytes=VMEM_LIMIT),
        name="in_proj",
    )(x, g, wqt, wk, wvt, wg, wgu_hi, wgu_lo, bg, cos_t, sin_t, cos_r, sin_r)


def _moba_kernel(qt_ref, k_ref, vt_ref, kmean_ref, o_ref, s_ref, p_ref, own_ref):
    NT = qt_ref.shape[1]
    QB = qt_ref.shape[3]
    NB = kmean_ref.shape[1]
    HD = MOBA_HEAD_DIM
    U = MOBA_UNROLL
    BLK = MOBA_BLOCK
    ROWS = MOBA_ROWS

    km = kmean_ref[0]
    km_hi = km.astype(bf16)
    km_lo = (km - km_hi.astype(f32)).astype(bf16)
    blk = lax.broadcasted_iota(jnp.int32, (NB, QB), 0)
    zeros_half = jnp.zeros((HD, QB), bf16)
    zeros_tail = jnp.zeros((LANES - NB, QB), bf16)
    in_causal = (lax.broadcasted_iota(jnp.int32, (BLK, QB), 0)
                 <= lax.broadcasted_iota(jnp.int32, (BLK, QB), 1))
    lane8 = lax.broadcasted_iota(jnp.int32, (8, LANES), 1)

    def select(qi):
        qzs = []
        for h in range(2):
            q_h = qt_ref[0, qi, h * HD:(h + 1) * HD, :]
            qz = jnp.concatenate([q_h, zeros_half] if h == 0 else [zeros_half, q_h], axis=0)
            gate = _dot(km_hi, qz) + _dot(km_lo, qz)
            gate = jnp.where(blk < qi, gate, -jnp.inf)
            sel = jnp.zeros((NB, QB), jnp.bool_)
            for _ in range(MOBA_TOPK):
                top = jnp.max(gate, axis=0, keepdims=True)
                idx = jnp.min(jnp.where(gate == top, blk, NB), axis=0, keepdims=True)
                pick = blk == idx
                sel = jnp.logical_or(sel, jnp.logical_and(pick, top > -jnp.inf))
                gate = jnp.where(pick, -jnp.inf, gate)
            bias = jnp.where(sel, 0.0, NEG_BIG).astype(bf16)
            qzs.append(jnp.concatenate([qz, bias, zeros_tail], axis=0))
        return tuple(qzs)

    def block_scores(qzs, j, u, slot, tops):
        r0 = pl.multiple_of(j * BLK, BLK)
        onehot = jnp.tile(jnp.where(lane8 == j, 1.0, 0.0), (BLK // 8, 1)).astype(bf16)
        k_aug = jnp.concatenate([k_ref[0, pl.ds(r0, BLK), :], onehot], axis=1)
        out = []
        for h in range(2):
            s = _dot(k_aug, qzs[h])
            s_ref[slot, h, u * BLK:(u + 1) * BLK, :] = s
            top = jnp.max(s, axis=0, keepdims=True)
            out.append(top if tops is None else jnp.maximum(tops[h], top))
        return tuple(out)

    def own_scores(qzs, qi):
        k_own = k_ref[0, pl.ds(pl.multiple_of(qi * BLK, BLK), BLK), :]
        out = []
        for h in range(2):
            s = jnp.where(in_causal, _dot(k_own, qzs[h][:LANES]), NEG_BIG)
            own_ref[qi % 2, h] = s
            out.append(jnp.max(s, axis=0, keepdims=True))
        return tuple(out)

    def block_softmax(load_rows, u, m_new, part, h):
        for r in range(u * BLK, (u + 1) * BLK, ROWS):
            e = jnp.exp2(load_rows(r) - m_new)
            part = part + e
            p_ref[h, r:r + ROWS, :] = e.astype(bf16)
        return part

    def block_values(j, u, pv, h):
        d = _dot(vt_ref[0, j, h * HD:(h + 1) * HD, :], p_ref[h, u * BLK:(u + 1) * BLK, :])
        return d if pv is None else pv + d

    def group(t, tops, mls, accs, load_rows, prefetch):
        m_new = [jnp.maximum(mls[h][0], tops[h]) for h in range(2)]
        alpha = [jnp.exp2(mls[h][0] - m_new[h]) for h in range(2)]
        part = [jnp.zeros((ROWS, QB), f32)] * 2
        pv = [None, None]
        tops_next = None
        for u in range(U):
            tops_next = prefetch(u, tops_next)
            for h in range(2):
                part[h] = block_softmax(load_rows[h], u, m_new[h], part[h], h)
            for h in range(2):
                pv[h] = block_values(t * U + u, u, pv[h], h)
        mls = tuple((m_new[h], alpha[h] * mls[h][1] + jnp.sum(part[h], axis=0, keepdims=True))
                    for h in range(2))
        accs = tuple(alpha[h] * accs[h] + pv[h] for h in range(2))
        return tops_next, mls, accs

    def slot_rows(slot):
        return [lambda r, h=h: s_ref[slot, h, r:r + ROWS, :] for h in range(2)]

    ml0 = (jnp.full((1, QB), -jnp.inf, f32), jnp.zeros((1, QB), f32))
    acc0 = jnp.zeros((HD, QB), f32)

    def tile(qi, carry):
        qzs, qzs_next, own, tops0, phase = carry
        n_groups = jnp.maximum((qi + U - 1) // U, 1)

        def step(t, c, cur):
            tops, mls, accs = c
            return group(t, tops, mls, accs, slot_rows(cur),
                         lambda u, tn: block_scores(qzs, (t + 1) * U + u, u, 1 - cur, tn))

        def body(t, c):
            return lax.cond((t + phase) % 2 == 0,
                            functools.partial(step, t, cur=0), functools.partial(step, t, cur=1), c)

        state = lax.fori_loop(0, n_groups - 1, body, (tops0, (ml0, ml0), (acc0, acc0)))

        def boundary(c, cur):
            tops, mls, accs = c
            m_new = [jnp.maximum(mls[h][0], jnp.maximum(tops[h], own[h])) for h in range(2)]
            alpha = [jnp.exp2(mls[h][0] - m_new[h]) for h in range(2)]
            part = [jnp.zeros((ROWS, QB), f32)] * 2
            pv = [None, None]
            tops_next = None
            load_rows = slot_rows(cur)
            q_next = jnp.minimum(qi + 1, NT - 1)
            for u in range(U + 1):
                if u < U:
                    tops_next = block_scores(qzs_next, u, u, 1 - cur, tops_next)
                else:
                    own_next = own_scores(qzs_next, q_next)
                for h in range(2):
                    if u == 0:
                        for r in range(0, BLK, ROWS):
                            e = jnp.exp2(own_ref[qi % 2, h, r:r + ROWS, :] - m_new[h])
                            part[h] = part[h] + e
                            p_ref[h, U * BLK + r:U * BLK + r + ROWS, :] = e.astype(bf16)
                    else:
                        part[h] = block_softmax(load_rows[h], u - 1, m_new[h], part[h], h)
                for h in range(2):
                    if u == 0:
                        pv[h] = _dot(vt_ref[0, qi, h * HD:(h + 1) * HD, :],
                                     p_ref[h, U * BLK:(U + 1) * BLK, :])
                    else:
                        pv[h] = block_values((n_groups - 1) * U + u - 1, u - 1, pv[h], h)
            qzs_after = select(jnp.minimum(qi + 2, NT - 1))
            outs = []
            for h in range(2):
                l = alpha[h] * mls[h][1] + jnp.sum(part[h], axis=0, keepdims=True)
                outs.append((alpha[h] * accs[h] + pv[h]) / l)
            o_t = jnp.concatenate(outs, axis=0)
            r0 = pl.multiple_of(qi * BLK, BLK)
            o_ref[0, pl.ds(r0, BLK), :] = o_t.T.astype(o_ref.dtype)
            return qzs_after, own_next, tops_next

        last_slot = (n_groups - 1 + phase) % 2
        qzs_after, own_next, tops_next = lax.cond(
            last_slot == 0, functools.partial(boundary, cur=0), functools.partial(boundary, cur=1),
            state)
        return qzs_next, qzs_after, own_next, tops_next, 1 - last_slot

    qzs0 = select(0)
    qzs1 = select(1)
    tops0 = None
    for u in range(U):
        tops0 = block_scores(qzs0, u, u, 0, tops0)
    lax.fori_loop(0, NT, tile, (qzs0, qzs1, own_scores(qzs0, 0), tops0, jnp.int32(0)))


def _moba_call(qt, k, vt, kmean):
    B, S, W = k.shape
    NB = S // MOBA_BLOCK
    QB = MOBA_BLOCK
    n_hp = W // LANES
    assert NB % MOBA_UNROLL == 0 and NB <= LANES
    return pl.pallas_call(
        _moba_kernel,
        grid=(B, n_hp),
        in_specs=[
            pl.BlockSpec((1, NB, LANES, QB), lambda b, hp: (b, 0, hp, 0)),
            pl.BlockSpec((1, S, LANES), lambda b, hp: (b, 0, hp)),
            pl.BlockSpec((1, NB, LANES, MOBA_BLOCK), lambda b, hp: (b, 0, hp, 0)),
            pl.BlockSpec((1, NB, LANES), lambda b, hp: (b, 0, hp)),
        ],
        out_specs=pl.BlockSpec((1, S, LANES), lambda b, hp: (b, 0, hp)),
        out_shape=jax.ShapeDtypeStruct((B, S, W), bf16),
        scratch_shapes=[
            pltpu.VMEM((2, 2, MOBA_UNROLL * MOBA_BLOCK, QB), f32),
            pltpu.VMEM((2, (MOBA_UNROLL + 1) * MOBA_BLOCK, QB), bf16),
            pltpu.VMEM((2, 2, MOBA_BLOCK, QB), f32),
        ],
        compiler_params=pltpu.CompilerParams(
            dimension_semantics=("parallel", "parallel"),
            vmem_limit_bytes=VMEM_LIMIT),
        name="moba",
    )(qt, k, vt, kmean)


def _split3(a):
    hi = a.astype(bf16)
    r1 = a - hi.astype(f32)
    mid = r1.astype(bf16)
    lo = (r1 - mid.astype(f32)).astype(bf16)
    return hi, mid, lo


def _gla_kernel(gq_ref, gk_ref, gv_ref, gr_ref, la_ref, gn_ref, y_ref, st_ref):
    C = GLA_CHUNK
    H = GLA_HEADS
    KW = GLA_KWIDTH
    DV = GLA_VAL_DIM
    NSUB = C // GLA_SUB
    T = gq_ref.shape[1]

    @pl.when(pl.program_id(1) == 0)
    def _():
        st_ref[...] = jnp.zeros_like(st_ref)

    row = lax.broadcasted_iota(jnp.int32, (C, C), 0)
    col = lax.broadcasted_iota(jnp.int32, (C, C), 1)
    tril = (col <= row).astype(bf16)
    lane_head = lax.broadcasted_iota(jnp.int32, (C, KW), 1) // GLA_KEY_DIM
    rt = lax.broadcasted_iota(jnp.int32, (C, NSUB * C), 0)
    rc = lax.broadcasted_iota(jnp.int32, (C, NSUB * C), 1)
    keep = jnp.logical_and(rc // C == rt // GLA_SUB, rc % C <= rt)
    st_lane_head = lax.broadcasted_iota(jnp.int32, (DV, KW), 1) // GLA_KEY_DIM

    n_chunks = T // C
    chunk_rows = [slice(c * C, (c + 1) * C) for c in range(n_chunks)]

    Gs = []
    for rows in chunk_rows:
        hi, mid, lo = _split3(la_ref[0, rows, :])
        Gs.append(_dot(tril, hi) + _dot(tril, mid) + _dot(tril, lo))

    rs, upds, qz_sts, decays, vs = [], [], [], [], []
    for rows, G in zip(chunk_rows, Gs):
        q = gq_ref[0, rows, :]
        k = gk_ref[0, rows, :]
        v = gv_ref[0, rows, :].astype(bf16)
        g_last = G[C - 1:C, :]
        g_ref_rows = [G[i * GLA_SUB:i * GLA_SUB + 1, :] for i in range(NSUB)]
        g_own = jnp.concatenate(
            [jnp.broadcast_to(g, (GLA_SUB, KW)) for g in g_ref_rows], axis=0)
        q_in = q * jnp.exp(G - g_own)
        k_in = jnp.concatenate(
            [(k * jnp.exp(jnp.minimum(g - G, EXP_CAP))).astype(bf16) for g in g_ref_rows],
            axis=0)
        q_st = q * jnp.exp(G)
        k_st = (k * jnp.exp(g_last - G)).astype(bf16)
        qz_in = jnp.concatenate(
            [jnp.where(lane_head == h, q_in, 0.0).astype(bf16) for h in range(H)], axis=0)
        qz_sts.append(jnp.concatenate(
            [jnp.where(lane_head == h, q_st, 0.0).astype(bf16) for h in range(H)], axis=0))
        rs.append(_dot_nt(qz_in, k_in))
        upds.append(_dot_tn(v, k_st))
        decays.append(jnp.exp(g_last))
        vs.append(v)

    o_intras = []
    for r, v in zip(rs, vs):
        per_head = []
        for h in range(H):
            r_h = jnp.where(keep, r[h * C:(h + 1) * C, :], 0.0).astype(bf16)
            v_rep = jnp.concatenate([v[:, h * DV:(h + 1) * DV]] * NSUB, axis=0)
            per_head.append(_dot(r_h, v_rep))
        o_intras.append(per_head)

    st = st_ref[...]
    o_inters = []
    for qz_st, upd, decay in zip(qz_sts, upds, decays):
        o_inters.append(_dot_nt(qz_st, st.astype(bf16)))
        st = st * decay
        for h in range(H):
            st = st + jnp.where(st_lane_head == h, upd[h * DV:(h + 1) * DV, :], 0.0)
    st_ref[...] = st

    for rows, o_intra, o_inter in zip(chunk_rows, o_intras, o_inters):
        for h in range(H):
            o = o_intra[h] + o_inter[h * C:(h + 1) * C, :]
            o = o * lax.rsqrt(jnp.mean(o * o, axis=-1, keepdims=True) + EPS)
            o = o * gn_ref[:, h * DV:(h + 1) * DV]
            gr = gr_ref[0, rows, h * DV:(h + 1) * DV]
            y = o * (gr * jax.nn.sigmoid(gr))
            y_ref[0, rows, h * DV:(h + 1) * DV] = y.astype(y_ref.dtype)


def _gla_call(gq, gk, gv, gr, la, gn):
    B, S, _ = gq.shape
    T = GLA_TILE
    tok = lambda w: pl.BlockSpec((1, T, w), lambda b, i: (b, i, 0))
    return pl.pallas_call(
        _gla_kernel,
        grid=(B, S // T),
        in_specs=[tok(GLA_KWIDTH), tok(GLA_KWIDTH), tok(GLA_VWIDTH), tok(GLA_VWIDTH),
                  tok(GLA_KWIDTH), pl.BlockSpec((1, GLA_VWIDTH), lambda b, i: (0, 0))],
        out_specs=tok(GLA_VWIDTH),
        out_shape=jax.ShapeDtypeStruct((B, S, GLA_VWIDTH), bf16),
        scratch_shapes=[pltpu.VMEM((GLA_VAL_DIM, GLA_KWIDTH), f32)],
        compiler_params=pltpu.CompilerParams(
            dimension_semantics=("parallel", "arbitrary"), vmem_limit_bytes=VMEM_LIMIT),
        name="gla",
    )(gq, gk, gv, gr, la, gn)


def _ffn_kernel(x_ref, ym_ref, yg_ref, wo_m_ref, wo_g_ref, fg_ref, wup_ref, cw_ref, cb_ref,
                wdn_ref, og_ref, out_ref, u_ref):
    T = x_ref.shape[1]
    PAD = 8

    @pl.when(pl.program_id(1) == 0)
    def _():
        u_ref[0:PAD, :] = jnp.zeros((PAD, u_ref.shape[1]), f32)

    h = x_ref[0] + _dot(ym_ref[0], wo_m_ref[...]) + _dot(yg_ref[0], wo_g_ref[...])
    hn = _rms(h, fg_ref[...]).astype(bf16)
    u_ref[PAD:PAD + T, :] = _dot(hn, wup_ref[...])
    cw = cw_ref[...]
    conv = (cw[0:1] * u_ref[PAD - 2:PAD - 2 + T, :]
            + cw[1:2] * u_ref[PAD - 1:PAD - 1 + T, :]
            + cw[2:3] * u_ref[PAD:PAD + T, :]
            + cb_ref[...])
    u_ref[0:PAD, :] = u_ref[T:T + PAD, :]
    hg = conv[:, :D_FF]
    act = (hg * jax.nn.sigmoid(hg) * conv[:, D_FF:]).astype(bf16)
    y = h + _dot(act, wdn_ref[...])
    out_ref[0] = _rms(y, og_ref[...])


def _ffn_call(x, ym, yg, wo_m, wo_g, fg, wup, cw, cb, wdn, og):
    B, S, D = x.shape
    T = FFN_TILE
    tok = lambda w: pl.BlockSpec((1, T, w), lambda b, i: (b, i, 0))
    const = lambda a: pl.BlockSpec(a.shape, lambda b, i: (0,) * a.ndim, pipeline_mode=pl.Buffered(1))
    return pl.pallas_call(
        _ffn_kernel,
        grid=(B, S // T),
        in_specs=[tok(D), tok(MOBA_WIDTH), tok(GLA_VWIDTH),
                  const(wo_m), const(wo_g), const(fg), const(wup), const(cw), const(cb),
                  const(wdn), const(og)],
        out_specs=tok(D),
        out_shape=jax.ShapeDtypeStruct((B, S, D), x.dtype),
        scratch_shapes=[pltpu.VMEM((T + 8, 2 * D_FF), f32)],
        compiler_params=pltpu.CompilerParams(
            dimension_semantics=("parallel", "arbitrary"), vmem_limit_bytes=VMEM_LIMIT),
        name="out_ffn",
    )(x, ym, yg, wo_m, wo_g, fg, wup, cw, cb, wdn, og)


def _rope_tables(S):
    hd = MOBA_HEAD_DIM
    inv_freq = 1.0 / (ROPE_THETA ** (jnp.arange(0, hd, 2, dtype=f32) / hd))
    ang = jnp.arange(S).astype(f32)[:, None] * inv_freq[None, :]
    cos, sin = jnp.cos(ang), jnp.sin(ang)
    cos_r = jnp.tile(cos, (1, LANES // (hd // 2)))
    sin_r = jnp.tile(jnp.concatenate([-sin, sin], axis=1), (1, LANES // hd))
    return cos.T, sin.T, cos_r, sin_r


def kernel(x, attn_norm_g, w_in, w_gate_up, b_gate, gla_norm_g, w_out, ffn_norm_g, w_ffn_up,
           conv_w, conv_b, w_ffn_down, final_norm_g):
    B, S, D = x.shape
    assert w_in.shape[0] == 1, "single-layer block"
    assert D == D_MODEL and conv_w.shape[1] == CONV_WIDTH and w_ffn_down.shape[1] == D_FF
    assert S % PROJ_TILE == 0 and S % GLA_TILE == 0 and S % FFN_TILE == 0
    l = 0
    o_mk = MOBA_WIDTH
    o_mv = 2 * MOBA_WIDTH
    o_gq = 3 * MOBA_WIDTH
    o_gg = o_gq + 2 * GLA_KWIDTH + 2 * GLA_VWIDTH
    w = w_in[l]
    wqt = w[:, :o_mk].T.astype(bf16)
    wk = w[:, o_mk:o_mv].astype(bf16)
    wvt = jnp.concatenate([w[:, o_mv:o_gq], w[:, o_gg:]], axis=1).T.astype(bf16)
    wg = w[:, o_gq:o_gg].astype(bf16)
    wgu = w_gate_up[l]
    wgu_hi = wgu.astype(bf16)
    wgu_lo = (wgu - wgu_hi.astype(f32)).astype(bf16)
    cos_t, sin_t, cos_r, sin_r = _rope_tables(S)

    qt, k, vt, kmean, gq, gk, gv, gr, la = _proj_call(
        x, attn_norm_g[l][None, :], wqt, wk, wvt, wg, wgu_hi, wgu_lo, b_gate[l][None, :],
        cos_t, sin_t, cos_r, sin_r)
    kmean = kmean.reshape(B, S // MOBA_BLOCK, MOBA_WIDTH)

    y_moba = _moba_call(qt, k, vt, kmean)
    y_gla = _gla_call(gq, gk, gv, gr, la, gla_norm_g[l].reshape(1, GLA_VWIDTH))

    wo = w_out[l].astype(bf16)
    return _ffn_call(
        x, y_moba, y_gla, wo[:MOBA_WIDTH], wo[MOBA_WIDTH:], ffn_norm_g[l][None, :],
        w_ffn_up[l].astype(bf16), conv_w[l], conv_b[l][None, :], w_ffn_down[l].astype(bf16),
        final_norm_g[None, :])
```

```python
import functools

import jax
import jax.numpy as jnp
from jax import lax
from jax.experimental import pallas as pl
from jax.experimental.pallas import tpu as pltpu

D_MODEL = 1024
MOBA_HEAD_DIM = 64
MOBA_HEADS = 8
MOBA_WIDTH = 512
MOBA_BLOCK = 256
MOBA_TOPK = 3
MOBA_UNROLL = 8
MOBA_ROWS = 32
GLA_HEADS = 4
GLA_VAL_DIM = 128
GLA_KEY_DIM = 64
GLA_VWIDTH = 512
GLA_KWIDTH = 256
GLA_GATE_RANK = 16
GLA_GATE_TAU = 16.0
GLA_CHUNK = 64
GLA_SUB = 16
D_FF = 2816
CONV_WIDTH = 3
ROPE_THETA = 10000.0
EPS = 1e-6

LANES = 128
LOG2_E = 1.4426950408889634
NEG_BIG = -1e30
EXP_CAP = 80.0

PROJ_TILE = 1024
GLA_TILE = 1024
FFN_TILE = 512
V7X_VMEM_BYTES = 64 * 1024 * 1024
VMEM_LIMIT = V7X_VMEM_BYTES - 8 * 1024 * 1024

f32 = jnp.float32
bf16 = jnp.bfloat16


def _dot(a, b):
    return jnp.dot(a, b, preferred_element_type=f32)


def _dot_nt(a, b):
    return lax.dot_general(a, b, (((1,), (1,)), ((), ())), preferred_element_type=f32)


def _dot_tn(a, b):
    return lax.dot_general(a, b, (((0,), (0,)), ((), ())), preferred_element_type=f32)


def _rms(xf, g):
    return xf * lax.rsqrt(jnp.mean(xf * xf, axis=-1, keepdims=True) + EPS) * g


def _proj_kernel(x_ref, g_ref, wqt_ref, wk_ref, wvt_ref, wg_ref, wgu_hi_ref, wgu_lo_ref, bg_ref,
                 cos_t_ref, sin_t_ref, cos_r_ref, sin_r_ref,
                 qt_ref, k_ref, vt_ref, kmean_ref, gq_ref, gk_ref, gv_ref, gr_ref, la_ref):
    T = x_ref.shape[1]
    xn = _rms(x_ref[0], g_ref[...]).astype(bf16)

    nblk = T // MOBA_BLOCK
    vg = _dot_nt(wvt_ref[...], xn)
    vt = vg[:MOBA_WIDTH].astype(bf16)
    for j in range(nblk):
        vt_ref[0, j] = vt[:, j * MOBA_BLOCK:(j + 1) * MOBA_BLOCK]

    gate_lr = vg[MOBA_WIDTH:]
    lr_hi = gate_lr.astype(bf16)
    lr_lo = (gate_lr - lr_hi.astype(f32)).astype(bf16)
    z = (_dot_tn(lr_hi, wgu_hi_ref[...]) + _dot_tn(lr_lo, wgu_hi_ref[...])
         + _dot_tn(lr_hi, wgu_lo_ref[...]) + bg_ref[...])
    log_sig = jnp.minimum(z, 0.0) - jnp.log1p(jnp.exp(-jnp.abs(z)))
    la_ref[0] = log_sig * (1.0 / GLA_GATE_TAU)

    qt = _dot_nt(wqt_ref[...], xn)
    cos_t = cos_t_ref[...]
    sin_t = sin_t_ref[...]
    half = MOBA_HEAD_DIM // 2
    scale = MOBA_HEAD_DIM ** -0.5 * LOG2_E
    for h in range(MOBA_HEADS):
        r0 = h * MOBA_HEAD_DIM
        t1 = qt[r0:r0 + half]
        t2 = qt[r0 + half:r0 + MOBA_HEAD_DIM]
        lo = ((t1 * cos_t - t2 * sin_t) * scale).astype(bf16)
        hi = ((t2 * cos_t + t1 * sin_t) * scale).astype(bf16)
        for j in range(nblk):
            qt_ref[0, j, r0:r0 + half, :] = lo[:, j * MOBA_BLOCK:(j + 1) * MOBA_BLOCK]
            qt_ref[0, j, r0 + half:r0 + MOBA_HEAD_DIM, :] = hi[:, j * MOBA_BLOCK:(j + 1) * MOBA_BLOCK]

    k = _dot(xn, wk_ref[...])
    cos_r = cos_r_ref[...]
    sin_r = sin_r_ref[...]
    lane = lax.broadcasted_iota(jnp.int32, (T, LANES), 1)
    first_half = (lane % MOBA_HEAD_DIM) < half
    for p in range(MOBA_WIDTH // LANES):
        kp = k[:, p * LANES:(p + 1) * LANES]
        rot = jnp.where(first_half, pltpu.roll(kp, LANES - half, 1), pltpu.roll(kp, half, 1))
        kr = kp * cos_r + rot * sin_r
        k_ref[0, :, p * LANES:(p + 1) * LANES] = kr.astype(bf16)
        for j in range(nblk):
            kmean_ref[0, 0, j:j + 1, p * LANES:(p + 1) * LANES] = jnp.mean(
                kr[j * MOBA_BLOCK:(j + 1) * MOBA_BLOCK], axis=0, keepdims=True)

    pg = _dot(xn, wg_ref[...])
    o_gk = GLA_KWIDTH
    o_gv = o_gk + GLA_KWIDTH
    o_gr = o_gv + GLA_VWIDTH
    gq_ref[0] = pg[:, :o_gk] * (GLA_KEY_DIM ** -0.5)
    gk_ref[0] = pg[:, o_gk:o_gv]
    gv_ref[0] = pg[:, o_gv:o_gr]
    gr_ref[0] = pg[:, o_gr:]


def _proj_call(x, g, wqt, wk, wvt, wg, wgu_hi, wgu_lo, bg, cos_t, sin_t, cos_r, sin_r):
    B, S, D = x.shape
    T = PROJ_TILE
    nb_t = T // MOBA_BLOCK
    const = lambda shape: pl.BlockSpec(shape, lambda b, i: (0,) * len(shape))
    tok = lambda w: pl.BlockSpec((1, T, w), lambda b, i: (b, i, 0))
    out_shape = (
        jax.ShapeDtypeStruct((B, S // MOBA_BLOCK, MOBA_WIDTH, MOBA_BLOCK), bf16),
        jax.ShapeDtypeStruct((B, S, MOBA_WIDTH), bf16),
        jax.ShapeDtypeStruct((B, S // MOBA_BLOCK, MOBA_WIDTH, MOBA_BLOCK), bf16),
        jax.ShapeDtypeStruct((B, S // T, nb_t, MOBA_WIDTH), f32),
        jax.ShapeDtypeStruct((B, S, GLA_KWIDTH), f32),
        jax.ShapeDtypeStruct((B, S, GLA_KWIDTH), f32),
        jax.ShapeDtypeStruct((B, S, GLA_VWIDTH), f32),
        jax.ShapeDtypeStruct((B, S, GLA_VWIDTH), f32),
        jax.ShapeDtypeStruct((B, S, GLA_KWIDTH), f32),
    )
    out_specs = (
        pl.BlockSpec((1, nb_t, MOBA_WIDTH, MOBA_BLOCK), lambda b, i: (b, i, 0, 0)),
        tok(MOBA_WIDTH),
        pl.BlockSpec((1, nb_t, MOBA_WIDTH, MOBA_BLOCK), lambda b, i: (b, i, 0, 0)),
        pl.BlockSpec((1, 1, nb_t, MOBA_WIDTH), lambda b, i: (b, i, 0, 0)),
        tok(GLA_KWIDTH), tok(GLA_KWIDTH), tok(GLA_VWIDTH), tok(GLA_VWIDTH), tok(GLA_KWIDTH),
    )
    in_specs = [
        tok(D),
        const((1, D)),
        const(wqt.shape), const(wk.shape), const(wvt.shape), const(wg.shape),
        const(wgu_hi.shape), const(wgu_lo.shape), const((1, GLA_KWIDTH)),
        pl.BlockSpec((MOBA_HEAD_DIM // 2, T), lambda b, i: (0, i)),
        pl.BlockSpec((MOBA_HEAD_DIM // 2, T), lambda b, i: (0, i)),
        pl.BlockSpec((T, LANES), lambda b, i: (i, 0)),
        pl.BlockSpec((T, LANES), lambda b, i: (i, 0)),
    ]
    return pl.pallas_call(
        _proj_kernel,
        grid=(B, S // T),
        in_specs=in_specs,
        out_specs=out_specs,
        out_shape=out_shape,
        compiler_params=pltpu.CompilerParams(
            dimension_semantics=("parallel", "parallel"), vmem_limit_bytes=VMEM_LIMIT),
        name="in_proj",
    )(x, g, wqt, wk, wvt, wg, wgu_hi, wgu_lo, bg, cos_t, sin_t, cos_r, sin_r)


def _moba_kernel(qt_ref, k_ref, vt_ref, kmean_ref, o_ref, s_ref, p_ref, own_ref):
    NT = qt_ref.shape[1]
    QB = qt_ref.shape[3]
    NB = kmean_ref.shape[1]
    HD = MOBA_HEAD_DIM
    U = MOBA_UNROLL
    BLK = MOBA_BLOCK
    ROWS = MOBA_ROWS

    km = kmean_ref[0]
    km_hi = km.astype(bf16)
    km_lo = (km - km_hi.astype(f32)).astype(bf16)
    blk = lax.broadcasted_iota(jnp.int32, (NB, QB), 0)
    zeros_half = jnp.zeros((HD, QB), bf16)
    zeros_tail = jnp.zeros((LANES - NB, QB), bf16)
    in_causal = (lax.broadcasted_iota(jnp.int32, (BLK, QB), 0)
                 <= lax.broadcasted_iota(jnp.int32, (BLK, QB), 1))
    lane8 = lax.broadcasted_iota(jnp.int32, (8, LANES), 1)

    def select(qi):
        qzs = []
        for h in range(2):
            q_h = qt_ref[0, qi, h * HD:(h + 1) * HD, :]
            qz = jnp.concatenate([q_h, zeros_half] if h == 0 else [zeros_half, q_h], axis=0)
            gate = _dot(km_hi, qz) + _dot(km_lo, qz)
            gate = jnp.where(blk < qi, gate, -jnp.inf)
            sel = jnp.zeros((NB, QB), jnp.bool_)
            for _ in range(MOBA_TOPK):
                top = jnp.max(gate, axis=0, keepdims=True)
                idx = jnp.min(jnp.where(gate == top, blk, NB), axis=0, keepdims=True)
                pick = blk == idx
                sel = jnp.logical_or(sel, jnp.logical_and(pick, top > -jnp.inf))
                gate = jnp.where(pick, -jnp.inf, gate)
            bias = jnp.where(sel, 0.0, NEG_BIG).astype(bf16)
            qzs.append(jnp.concatenate([qz, bias, zeros_tail], axis=0))
        return tuple(qzs)

    def block_scores(qzs, j, u, slot, tops):
        r0 = pl.multiple_of(j * BLK, BLK)
        onehot = jnp.tile(jnp.where(lane8 == j, 1.0, 0.0), (BLK // 8, 1)).astype(bf16)
        k_aug = jnp.concatenate([k_ref[0, pl.ds(r0, BLK), :], onehot], axis=1)
        out = []
        for h in range(2):
            s = _dot(k_aug, qzs[h])
            s_ref[slot, h, u * BLK:(u + 1) * BLK, :] = s
            top = jnp.max(s, axis=0, keepdims=True)
            out.append(top if tops is None else jnp.maximum(tops[h], top))
        return tuple(out)

    def own_scores(qzs, qi):
        k_own = k_ref[0, pl.ds(pl.multiple_of(qi * BLK, BLK), BLK), :]
        out = []
        for h in range(2):
            s = jnp.where(in_causal, _dot(k_own, qzs[h][:LANES]), NEG_BIG)
            own_ref[qi % 2, h] = s
            out.append(jnp.max(s, axis=0, keepdims=True))
        return tuple(out)

    def block_softmax(load_rows, u, m_new, part, h):
        for r in range(u * BLK, (u + 1) * BLK, ROWS):
            e = jnp.exp2(load_rows(r) - m_new)
            part = part + e
            p_ref[h, r:r + ROWS, :] = e.astype(bf16)
        return part

    def block_values(j, u, pv, h):
        d = _dot(vt_ref[0, j, h * HD:(h + 1) * HD, :], p_ref[h, u * BLK:(u + 1) * BLK, :])
        return d if pv is None else pv + d

    def group(t, tops, mls, accs, load_rows, prefetch):
        m_new = [jnp.maximum(mls[h][0], tops[h]) for h in range(2)]
        alpha = [jnp.exp2(mls[h][0] - m_new[h]) for h in range(2)]
        part = [jnp.zeros((ROWS, QB), f32)] * 2
        pv = [None, None]
        tops_next = None
        for u in range(U):
            tops_next = prefetch(u, tops_next)
            for h in range(2):
                part[h] = block_softmax(load_rows[h], u, m_new[h], part[h], h)
            for h in range(2):
                pv[h] = block_values(t * U + u, u, pv[h], h)
        mls = tuple((m_new[h], alpha[h] * mls[h][1] + jnp.sum(part[h], axis=0, keepdims=True))
                    for h in range(2))
        accs = tuple(alpha[h] * accs[h] + pv[h] for h in range(2))
        return tops_next, mls, accs

    def slot_rows(slot):
        return [lambda r, h=h: s_ref[slot, h, r:r + ROWS, :] for h in range(2)]

    ml0 = (jnp.full((1, QB), -jnp.inf, f32), jnp.zeros((1, QB), f32))
    acc0 = jnp.zeros((HD, QB), f32)

    def tile(qi, carry):
        qzs, qzs_next, own, tops0, phase = carry
        n_groups = jnp.maximum((qi + U - 1) // U, 1)

        def step(t, c, cur):
            tops, mls, accs = c
            return group(t, tops, mls, accs, slot_rows(cur),
                         lambda u, tn: block_scores(qzs, (t + 1) * U + u, u, 1 - cur, tn))

        def body(t, c):
            return lax.cond((t + phase) % 2 == 0,
                            functools.partial(step, t, cur=0), functools.partial(step, t, cur=1), c)

        state = lax.fori_loop(0, n_groups - 1, body, (tops0, (ml0, ml0), (acc0, acc0)))

        def boundary(c, cur):
            tops, mls, accs = c
            m_new = [jnp.maximum(mls[h][0], jnp.maximum(tops[h], own[h])) for h in range(2)]
            alpha = [jnp.exp2(mls[h][0] - m_new[h]) for h in range(2)]
            part = [jnp.zeros((ROWS, QB), f32)] * 2
            pv = [None, None]
            tops_next = None
            load_rows = slot_rows(cur)
            q_next = jnp.minimum(qi + 1, NT - 1)
            for u in range(U + 1):
                if u < U:
                    tops_next = block_scores(qzs_next, u, u, 1 - cur, tops_next)
                else:
                    own_next = own_scores(qzs_next, q_next)
                for h in range(2):
                    if u == 0:
                        for r in range(0, BLK, ROWS):
                            e = jnp.exp2(own_ref[qi % 2, h, r:r + ROWS, :] - m_new[h])
                            part[h] = part[h] + e
                            p_ref[h, U * BLK + r:U * BLK + r + ROWS, :] = e.astype(bf16)
                    else:
                        part[h] = block_softmax(load_rows[h], u - 1, m_new[h], part[h], h)
                for h in range(2):
                    if u == 0:
                        pv[h] = _dot(vt_ref[0, qi, h * HD:(h + 1) * HD, :],
                                     p_ref[h, U * BLK:(U + 1) * BLK, :])
                    else:
                        pv[h] = block_values((n_groups - 1) * U + u - 1, u - 1, pv[h], h)
            qzs_after = select(jnp.minimum(qi + 2, NT - 1))
            outs = []
            for h in range(2):
                l = alpha[h] * mls[h][1] + jnp.sum(part[h], axis=0, keepdims=True)
                outs.append((alpha[h] * accs[h] + pv[h]) / l)
            o_t = jnp.concatenate(outs, axis=0)
            r0 = pl.multiple_of(qi * BLK, BLK)
            o_ref[0, pl.ds(r0, BLK), :] = o_t.T.astype(o_ref.dtype)
            return qzs_after, own_next, tops_next

        last_slot = (n_groups - 1 + phase) % 2
        qzs_after, own_next, tops_next = lax.cond(
            last_slot == 0, functools.partial(boundary, cur=0), functools.partial(boundary, cur=1),
            state)
        return qzs_next, qzs_after, own_next, tops_next, 1 - last_slot

    qzs0 = select(0)
    qzs1 = select(1)
    tops0 = None
    for u in range(U):
        tops0 = block_scores(qzs0, u, u, 0, tops0)
    lax.fori_loop(0, NT, tile, (qzs0, qzs1, own_scores(qzs0, 0), tops0, jnp.int32(0)))


def _moba_call(qt, k, vt, kmean):
    B, S, W = k.shape
    NB = S // MOBA_BLOCK
    QB = MOBA_BLOCK
    n_hp = W // LANES
    assert NB % MOBA_UNROLL == 0 and NB <= LANES
    return pl.pallas_call(
        _moba_kernel,
        grid=(B, n_hp),
        in_specs=[
            pl.BlockSpec((1, NB, LANES, QB), lambda b, hp: (b, 0, hp, 0)),
            pl.BlockSpec((1, S, LANES), lambda b, hp: (b, 0, hp)),
            pl.BlockSpec((1, NB, LANES, MOBA_BLOCK), lambda b, hp: (b, 0, hp, 0)),
            pl.BlockSpec((1, NB, LANES), lambda b, hp: (b, 0, hp)),
        ],
        out_specs=pl.BlockSpec((1, S, LANES), lambda b, hp: (b, 0, hp)),
        out_shape=jax.ShapeDtypeStruct((B, S, W), bf16),
        scratch_shapes=[
            pltpu.VMEM((2, 2, MOBA_UNROLL * MOBA_BLOCK, QB), f32),
            pltpu.VMEM((2, (MOBA_UNROLL + 1) * MOBA_BLOCK, QB), bf16),
            pltpu.VMEM((2, 2, MOBA_BLOCK, QB), f32),
        ],
        compiler_params=pltpu.CompilerParams(
            dimension_semantics=("parallel", "parallel"),
            vmem_limit_bytes=VMEM_LIMIT),
        name="moba",
    )(qt, k, vt, kmean)


def _split3(a):
    hi = a.astype(bf16)
    r1 = a - hi.astype(f32)
    mid = r1.astype(bf16)
    lo = (r1 - mid.astype(f32)).astype(bf16)
    return hi, mid, lo


def _gla_kernel(gq_ref, gk_ref, gv_ref, gr_ref, la_ref, gn_ref, y_ref, st_ref):
    C = GLA_CHUNK
    H = GLA_HEADS
    KW = GLA_KWIDTH
    DV = GLA_VAL_DIM
    NSUB = C // GLA_SUB
    T = gq_ref.shape[1]

    @pl.when(pl.program_id(1) == 0)
    def _():
        st_ref[...] = jnp.zeros_like(st_ref)

    row = lax.broadcasted_iota(jnp.int32, (C, C), 0)
    col = lax.broadcasted_iota(jnp.int32, (C, C), 1)
    tril = (col <= row).astype(bf16)
    lane_head = lax.broadcasted_iota(jnp.int32, (C, KW), 1) // GLA_KEY_DIM
    rt = lax.broadcasted_iota(jnp.int32, (C, NSUB * C), 0)
    rc = lax.broadcasted_iota(jnp.int32, (C, NSUB * C), 1)
    keep = jnp.logical_and(rc // C == rt // GLA_SUB, rc % C <= rt)
    st_lane_head = lax.broadcasted_iota(jnp.int32, (DV, KW), 1) // GLA_KEY_DIM

    n_chunks = T // C
    chunk_rows = [slice(c * C, (c + 1) * C) for c in range(n_chunks)]

    Gs = []
    for rows in chunk_rows:
        hi, mid, lo = _split3(la_ref[0, rows, :])
        Gs.append(_dot(tril, hi) + _dot(tril, mid) + _dot(tril, lo))

    rs, upds, qz_sts, decays, vs = [], [], [], [], []
    for rows, G in zip(chunk_rows, Gs):
        q = gq_ref[0, rows, :]
        k = gk_ref[0, rows, :]
        v = gv_ref[0, rows, :].astype(bf16)
        g_last = G[C - 1:C, :]
        g_ref_rows = [G[i * GLA_SUB:i * GLA_SUB + 1, :] for i in range(NSUB)]
        g_own = jnp.concatenate(
            [jnp.broadcast_to(g, (GLA_SUB, KW)) for g in g_ref_rows], axis=0)
        q_in = q * jnp.exp(G - g_own)
        k_in = jnp.concatenate(
            [(k * jnp.exp(jnp.minimum(g - G, EXP_CAP))).astype(bf16) for g in g_ref_rows],
            axis=0)
        q_st = q * jnp.exp(G)
        k_st = (k * jnp.exp(g_last - G)).astype(bf16)
        qz_in = jnp.concatenate(
            [jnp.where(lane_head == h, q_in, 0.0).astype(bf16) for h in range(H)], axis=0)
        qz_sts.append(jnp.concatenate(
            [jnp.where(lane_head == h, q_st, 0.0).astype(bf16) for h in range(H)], axis=0))
        rs.append(_dot_nt(qz_in, k_in))
        upds.append(_dot_tn(v, k_st))
        decays.append(jnp.exp(g_last))
        vs.append(v)

    o_intras = []
    for r, v in zip(rs, vs):
        per_head = []
        for h in range(H):
            r_h = jnp.where(keep, r[h * C:(h + 1) * C, :], 0.0).astype(bf16)
            v_rep = jnp.concatenate([v[:, h * DV:(h + 1) * DV]] * NSUB, axis=0)
            per_head.append(_dot(r_h, v_rep))
        o_intras.append(per_head)

    st = st_ref[...]
    o_inters = []
    for qz_st, upd, decay in zip(qz_sts, upds, decays):
        o_inters.append(_dot_nt(qz_st, st.astype(bf16)))
        st = st * decay
        for h in range(H):
            st = st + jnp.where(st_lane_head == h, upd[h * DV:(h + 1) * DV, :], 0.0)
    st_ref[...] = st

    for rows, o_intra, o_inter in zip(chunk_rows, o_intras, o_inters):
        for h in range(H):
            o = o_intra[h] + o_inter[h * C:(h + 1) * C, :]
            o = o * lax.rsqrt(jnp.mean(o * o, axis=-1, keepdims=True) + EPS)
            o = o * gn_ref[:, h * DV:(h + 1) * DV]
            gr = gr_ref[0, rows, h * DV:(h + 1) * DV]
            y = o * (gr * jax.nn.sigmoid(gr))
            y_ref[0, rows, h * DV:(h + 1) * DV] = y.astype(y_ref.dtype)


def _gla_call(gq, gk, gv, gr, la, gn):
    B, S, _ = gq.shape
    T = GLA_TILE
    tok = lambda w: pl.BlockSpec((1, T, w), lambda b, i: (b, i, 0))
    return pl.pallas_call(
        _gla_kernel,
        grid=(B, S // T),
        in_specs=[tok(GLA_KWIDTH), tok(GLA_KWIDTH), tok(GLA_VWIDTH), tok(GLA_VWIDTH),
                  tok(GLA_KWIDTH), pl.BlockSpec((1, GLA_VWIDTH), lambda b, i: (0, 0))],
        out_specs=tok(GLA_VWIDTH),
        out_shape=jax.ShapeDtypeStruct((B, S, GLA_VWIDTH), bf16),
        scratch_shapes=[pltpu.VMEM((GLA_VAL_DIM, GLA_KWIDTH), f32)],
        compiler_params=pltpu.CompilerParams(
            dimension_semantics=("parallel", "arbitrary"), vmem_limit_bytes=VMEM_LIMIT),
        name="gla",
    )(gq, gk, gv, gr, la, gn)


def _ffn_kernel(x_ref, ym_ref, yg_ref, wo_m_ref, wo_g_ref, fg_ref, wup_ref, cw_ref, cb_ref,
                wdn_ref, og_ref, out_ref, u_ref):
    T = x_ref.shape[1]
    PAD = 8

    @pl.when(pl.program_id(1) == 0)
    def _():
        u_ref[0:PAD, :] = jnp.zeros((PAD, u_ref.shape[1]), f32)

    h = x_ref[0] + _dot(ym_ref[0], wo_m_ref[...]) + _dot(yg_ref[0], wo_g_ref[...])
    hn = _rms(h, fg_ref[...]).astype(bf16)
    u_ref[PAD:PAD + T, :] = _dot(hn, wup_ref[...])
    cw = cw_ref[...]
    uu = u_ref[...]
    conv = (cw[0:1] * pltpu.roll(uu, 2, 0)[PAD:]
            + cw[1:2] * pltpu.roll(uu, 1, 0)[PAD:]
            + cw[2:3] * uu[PAD:]
            + cb_ref[...])
    u_ref[0:PAD, :] = u_ref[T:T + PAD, :]
    hg = conv[:, :D_FF]
    act = (hg * jax.nn.sigmoid(hg) * conv[:, D_FF:]).astype(bf16)
    y = h + _dot(act, wdn_ref[...])
    out_ref[0] = _rms(y, og_ref[...])


def _ffn_call(x, ym, yg, wo_m, wo_g, fg, wup, cw, cb, wdn, og):
    B, S, D = x.shape
    T = FFN_TILE
    tok = lambda w: pl.BlockSpec((1, T, w), lambda b, i: (b, i, 0))
    const = lambda a: pl.BlockSpec(a.shape, lambda b, i: (0,) * a.ndim, pipeline_mode=pl.Buffered(1))
    return pl.pallas_call(
        _ffn_kernel,
        grid=(B, S // T),
        in_specs=[tok(D), tok(MOBA_WIDTH), tok(GLA_VWIDTH),
                  const(wo_m), const(wo_g), const(fg), const(wup), const(cw), const(cb),
                  const(wdn), const(og)],
        out_specs=tok(D),
        out_shape=jax.ShapeDtypeStruct((B, S, D), x.dtype),
        scratch_shapes=[pltpu.VMEM((T + 8, 2 * D_FF), f32)],
        compiler_params=pltpu.CompilerParams(
            dimension_semantics=("parallel", "arbitrary"), vmem_limit_bytes=VMEM_LIMIT),
        name="out_ffn",
    )(x, ym, yg, wo_m, wo_g, fg, wup, cw, cb, wdn, og)


def _rope_tables(S):
    hd = MOBA_HEAD_DIM
    inv_freq = 1.0 / (ROPE_THETA ** (jnp.arange(0, hd, 2, dtype=f32) / hd))
    ang = jnp.arange(S).astype(f32)[:, None] * inv_freq[None, :]
    cos, sin = jnp.cos(ang), jnp.sin(ang)
    cos_r = jnp.tile(cos, (1, LANES // (hd // 2)))
    sin_r = jnp.tile(jnp.concatenate([-sin, sin], axis=1), (1, LANES // hd))
    return cos.T, sin.T, cos_r, sin_r


def kernel(x, attn_norm_g, w_in, w_gate_up, b_gate, gla_norm_g, w_out, ffn_norm_g, w_ffn_up,
           conv_w, conv_b, w_ffn_down, final_norm_g):
    B, S, D = x.shape
    assert w_in.shape[0] == 1, "single-layer block"
    assert D == D_MODEL and conv_w.shape[1] == CONV_WIDTH and w_ffn_down.shape[1] == D_FF
    assert S % PROJ_TILE == 0 and S % GLA_TILE == 0 and S % FFN_TILE == 0
    l = 0
    o_mk = MOBA_WIDTH
    o_mv = 2 * MOBA_WIDTH
    o_gq = 3 * MOBA_WIDTH
    o_gg = o_gq + 2 * GLA_KWIDTH + 2 * GLA_VWIDTH
    w = w_in[l]
    wqt = w[:, :o_mk].T.astype(bf16)
    wk = w[:, o_mk:o_mv].astype(bf16)
    wvt = jnp.concatenate([w[:, o_mv:o_gq], w[:, o_gg:]], axis=1).T.astype(bf16)
    wg = w[:, o_gq:o_gg].astype(bf16)
    wgu = w_gate_up[l]
    wgu_hi = wgu.astype(bf16)
    wgu_lo = (wgu - wgu_hi.astype(f32)).astype(bf16)
    cos_t, sin_t, cos_r, sin_r = _rope_tables(S)

    qt, k, vt, kmean, gq, gk, gv, gr, la = _proj_call(
        x, attn_norm_g[l][None, :], wqt, wk, wvt, wg, wgu_hi, wgu_lo, b_gate[l][None, :],
        cos_t, sin_t, cos_r, sin_r)
    kmean = kmean.reshape(B, S // MOBA_BLOCK, MOBA_WIDTH)

    y_moba = _moba_call(qt, k, vt, kmean)
    y_gla = _gla_call(gq, gk, gv, gr, la, gla_norm_g[l].reshape(1, GLA_VWIDTH))

    wo = w_out[l].astype(bf16)
    return _ffn_call(
        x, y_moba, y_gla, wo[:MOBA_WIDTH], wo[MOBA_WIDTH:], ffn_norm_g[l][None, :],
        w_ffn_up[l].astype(bf16), conv_w[l], conv_b[l][None, :], w_ffn_down[l].astype(bf16),
        final_norm_g[None, :])
```
